```python
import math
import jax, jax.numpy as jnp
from jax import lax
import numpy as np

D_MODEL = 1024
BATCH = 1
SEQ = 16384
DEPTH = 2
DEC_BATCH = 8
DEC_SEQ = 32
PAST_LEN = 2048

CHUNK = 64
QBLOCK = 128
N_EVEN = (DEPTH + 1) // 2
N_ODD = DEPTH // 2
D_FF = 4 * D_MODEL
EPS = 1e-6
NEG = -1e30
F32 = jnp.float32

MLA_HEADS = 8
MLA_Q_RANK = 256
MLA_KV_RANK = 128
MLA_NOPE = 64
MLA_ROPE = 32
MLA_V = 64
ROPE_BASE = 10000.0
DIFF_HEADS = 4
DIFF_QK = 64
DIFF_V = 2 * DIFF_QK
T5_BUCKETS = 32
T5_MAX_DIST = 128
SB_HEADS = 8
SB_DIM = 64
CA_HEADS = 8
CA_DIM = 64
CA_LEFT_CHUNKS = 8
CA_BAND = CA_LEFT_CHUNKS * CHUNK
CA_MAX_REL = 128

EVEN_SIZES = [MLA_Q_RANK, MLA_KV_RANK, MLA_ROPE, DIFF_HEADS * 2 * DIFF_QK, DIFF_HEADS * 2 * DIFF_QK, DIFF_HEADS * DIFF_V]
EVEN_IN = sum(EVEN_SIZES)
EVEN_MIX = MLA_HEADS * MLA_V + DIFF_HEADS * DIFF_V
ODD_SIZES = [SB_HEADS * SB_DIM] * 3 + [CA_HEADS * CA_DIM] * 3
ODD_IN = sum(ODD_SIZES)
ODD_MIX = SB_HEADS * SB_DIM + CA_HEADS * CA_DIM

kernel_name = 'hybrid_chunk_stream_encoder_step'


def split_cols(a, sizes):
    out, start = [], 0
    for s in sizes:
        out.append(a[..., start:start + s])
        start += s
    return out


def rmsnorm(x, g):
    xf = x.astype(F32)
    y = xf * lax.rsqrt(jnp.mean(xf * xf, axis=-1, keepdims=True) + EPS)
    return (y * g.astype(F32)).astype(x.dtype)


def rope(x, pos):
    half = x.shape[-1] // 2
    inv = ROPE_BASE ** (-jnp.arange(half, dtype=F32) / half)
    ang = pos.astype(F32)[:, None] * inv[None, :]
    shp = (pos.shape[0],) + (1,) * (x.ndim - 3) + (half,)
    cos = jnp.cos(ang).reshape(shp)
    sin = jnp.sin(ang).reshape(shp)
    xf = x.astype(F32)
    x1, x2 = xf[..., :half], xf[..., half:]
    return jnp.concatenate([x1 * cos - x2 * sin, x2 * cos + x1 * sin], axis=-1).astype(x.dtype)


def t5_bucket(rel):
    nb = T5_BUCKETS // 2
    max_exact = nb // 2
    ret = jnp.where(rel > 0, nb, 0)
    n = jnp.abs(rel)
    nf = jnp.maximum(n, 1).astype(F32)
    large = max_exact + (jnp.log(nf / max_exact) / math.log(T5_MAX_DIST / max_exact) * (nb - max_exact)).astype(jnp.int32)
    large = jnp.minimum(large, nb - 1)
    return ret + jnp.where(n < max_exact, n, large)


def chunk_causal(q_pos, k_pos):
    return (k_pos[None, :] // CHUNK) <= (q_pos[:, None] // CHUNK)


def sweep_queries(fn, q_args, q_pos):
    sq = q_pos.shape[0]
    if sq <= QBLOCK or sq % QBLOCK:
        return fn(q_args, q_pos)
    nb = sq // QBLOCK
    blocks = tuple(jnp.moveaxis(a.reshape((a.shape[0], nb, QBLOCK) + a.shape[2:]), 1, 0) for a in q_args)
    out = lax.map(lambda bp: fn(bp[0], bp[1]), (blocks, q_pos.reshape(nb, QBLOCK)))
    out = jnp.moveaxis(out, 0, 1)
    return out.reshape((out.shape[0], sq) + out.shape[3:])


def mla_attend(c_q, q_pos, ckv, krope, k_pos, w_uq, w_ukv):
    b, s = c_q.shape[:2]
    q = (c_q @ w_uq).reshape(b, s, MLA_HEADS, MLA_NOPE + MLA_ROPE)
    q_nope, q_rope = q[..., :MLA_NOPE], rope(q[..., MLA_NOPE:], q_pos)
    kv = (ckv @ w_ukv).reshape(b, ckv.shape[1], MLA_HEADS, MLA_NOPE + MLA_V)
    k_nope, v = kv[..., :MLA_NOPE], kv[..., MLA_NOPE:]
    scale = (MLA_NOPE + MLA_ROPE) ** -0.5

    def block(qs, qp):
        qn, qr = qs
        sc = (jnp.einsum('bqhd,bkhd->bhqk', qn, k_nope, preferred_element_type=F32)
              + jnp.einsum('bqhr,bkr->bhqk', qr, krope, preferred_element_type=F32)) * scale
        sc = jnp.where(chunk_causal(qp, k_pos), sc, NEG)
        p = jax.nn.softmax(sc, axis=-1).astype(v.dtype)
        return jnp.einsum('bhqk,bkhd->bqhd', p, v)

    o = sweep_queries(block, (q_nope, q_rope), q_pos)
    return o.reshape(b, s, MLA_HEADS * MLA_V)


def diff_lambda(lam_vecs, lam_init):
    lv = lam_vecs.astype(F32)
    return jnp.exp(jnp.sum(lv[0] * lv[1])) - jnp.exp(jnp.sum(lv[2] * lv[3])) + lam_init


def diff_attend(q, q_pos, k, v, k_pos, lam, lam_init, subln, t5_table):
    b, s = q.shape[:2]
    scale = DIFF_QK ** -0.5

    def block(qs, qp):
        (qb,) = qs
        sc = jnp.einsum('bqhmd,bkhmd->bmhqk', qb, k, preferred_element_type=F32) * scale
        bias = jnp.moveaxis(t5_table[t5_bucket(k_pos[None, :] - qp[:, None])], -1, 0).astype(F32)
        sc = jnp.where(chunk_causal(qp, k_pos), sc + bias, NEG)
        p = jax.nn.softmax(sc, axis=-1)
        a = (p[:, 0] - lam * p[:, 1]).astype(v.dtype)
        return jnp.einsum('bhqk,bkhd->bqhd', a, v)

    o = sweep_queries(block, (q,), q_pos)
    o = rmsnorm(o, subln) * (1.0 - lam_init)
    return o.reshape(b, s, DIFF_HEADS * DIFF_V)


def stick_breaking(q, q_pos, k, v, k_pos):
    b, s = q.shape[:2]
    scale = SB_DIM ** -0.5

    def block(qs, qp):
        (qb,) = qs
        z = jnp.einsum('bqhd,bkhd->bhqk', qb, k, preferred_element_type=F32) * scale
        mask = k_pos[None, :] < qp[:, None]
        log_beta = jax.nn.log_sigmoid(z)
        log_1m = jnp.where(mask, jax.nn.log_sigmoid(-z), 0.0)
        nxt = jnp.concatenate([log_1m[..., 1:], jnp.zeros_like(log_1m[..., :1])], axis=-1)
        after = lax.cumsum(nxt, axis=3, reverse=True)
        a = jnp.where(mask, jnp.exp(log_beta + after), 0.0).astype(v.dtype)
        return jnp.einsum('bhqk,bkhd->bqhd', a, v)

    o = sweep_queries(block, (q,), q_pos)
    return o.reshape(b, s, SB_HEADS * SB_DIM)


def ca_bias(table, rel):
    idx = jnp.clip(rel, -CA_MAX_REL, CA_MAX_REL) + CA_MAX_REL
    return jnp.moveaxis(table[idx], -1, 0).astype(F32)


def chunk_band_prompt(q, k, v, table):
    b, s, h, dh = q.shape
    nc = s // CHUNK
    nb = CA_LEFT_CHUNKS + 1
    pad = ((0, 0), (CA_BAND, 0), (0, 0), (0, 0))
    kc = jnp.pad(k, pad).reshape(b, nc + CA_LEFT_CHUNKS, CHUNK, h, dh)
    vc = jnp.pad(v, pad).reshape(b, nc + CA_LEFT_CHUNKS, CHUNK, h, dh)
    idx = jnp.arange(nc)[:, None] + jnp.arange(nb)[None, :]
    k_band = kc[:, idx].reshape(b, nc, nb * CHUNK, h, dh)
    v_band = vc[:, idx].reshape(b, nc, nb * CHUNK, h, dh)
    qc = q.reshape(b, nc, CHUNK, h, dh)
    sc = jnp.einsum('bcqhd,bckhd->bchqk', qc, k_band, preferred_element_type=F32) * (dh ** -0.5)
    qi = jnp.arange(CHUNK)
    kj = jnp.arange(nb * CHUNK)
    sc = sc + ca_bias(table, kj[None, :] - CA_BAND - qi[:, None])
    valid = (jnp.arange(nc)[:, None] - CA_LEFT_CHUNKS) * CHUNK + kj[None, :] >= 0
    sc = jnp.where(valid[None, :, None, None, :], sc, NEG)
    p = jax.nn.softmax(sc, axis=-1).astype(v.dtype)
    o = jnp.einsum('bchqk,bckhd->bcqhd', p, v_band)
    return o.reshape(b, s, h * dh)


def chunk_band_direct(q, q_pos, k, v, k_pos, table):
    b, s, h, dh = q.shape
    sc = jnp.einsum('bqhd,bkhd->bhqk', q, k, preferred_element_type=F32) * (dh ** -0.5)
    sc = sc + ca_bias(table, k_pos[None, :] - q_pos[:, None])
    qc, kc = q_pos[:, None] // CHUNK, k_pos[None, :] // CHUNK
    mask = (kc <= qc) & (kc >= qc - CA_LEFT_CHUNKS)
    p = jax.nn.softmax(jnp.where(mask, sc, NEG), axis=-1).astype(v.dtype)
    return jnp.einsum('bhqk,bkhd->bqhd', p, v).reshape(b, s, h * dh)


def even_mixer(h, pos, past, past_pos, w_in, q_norm, kv_norm, w_uq, w_ukv, lam_vecs, subln, t5_table, w_out, lam_init):
    b, s, _ = h.shape
    c_q, c_kv, k_r, dq, dk, dv = split_cols(h @ w_in, EVEN_SIZES)
    c_q = rmsnorm(c_q, q_norm)
    c_kv = rmsnorm(c_kv, kv_norm)
    k_r = rope(k_r, pos)
    dk = dk.reshape(b, s, DIFF_HEADS, 2 * DIFF_QK)
    dv = dv.reshape(b, s, DIFF_HEADS, DIFF_V)
    new = (c_kv, k_r, dk, dv)
    if past is None:
        ckv_all, kr_all, dk_all, dv_all = new
        k_pos = pos
    else:
        ckv_all, kr_all, dk_all, dv_all = (jnp.concatenate([pa, nw], axis=1) for pa, nw in zip(past, new))
        k_pos = jnp.concatenate([past_pos, pos])
    o_mla = mla_attend(c_q, pos, ckv_all, kr_all, k_pos, w_uq, w_ukv)
    lam = diff_lambda(lam_vecs, lam_init)
    o_diff = diff_attend(dq.reshape(b, s, DIFF_HEADS, 2, DIFF_QK), pos,
                         dk_all.reshape(b, dk_all.shape[1], DIFF_HEADS, 2, DIFF_QK), dv_all, k_pos,
                         lam, lam_init, subln, t5_table)
    out = jnp.concatenate([o_mla, o_diff], axis=-1) @ w_out
    return out, new


def odd_mixer(h, pos, past, past_pos, band_pos, w_in, ca_table, w_out):
    b, s, _ = h.shape
    sq, sk, sv, cq, ck, cv = split_cols(h @ w_in, ODD_SIZES)
    sq, sk, sv = (a.reshape(b, s, SB_HEADS, SB_DIM) for a in (sq, sk, sv))
    cq, ck, cv = (a.reshape(b, s, CA_HEADS, CA_DIM) for a in (cq, ck, cv))
    if past is None:
        o_sb = stick_breaking(sq, pos, sk, sv, pos)
        o_ca = chunk_band_prompt(cq, ck, cv, ca_table)
        nb = min(CA_BAND, s)
        new = (sk, sv, ck[:, s - nb:], cv[:, s - nb:])
    else:
        sbk_p, sbv_p, cak_p, cav_p = past
        o_sb = stick_breaking(sq, pos, jnp.concatenate([sbk_p, sk], axis=1), jnp.concatenate([sbv_p, sv], axis=1),
                              jnp.concatenate([past_pos, pos]))
        k_all = jnp.concatenate([cak_p, ck], axis=1)
        v_all = jnp.concatenate([cav_p, cv], axis=1)
        o_ca = chunk_band_direct(cq, pos, k_all, v_all, jnp.concatenate([band_pos, pos]), ca_table)
        nb = cak_p.shape[1]
        new = (sk, sv, k_all[:, k_all.shape[1] - nb:], v_all[:, v_all.shape[1] - nb:])
    out = jnp.concatenate([o_sb, o_ca], axis=-1) @ w_out
    return out, new


def trunk(x, pos, caches, past_pos, band_pos, weights):
    (norm_mix, norm_ff, w_in_even, mla_q_norm, mla_kv_norm, mla_w_uq, mla_w_ukv, diff_lambda_vecs,
     diff_subln, t5_bias, w_out_even, w_in_odd, ca_rel_bias, w_out_odd, w_ff1, w_ff2, final_norm) = weights
    new_even, new_odd = [], []
    for l in range(DEPTH):
        i = l // 2
        h = rmsnorm(x, norm_mix[l])
        if l % 2 == 0:
            past = None if caches is None else tuple(c[i] for c in caches[:4])
            lam_init = 0.8 - 0.6 * math.exp(-0.3 * l)
            out, new = even_mixer(h, pos, past, past_pos, w_in_even[i], mla_q_norm[i], mla_kv_norm[i],
                                  mla_w_uq[i], mla_w_ukv[i], diff_lambda_vecs[i], diff_subln[i], t5_bias,
                                  w_out_even[i], lam_init)
            new_even.append(new)
        else:
            past = None if caches is None else tuple(c[i] for c in caches[4:])
            out, new = odd_mixer(h, pos, past, past_pos, band_pos, w_in_odd[i], ca_rel_bias[i], w_out_odd[i])
            new_odd.append(new)
        x = x + out
        h = rmsnorm(x, norm_ff[l])
        x = x + jnp.square(jax.nn.relu(h @ w_ff1[l])) @ w_ff2[l]
    y = rmsnorm(x, final_norm)
    stacked_even = [jnp.stack(rows) for rows in zip(*new_even)]
    stacked_odd = [jnp.stack(rows) for rows in zip(*new_odd)]
    return y, stacked_even + stacked_odd


def setup_inputs(seed: int = 0) -> dict:
    key = jax.random.key(seed)
    ks = iter(jax.random.split(key, 40))

    def nrm(shape, scale):
        return jax.random.normal(next(ks), shape, F32) * scale

    def gain(shape):
        return 1.0 + nrm(shape, 0.02)

    band = min(CA_BAND, PAST_LEN)
    return {
        'x_prompt': nrm((BATCH, SEQ, D_MODEL), 1.0),
        'x_sample': nrm((DEC_BATCH, DEC_SEQ, D_MODEL), 1.0),
        'cache_mla_ckv': nrm((N_EVEN, DEC_BATCH, PAST_LEN, MLA_KV_RANK), 1.0),
        'cache_mla_krope': nrm((N_EVEN, DEC_BATCH, PAST_LEN, MLA_ROPE), 1.0),
        'cache_diff_k': nrm((N_EVEN, DEC_BATCH, PAST_LEN, DIFF_HEADS, 2 * DIFF_QK), 1.0),
        'cache_diff_v': nrm((N_EVEN, DEC_BATCH, PAST_LEN, DIFF_HEADS, DIFF_V), 1.0),
        'cache_sb_k': nrm((N_ODD, DEC_BATCH, PAST_LEN, SB_HEADS, SB_DIM), 1.0),
        'cache_sb_v': nrm((N_ODD, DEC_BATCH, PAST_LEN, SB_HEADS, SB_DIM), 1.0),
        'cache_ca_k': nrm((N_ODD, DEC_BATCH, band, CA_HEADS, CA_DIM), 1.0),
        'cache_ca_v': nrm((N_ODD, DEC_BATCH, band, CA_HEADS, CA_DIM), 1.0),
        'norm_mix': gain((DEPTH, D_MODEL)),
        'norm_ff': gain((DEPTH, D_MODEL)),
        'w_in_even': nrm((N_EVEN, D_MODEL, EVEN_IN), D_MODEL ** -0.5),
        'mla_q_norm': gain((N_EVEN, MLA_Q_RANK)),
        'mla_kv_norm': gain((N_EVEN, MLA_KV_RANK)),
        'mla_w_uq': nrm((N_EVEN, MLA_Q_RANK, MLA_HEADS * (MLA_NOPE + MLA_ROPE)), MLA_Q_RANK ** -0.5),
        'mla_w_ukv': nrm((N_EVEN, MLA_KV_RANK, MLA_HEADS * (MLA_NOPE + MLA_V)), MLA_KV_RANK ** -0.5),
        'diff_lambda_vecs': nrm((N_EVEN, 4, DIFF_QK), 0.1),
        'diff_subln': gain((N_EVEN, DIFF_V)),
        't5_bias': nrm((T5_BUCKETS, DIFF_HEADS), 0.5),
        'w_out_even': nrm((N_EVEN, EVEN_MIX, D_MODEL), EVEN_MIX ** -0.5),
        'w_in_odd': nrm((N_ODD, D_MODEL, ODD_IN), D_MODEL ** -0.5),
        'ca_rel_bias': nrm((N_ODD, 2 * CA_MAX_REL + 1, CA_HEADS), 0.5),
        'w_out_odd': nrm((N_ODD, ODD_MIX, D_MODEL), ODD_MIX ** -0.5),
        'w_ff1': nrm((DEPTH, D_MODEL, D_FF), D_MODEL ** -0.5),
        'w_ff2': nrm((DEPTH, D_FF, D_MODEL), D_FF ** -0.5),
        'final_norm': gain((D_MODEL,)),
    }


def reference(x_prompt, x_sample, cache_mla_ckv, cache_mla_krope, cache_diff_k, cache_diff_v,
              cache_sb_k, cache_sb_v, cache_ca_k, cache_ca_v,
              norm_mix, norm_ff, w_in_even, mla_q_norm, mla_kv_norm, mla_w_uq, mla_w_ukv,
              diff_lambda_vecs, diff_subln, t5_bias, w_out_even, w_in_odd, ca_rel_bias, w_out_odd,
              w_ff1, w_ff2, final_norm):
    weights = (norm_mix, norm_ff, w_in_even, mla_q_norm, mla_kv_norm, mla_w_uq, mla_w_ukv, diff_lambda_vecs,
               diff_subln, t5_bias, w_out_even, w_in_odd, ca_rel_bias, w_out_odd, w_ff1, w_ff2, final_norm)
    seq = x_prompt.shape[1]
    dec_seq = x_sample.shape[1]
    past_len = cache_mla_ckv.shape[2]
    band_len = cache_ca_k.shape[2]
    pos_p = jnp.arange(seq, dtype=jnp.int32)
    pos_s = past_len + jnp.arange(dec_seq, dtype=jnp.int32)
    past_pos = jnp.arange(past_len, dtype=jnp.int32)
    band_pos = jnp.arange(past_len - band_len, past_len, dtype=jnp.int32)

    y_prompt, new_p = trunk(x_prompt, pos_p, None, None, None, weights)
    caches = (cache_mla_ckv, cache_mla_krope, cache_diff_k, cache_diff_v,
              cache_sb_k, cache_sb_v, cache_ca_k, cache_ca_v)
    y_sample, new_s = trunk(x_sample, pos_s, caches, past_pos, band_pos, weights)

    p_ckv, p_krope, p_dk, p_dv, p_sbk, p_sbv, p_cak, p_cav = new_p
    s_ckv, s_krope, s_dk, s_dv, s_sbk, s_sbv, s_cak, s_cav = new_s
    return (y_prompt, y_sample, p_ckv, p_krope, p_dk, p_dv, p_sbk, p_sbv, p_cak, p_cav,
            s_ckv, s_krope, s_dk, s_dv, s_sbk, s_sbv, s_cak, s_cav)
```

```python
import functools
import math

import numpy as np
import jax
import jax.numpy as jnp
from jax import lax
from jax.experimental import pallas as pl
from jax.experimental.pallas import tpu as pltpu

F32 = jnp.float32
BF16 = jnp.bfloat16

D_MODEL = 1024
CHUNK = 64
CHUNK_SHIFT = 6
EPS = 1e-6
NEG = -1e30

MLA_HEADS = 8
MLA_Q_RANK = 256
MLA_KV_RANK = 128
MLA_NOPE = 64
MLA_ROPE = 32
MLA_V = 64
ROPE_BASE = 10000.0
DIFF_HEADS = 4
DIFF_QK = 64
DIFF_V = 2 * DIFF_QK
T5_BUCKETS = 32
T5_MAX_DIST = 128
SB_HEADS = 8
SB_DIM = 64
CA_HEADS = 8
CA_DIM = 64
CA_LEFT_CHUNKS = 8
CA_BAND = CA_LEFT_CHUNKS * CHUNK
CA_MAX_REL = 128

LANES = 128
VMEM_LIMIT = 48 * 1024 * 1024
SB_EXIT = -104.0


def _cparams(n_axes):
    return pltpu.CompilerParams(dimension_semantics=("arbitrary",) * n_axes,
                                vmem_limit_bytes=VMEM_LIMIT)


def _rms(x, g):
    return x * lax.rsqrt(jnp.mean(x * x, axis=-1, keepdims=True) + EPS) * g


def _dot(a, b):
    return jnp.dot(a, b, preferred_element_type=F32)


def _dot_nt(a, b):
    return lax.dot_general(a, b, (((1,), (1,)), ((), ())), preferred_element_type=F32)


def _keep_lanes(q, keep):
    return jnp.where(keep, q.astype(F32), 0.0).astype(BF16)


def _full(shape):
    n = len(shape)
    return pl.BlockSpec(shape, lambda *_: (0,) * n)


def _rows(tm, width):
    return pl.BlockSpec((tm, width), lambda i: (i, 0))


def _even_proj_kernel(x_ref, g_ref, wcq_ref, wckv_ref, wkr_ref, wkrs_ref, wdq_ref, wdk_ref, wdv_ref,
                      qn_ref, kvn_ref, wq_ref, wqs_ref, cosq_ref, sinq_ref, cosk_ref, sink_ref,
                      qext_ref, ckv_ref, kr_ref, dq_ref, dk_ref, dkb_ref, dv_ref, dvb_ref):
    hn = _rms(x_ref[...], g_ref[...]).astype(BF16)
    cq = _rms(_dot(hn, wcq_ref[...]), qn_ref[...]).astype(BF16)
    cosq = jnp.concatenate([cosq_ref[...]] * MLA_HEADS, axis=1)
    sinq = jnp.concatenate([sinq_ref[...]] * MLA_HEADS, axis=1)
    qext_ref[...] = (_dot(cq, wq_ref[...]) * cosq + _dot(cq, wqs_ref[...]) * sinq).astype(BF16)
    ckv_ref[...] = _rms(_dot(hn, wckv_ref[...]), kvn_ref[...])
    kr_ref[...] = _dot(hn, wkr_ref[...]) * cosk_ref[...] + _dot(hn, wkrs_ref[...]) * sink_ref[...]
    dq_ref[...] = (_dot(hn, wdq_ref[...]) * (DIFF_QK ** -0.5)).astype(BF16)
    dk = _dot(hn, wdk_ref[...])
    dk_ref[...] = dk
    dkb_ref[...] = dk.astype(BF16)
    dv = _dot(hn, wdv_ref[...])
    dv_ref[...] = dv
    dvb_ref[...] = dv.astype(BF16)


def _even_proj(x, g, w, tabs, tm):
    m = x.shape[0]
    ins = [x, g, w["wcq"], w["wckv"], w["wkr"], w["wkrs"], w["wdq"], w["wdk"], w["wdv"],
           w["qn"], w["kvn"], w["wq"], w["wqs"], tabs["cosq"], tabs["sinq"], tabs["cosk"], tabs["sink"]]
    row_in = {0: D_MODEL, 13: LANES, 14: LANES, 15: MLA_ROPE, 16: MLA_ROPE}
    in_specs = [_rows(tm, row_in[i]) if i in row_in else _full(a.shape) for i, a in enumerate(ins)]
    outs = [(MLA_HEADS * LANES, BF16), (MLA_KV_RANK, F32), (MLA_ROPE, F32),
            (DIFF_HEADS * DIFF_V, BF16), (DIFF_HEADS * DIFF_V, F32), (DIFF_HEADS * DIFF_V, BF16),
            (DIFF_HEADS * DIFF_V, F32), (DIFF_HEADS * DIFF_V, BF16)]
    return pl.pallas_call(
        _even_proj_kernel,
        grid=(m // tm,),
        in_specs=in_specs,
        out_specs=[_rows(tm, n) for n, _ in outs],
        out_shape=[jax.ShapeDtypeStruct((m, n), dt) for n, dt in outs],
        compiler_params=_cparams(1),
        name="even_proj",
    )(*ins)


def _odd_proj_kernel(x_ref, g_ref, w_ref, sq_ref, sk_ref, skb_ref, sv_ref, svb_ref,
                     cq_ref, ck_ref, ckb_ref, cv_ref, cvb_ref):
    hn = _rms(x_ref[...], g_ref[...]).astype(BF16)
    width = SB_HEADS * SB_DIM

    def seg(i):
        return _dot(hn, w_ref[:, i * width:(i + 1) * width])

    sq_ref[...] = (seg(0) * (SB_DIM ** -0.5)).astype(BF16)
    for i, (f_ref, b_ref) in ((1, (sk_ref, skb_ref)), (2, (sv_ref, svb_ref)),
                              (4, (ck_ref, ckb_ref)), (5, (cv_ref, cvb_ref))):
        y = seg(i)
        f_ref[...] = y
        b_ref[...] = y.astype(BF16)
    cq_ref[...] = (seg(3) * (CA_DIM ** -0.5)).astype(BF16)


def _odd_proj(x, g, w, tm):
    m = x.shape[0]
    width = SB_HEADS * SB_DIM
    dts = [BF16, F32, BF16, F32, BF16, BF16, F32, BF16, F32, BF16]
    return pl.pallas_call(
        _odd_proj_kernel,
        grid=(m // tm,),
        in_specs=[_rows(tm, D_MODEL), _full(g.shape), _full(w.shape)],
        out_specs=[_rows(tm, width) for _ in dts],
        out_shape=[jax.ShapeDtypeStruct((m, width), dt) for dt in dts],
        compiler_params=_cparams(1),
        name="odd_proj",
    )(x, g, w)


def _kv_up_kernel(ckv_ref, kr_ref, wk_ref, wv_ref, place_ref, ones_ref, kext_ref, vext_ref):
    c = ckv_ref[...].astype(BF16)
    r = kr_ref[...].astype(BF16)
    kext_ref[...] = (_dot(c, wk_ref[...]) + _dot(r, place_ref[...])).astype(BF16)
    vext_ref[...] = (_dot(c, wv_ref[...]) + ones_ref[...]).astype(BF16)


def _kv_up(ckv, kr, w, tm):
    m = ckv.shape[0]
    width = MLA_HEADS * LANES
    return pl.pallas_call(
        _kv_up_kernel,
        grid=(m // tm,),
        in_specs=[_rows(tm, MLA_KV_RANK), _rows(tm, MLA_ROPE), _full(w["wk"].shape), _full(w["wv"].shape),
                  _full(w["place"].shape), _full(w["ones"].shape)],
        out_specs=[_rows(tm, width)] * 2,
        out_shape=[jax.ShapeDtypeStruct((m, width), BF16)] * 2,
        compiler_params=_cparams(1),
        name="mla_kv_up",
    )(ckv, kr, w["wk"], w["wv"], w["place"], w["ones"])


def _attn_specs(tq, skp):
    q_spec = pl.BlockSpec((None, tq, LANES), lambda b, h, qi: (b, qi, h))
    kv_spec = pl.BlockSpec((None, skp, LANES), lambda b, h, qi: (b, 0, h))
    return q_spec, kv_spec


def _softmax_block(s, m_ref, scale_acc):
    m_old = m_ref[...]
    m_new = jnp.maximum(m_old, jnp.max(s, axis=-1, keepdims=True))
    scale_acc(jnp.exp(m_old - m_new))
    m_ref[...] = m_new
    return jnp.exp(s - m_new)


def _mla_kernel(q_ref, k_ref, v_ref, o_ref, m_ref, acc_ref, *, tq, tk, q_off, sk_real, n_near):
    qi = pl.program_id(2)
    qpos0 = q_off + qi * tq
    n_far = qpos0 // tk
    q = q_ref[...]
    scale = (MLA_NOPE + MLA_ROPE) ** -0.5
    m_ref[...] = jnp.full(m_ref.shape, NEG, F32)
    acc_ref[...] = jnp.zeros(acc_ref.shape, F32)

    def block(kj, masked):
        ks = pl.multiple_of(kj * tk, tk)
        s = _dot_nt(q, k_ref[pl.ds(ks, tk), :]) * scale
        if masked:
            qp = qpos0 + lax.broadcasted_iota(jnp.int32, (tq, tk), 0)
            kp = kj * tk + lax.broadcasted_iota(jnp.int32, (tq, tk), 1)
            ok = ((kp >> CHUNK_SHIFT) <= (qp >> CHUNK_SHIFT)) & (kp < sk_real)
            s = jnp.where(ok, s, NEG)

        def scale_acc(alpha):
            acc_ref[...] = acc_ref[...] * alpha

        p = _softmax_block(s, m_ref, scale_acc)
        acc_ref[...] += _dot(p.astype(BF16), v_ref[pl.ds(ks, tk), :])

    def far(kj, carry):
        block(kj, False)
        return carry

    lax.fori_loop(0, n_far, far, 0)
    for r in range(n_near):
        block(n_far + r, True)

    acc = acc_ref[...]
    lane = lax.broadcasted_iota(jnp.int32, acc.shape, 1)
    denom = jnp.sum(jnp.where(lane == MLA_V, acc, 0.0), axis=-1, keepdims=True)
    o_ref[...] = jnp.where(lane < MLA_V, acc / denom, 0.0).astype(BF16)


def _mla_attn(qext, kext, vext, *, tq, tk, q_off, sk_real, n_near):
    b, sq, _ = qext.shape
    skp = kext.shape[1]
    q_spec, kv_spec = _attn_specs(tq, skp)
    kern = functools.partial(_mla_kernel, tq=tq, tk=tk, q_off=q_off, sk_real=sk_real, n_near=n_near)
    return pl.pallas_call(
        kern,
        grid=(b, MLA_HEADS, sq // tq),
        in_specs=[q_spec, kv_spec, kv_spec],
        out_specs=q_spec,
        out_shape=jax.ShapeDtypeStruct((b, sq, MLA_HEADS * LANES), BF16),
        scratch_shapes=[pltpu.VMEM((tq, 1), F32), pltpu.VMEM((tq, LANES), F32)],
        compiler_params=_cparams(3),
        name="mla_attn",
    )(qext, kext, vext)


def _diff_kernel(t5_ref, q_ref, k_ref, v_ref, bkt_ref, lamv_ref, subln_ref, o_ref,
                 bias_ref, m_ref, l_ref, acc_ref, *, tq, tk, q_off, n_near, near_back, far_bucket, lam_init):
    first = (pl.program_id(0) == 0) & (pl.program_id(1) == 0) & (pl.program_id(2) == 0)

    @pl.when(first)
    def _build_bias():
        for r in range(n_near):
            bkt = bkt_ref[r]
            vals = [jnp.full((tq, tk), NEG, F32) for _ in range(DIFF_HEADS)]
            for t in range(T5_BUCKETS):
                hit = bkt == t
                for hh in range(DIFF_HEADS):
                    vals[hh] = jnp.where(hit, t5_ref[t, hh] - t5_ref[far_bucket, hh], vals[hh])
            for hh in range(DIFF_HEADS):
                bias_ref[r, hh] = vals[hh]

    h = pl.program_id(1)
    qi = pl.program_id(2)
    qpos0 = q_off + qi * tq
    near0 = qpos0 // tk - near_back
    q = q_ref[...]
    lane = lax.broadcasted_iota(jnp.int32, q.shape, 1)
    qm = [_keep_lanes(q, lane < DIFF_QK), _keep_lanes(q, lane >= DIFF_QK)]
    m_ref[...] = jnp.full(m_ref.shape, NEG, F32)
    l_ref[...] = jnp.zeros(l_ref.shape, F32)
    acc_ref[...] = jnp.zeros(acc_ref.shape, F32)

    def block(kj, bias):
        ks = pl.multiple_of(kj * tk, tk)
        k = k_ref[pl.ds(ks, tk), :]
        v = v_ref[pl.ds(ks, tk), :]
        for mi in range(2):
            s = _dot_nt(qm[mi], k)
            if bias is not None:
                s = s + bias

            def scale_acc(alpha, mi=mi):
                acc_ref[mi] = acc_ref[mi] * alpha
                l_ref[mi] = l_ref[mi] * alpha

            p = _softmax_block(s, m_ref.at[mi], scale_acc)
            l_ref[mi] += jnp.sum(p, axis=-1, keepdims=True)
            acc_ref[mi] += _dot(p.astype(BF16), v)

    def far(kj, carry):
        block(kj, None)
        return carry

    lax.fori_loop(0, jnp.maximum(near0, 0), far, 0)
    for r in range(n_near):
        @pl.when(near0 + r >= 0)
        def _near(r=r):
            block(near0 + r, bias_ref[r, h])

    lv = lamv_ref[...]
    lam = (jnp.exp(jnp.sum(lv[0:1] * lv[1:2], axis=-1, keepdims=True))
           - jnp.exp(jnp.sum(lv[2:3] * lv[3:4], axis=-1, keepdims=True)) + lam_init)
    o = acc_ref[0] / l_ref[0] - lam * (acc_ref[1] / l_ref[1])
    o_ref[...] = (_rms(o, subln_ref[...]) * (1.0 - lam_init)).astype(BF16)


def _diff_attn(t5, dq, dk, dv, bkt, lamv, subln, *, tq, tk, q_off, near_back, far_bucket, lam_init):
    b, sq, _ = dq.shape
    skp = dk.shape[1]
    n_near = bkt.shape[0]
    q_spec, kv_spec = _attn_specs(tq, skp)
    kern = functools.partial(_diff_kernel, tq=tq, tk=tk, q_off=q_off, n_near=n_near, near_back=near_back,
                             far_bucket=far_bucket, lam_init=lam_init)
    return pl.pallas_call(
        kern,
        grid=(b, DIFF_HEADS, sq // tq),
        in_specs=[pl.BlockSpec(memory_space=pltpu.SMEM), q_spec, kv_spec, kv_spec,
                  _full(bkt.shape), _full(lamv.shape), _full(subln.shape)],
        out_specs=q_spec,
        out_shape=jax.ShapeDtypeStruct((b, sq, DIFF_HEADS * DIFF_V), BF16),
        scratch_shapes=[pltpu.VMEM((n_near, DIFF_HEADS, tq, tk), F32), pltpu.VMEM((2, tq, 1), F32),
                        pltpu.VMEM((2, tq, 1), F32), pltpu.VMEM((2, tq, LANES), F32)],
        compiler_params=_cparams(3),
        name="diff_attn",
    )(t5, dq, dk, dv, bkt, lamv, subln)


def _sb_kernel(q_ref, k_ref, v_ref, tri_ref, o_ref, run_ref, acc_ref, *, tq, tk, q_off, n_masked):
    qi = pl.program_id(2)
    qpos0 = q_off + qi * tq
    top = (qpos0 + tq - 1) // tk
    q = q_ref[...]
    lane = lax.broadcasted_iota(jnp.int32, q.shape, 1)
    tri = tri_ref[...]

    def block(kj, qh, masked):
        ks = pl.multiple_of(kj * tk, tk)
        z = _dot_nt(qh, k_ref[pl.ds(ks, tk), :])
        t = jnp.log1p(jnp.exp(-jnp.abs(z)))
        log_1m = -(jnp.maximum(z, 0.0) + t)
        log_beta = jnp.minimum(z, 0.0) - t
        if masked:
            qp = qpos0 + lax.broadcasted_iota(jnp.int32, (tq, tk), 0)
            kp = kj * tk + lax.broadcasted_iota(jnp.int32, (tq, tk), 1)
            ok = kp < qp
            log_1m = jnp.where(ok, log_1m, 0.0)
        hi = log_1m.astype(BF16)
        lo = (log_1m - hi.astype(F32)).astype(BF16)
        after = _dot(hi, tri) + _dot(lo, tri) + run_ref[...]
        a = jnp.exp(log_beta + after)
        if masked:
            a = jnp.where(ok, a, 0.0)
        acc_ref[...] += _dot(a.astype(BF16), v_ref[pl.ds(ks, tk), :])
        run_ref[...] += jnp.sum(log_1m, axis=-1, keepdims=True)

    outs = []
    for hh in range(2):
        qh = _keep_lanes(q, (lane < SB_DIM) if hh == 0 else (lane >= SB_DIM))
        run_ref[...] = jnp.zeros(run_ref.shape, F32)
        acc_ref[...] = jnp.zeros(acc_ref.shape, F32)
        for r in range(n_masked):
            block(top - r, qh, True)

        def cond(kj):
            return (kj >= 0) & (jnp.max(run_ref[...]) > SB_EXIT)

        def body(kj, qh=qh):
            block(kj, qh, False)
            return kj - 1

        lax.while_loop(cond, body, top - n_masked)
        outs.append(acc_ref[...])
    o_ref[...] = jnp.where(lane < SB_DIM, outs[0], outs[1]).astype(BF16)


def _sb_attn(sq_, sk_, sv_, tri, *, tq, tk, q_off, n_masked):
    b, sq, _ = sq_.shape
    skp = sk_.shape[1]
    q_spec, kv_spec = _attn_specs(tq, skp)
    kern = functools.partial(_sb_kernel, tq=tq, tk=tk, q_off=q_off, n_masked=n_masked)
    return pl.pallas_call(
        kern,
        grid=(b, SB_HEADS // 2, sq // tq),
        in_specs=[q_spec, kv_spec, kv_spec, _full(tri.shape)],
        out_specs=q_spec,
        out_shape=jax.ShapeDtypeStruct((b, sq, SB_HEADS * SB_DIM), BF16),
        scratch_shapes=[pltpu.VMEM((tq, 1), F32), pltpu.VMEM((tq, LANES), F32)],
        compiler_params=_cparams(3),
        name="sb_attn",
    )(sq_, sk_, sv_, tri)


CA_ROW_GROUP = 16


def _ca_kernel(tab_ref, q_ref, k_ref, v_ref, o_ref, bias_ref, *, tq, win, win_real, q_off):
    first = (pl.program_id(0) == 0) & (pl.program_id(1) == 0) & (pl.program_id(2) == 0)
    n_shift = -(-(tq + win) // LANES) * LANES
    n_rel = n_shift + LANES

    @pl.when(first)
    def _build_bias():
        x = lax.broadcasted_iota(jnp.int32, (CA_HEADS, n_rel), 1)
        idx = jnp.clip(x - (tq - 1) - CA_BAND, -CA_MAX_REL, CA_MAX_REL) + CA_MAX_REL
        f = jnp.zeros((CA_HEADS, n_rel), F32)
        for t in range(2 * CA_MAX_REL + 1):
            f = jnp.where(idx == t, tab_ref[:, t:t + 1], f)
        g = CA_ROW_GROUP
        i_loc = lax.broadcasted_iota(jnp.int32, (g, win), 0)
        j_loc = lax.broadcasted_iota(jnp.int32, (g, win), 1)
        for hh in range(CA_HEADS):
            fh = f[hh:hh + 1, :]
            shifted = jnp.concatenate([fh[:, g - 1 - bb:g - 1 - bb + n_shift] for bb in range(g)], axis=0)
            for a in range(tq // g):
                start = tq - g * a - g
                tile = shifted[:, start:start + win]
                i = i_loc + g * a
                kc = (j_loc >> CHUNK_SHIFT) - CA_LEFT_CHUNKS
                qc = i >> CHUNK_SHIFT
                ok = (kc <= qc) & (kc >= qc - CA_LEFT_CHUNKS) & (j_loc < win_real)
                bias_ref[hh, g * a:g * a + g, :] = jnp.where(ok, tile, NEG)

    hp = pl.program_id(1)
    qi = pl.program_id(2)
    qpos0 = q_off + qi * tq
    ws = pl.multiple_of(qi * tq, tq)
    q = q_ref[...]
    k = k_ref[pl.ds(ws, win), :]
    v = v_ref[pl.ds(ws, win), :]
    lane = lax.broadcasted_iota(jnp.int32, q.shape, 1)
    kpos = qpos0 - CA_BAND + lax.broadcasted_iota(jnp.int32, (tq, win), 1)
    outs = []
    for hh in range(2):
        qh = _keep_lanes(q, (lane < CA_DIM) if hh == 0 else (lane >= CA_DIM))
        s = _dot_nt(qh, k) + bias_ref[2 * hp + hh]
        s = jnp.where(kpos >= 0, s, NEG)
        m = jnp.max(s, axis=-1, keepdims=True)
        p = jnp.exp(s - m)
        denom = jnp.sum(p, axis=-1, keepdims=True)
        outs.append(_dot(p.astype(BF16), v) / denom)
    o_ref[...] = jnp.where(lane < CA_DIM, outs[0], outs[1]).astype(BF16)


def _ca_attn(tab_t, cq, ck, cv, *, tq, win, win_real, q_off):
    b, sq, _ = cq.shape
    skp = ck.shape[1]
    q_spec, kv_spec = _attn_specs(tq, skp)
    kern = functools.partial(_ca_kernel, tq=tq, win=win, win_real=win_real, q_off=q_off)
    return pl.pallas_call(
        kern,
        grid=(b, CA_HEADS // 2, sq // tq),
        in_specs=[_full(tab_t.shape), q_spec, kv_spec, kv_spec],
        out_specs=q_spec,
        out_shape=jax.ShapeDtypeStruct((b, sq, CA_HEADS * CA_DIM), BF16),
        scratch_shapes=[pltpu.VMEM((CA_HEADS, tq, win), F32)],
        compiler_params=_cparams(3),
        name="ca_attn",
    )(tab_t, cq, ck, cv)


def _post_kernel(x_ref, oa_ref, ob_ref, wa_ref, wb_ref, g_ref, w1_ref, w2_ref, gf_ref, y_ref,
                 x1_ref, hn_ref, acc_ref, *, final_norm):
    j = pl.program_id(1)

    @pl.when(j == 0)
    def _mix():
        x1 = x_ref[...] + _dot(oa_ref[...], wa_ref[...]) + _dot(ob_ref[...], wb_ref[...])
        x1_ref[...] = x1
        hn_ref[...] = _rms(x1, g_ref[...]).astype(BF16)
        acc_ref[...] = jnp.zeros(acc_ref.shape, F32)

    a = jnp.maximum(_dot(hn_ref[...], w1_ref[...]), 0.0)
    acc_ref[...] += _dot((a * a).astype(BF16), w2_ref[...])

    @pl.when(j == pl.num_programs(1) - 1)
    def _finish():
        y = x1_ref[...] + acc_ref[...]
        if final_norm:
            y = _rms(y, gf_ref[...])
        y_ref[...] = y


def _post(x, oa, ob, wa, wb, g, w1, w2, gf, *, tm, tf, final_norm):
    m = x.shape[0]
    d_ff = w1.shape[1]
    rows = lambda width: pl.BlockSpec((tm, width), lambda i, j: (i, 0))
    return pl.pallas_call(
        functools.partial(_post_kernel, final_norm=final_norm),
        grid=(m // tm, d_ff // tf),
        in_specs=[rows(D_MODEL), rows(oa.shape[1]), rows(ob.shape[1]), _full(wa.shape), _full(wb.shape),
                  _full(g.shape), pl.BlockSpec((D_MODEL, tf), lambda i, j: (0, j)),
                  pl.BlockSpec((tf, D_MODEL), lambda i, j: (j, 0)), _full(gf.shape)],
        out_specs=rows(D_MODEL),
        out_shape=jax.ShapeDtypeStruct((m, D_MODEL), F32),
        scratch_shapes=[pltpu.VMEM((tm, D_MODEL), F32), pltpu.VMEM((tm, D_MODEL), BF16),
                        pltpu.VMEM((tm, D_MODEL), F32)],
        compiler_params=_cparams(2),
        name="post_mlp",
    )(x, oa, ob, wa, wb, g, w1, w2, gf)


def _t5_bucket(rel):
    nb = T5_BUCKETS // 2
    max_exact = nb // 2
    ret = jnp.where(rel > 0, nb, 0)
    n = jnp.abs(rel)
    nf = jnp.maximum(n, 1).astype(F32)
    large = max_exact + (jnp.log(nf / max_exact) / math.log(T5_MAX_DIST / max_exact) * (nb - max_exact)).astype(jnp.int32)
    large = jnp.minimum(large, nb - 1)
    return ret + jnp.where(n < max_exact, n, large)


def _rope_tables(pos):
    half = MLA_ROPE // 2
    inv = ROPE_BASE ** (-jnp.arange(half, dtype=F32) / half)
    ang = pos.astype(F32)[:, None] * inv[None, :]
    cos, sin = jnp.cos(ang), jnp.sin(ang)
    cosk = jnp.concatenate([cos, cos], axis=1)
    sink = jnp.concatenate([-sin, sin], axis=1)
    n = pos.shape[0]
    pad = jnp.zeros((n, LANES - MLA_NOPE - MLA_ROPE), F32)
    cosq = jnp.concatenate([jnp.ones((n, MLA_NOPE), F32), cosk, pad], axis=1)
    sinq = jnp.concatenate([jnp.zeros((n, MLA_NOPE), F32), sink, pad], axis=1)
    return {"cosq": cosq, "sinq": sinq, "cosk": cosk, "sink": sink}


def _swap_halves(w):
    half = w.shape[-1] // 2
    return jnp.concatenate([w[..., half:], w[..., :half]], axis=-1)


def _even_weights(w_in, q_norm, kv_norm, w_uq, w_ukv, w_out):
    sizes = [MLA_Q_RANK, MLA_KV_RANK, MLA_ROPE, DIFF_HEADS * 2 * DIFF_QK, DIFF_HEADS * 2 * DIFF_QK,
             DIFF_HEADS * DIFF_V]
    offs = np.cumsum([0] + sizes)
    wcq, wckv, wkr, wdq, wdk, wdv = (w_in[:, offs[i]:offs[i + 1]].astype(BF16) for i in range(6))
    uq = w_uq.reshape(MLA_Q_RANK, MLA_HEADS, MLA_NOPE + MLA_ROPE)
    zq = jnp.zeros((MLA_Q_RANK, MLA_HEADS, LANES - MLA_NOPE - MLA_ROPE), F32)
    wq = jnp.concatenate([uq, zq], axis=-1)
    wqs = jnp.concatenate([jnp.zeros_like(uq[..., :MLA_NOPE]), _swap_halves(uq[..., MLA_NOPE:]), zq], axis=-1)
    ukv = w_ukv.reshape(MLA_KV_RANK, MLA_HEADS, MLA_NOPE + MLA_V)
    zk = jnp.zeros((MLA_KV_RANK, MLA_HEADS, LANES - MLA_NOPE), F32)
    wk = jnp.concatenate([ukv[..., :MLA_NOPE], zk], axis=-1)
    wv = jnp.concatenate([ukv[..., MLA_NOPE:], jnp.zeros((MLA_KV_RANK, MLA_HEADS, LANES - MLA_V), F32)], axis=-1)
    place = np.zeros((MLA_ROPE, MLA_HEADS, LANES), np.float32)
    ones = np.zeros((1, MLA_HEADS, LANES), np.float32)
    for hh in range(MLA_HEADS):
        place[np.arange(MLA_ROPE), hh, MLA_NOPE + np.arange(MLA_ROPE)] = 1.0
        ones[0, hh, MLA_V] = 1.0
    flat = lambda a: a.reshape(a.shape[0], MLA_HEADS * LANES)
    wo_mla = w_out[:MLA_HEADS * MLA_V].reshape(MLA_HEADS, MLA_V, D_MODEL)
    wo_mla = jnp.concatenate([wo_mla, jnp.zeros((MLA_HEADS, LANES - MLA_V, D_MODEL), F32)], axis=1)
    return {
        "wcq": wcq, "wckv": wckv, "wkr": wkr, "wkrs": _swap_halves(wkr), "wdq": wdq, "wdk": wdk, "wdv": wdv,
        "qn": q_norm.reshape(1, -1), "kvn": kv_norm.reshape(1, -1),
        "wq": flat(wq).astype(BF16), "wqs": flat(wqs).astype(BF16),
        "wk": flat(wk).astype(BF16), "wv": flat(wv).astype(BF16),
        "place": jnp.asarray(flat(place), BF16), "ones": jnp.asarray(flat(ones), F32),
        "wo_mla": wo_mla.reshape(MLA_HEADS * LANES, D_MODEL).astype(BF16),
        "wo_diff": w_out[MLA_HEADS * MLA_V:].astype(BF16),
    }


def _pad_rows(a, total, front=0):
    back = total - front - a.shape[1]
    return jnp.pad(a, ((0, 0), (front, back), (0, 0)))


def _round_up(n, mult):
    return -(-n // mult) * mult


def _diff_buckets(tq, tk, q_off, sk_real, skp):
    near_back = -((q_off - (T5_MAX_DIST - 1)) // tk - q_off // tk)
    last = (_round_up(q_off + tq, CHUNK) - 1) // tk
    last = min(last, skp // tk - 1)
    n_near = last - (q_off // tk - near_back) + 1
    i = np.arange(tq)[:, None]
    mats = []
    for r in range(n_near):
        kp = (q_off // tk - near_back + r) * tk + np.arange(tk)[None, :]
        qp = q_off + i
        ok = ((kp >> CHUNK_SHIFT) <= (qp >> CHUNK_SHIFT)) & (kp < sk_real)
        bkt = _t5_bucket(jnp.asarray(kp - qp, jnp.int32))
        mats.append(jnp.where(jnp.asarray(ok), bkt, -1))
    return jnp.stack(mats).astype(jnp.int32), near_back


def _trunk(x, q_off, caches, prm, cfg):
    b, sq, _ = x.shape
    m = b * sq
    tq, tk, tm, tf = cfg["tq"], cfg["tk"], cfg["tm"], cfg["tf"]
    sk_real = q_off + sq
    skp = _round_up(sk_real, tk)
    pos = q_off + jnp.arange(sq, dtype=jnp.int32)
    tabs = {k: jnp.tile(v, (b, 1)) for k, v in _rope_tables(pos).items()}
    x2 = x.reshape(m, D_MODEL)

    def with_past(past, new, dtype):
        new = new.reshape(b, sq, -1)
        if past is None:
            return new.astype(dtype)
        return jnp.concatenate([past.reshape(b, past.shape[1], -1).astype(dtype), new.astype(dtype)], axis=1)

    ew = prm["even"]
    qext, ckv, kr, dq, dk, dkb, dv, dvb = _even_proj(x2, prm["norm_mix"][0:1], ew, tabs, tm)
    past = (None,) * 4 if caches is None else tuple(c[0] for c in caches[:4])
    ckv_all = _pad_rows(with_past(past[0], ckv, F32), skp)
    kr_all = _pad_rows(with_past(past[1], kr, F32), skp)
    kext, vext = _kv_up(ckv_all.reshape(b * skp, -1), kr_all.reshape(b * skp, -1), ew, cfg["tm_kv"])
    kext = kext.reshape(b, skp, -1)
    vext = vext.reshape(b, skp, -1)
    n_near_mla = (_round_up(q_off + tq, CHUNK) - 1) // tk - q_off // tk + 1
    o_mla = _mla_attn(qext.reshape(b, sq, -1), kext, vext, tq=tq, tk=tk, q_off=q_off, sk_real=sk_real,
                      n_near=n_near_mla)
    dk_all = _pad_rows(with_past(past[2], dkb, BF16), skp)
    dv_all = _pad_rows(with_past(past[3], dvb, BF16), skp)
    bkt, near_back = _diff_buckets(tq, tk, q_off, sk_real, skp)
    lam_init = 0.8 - 0.6 * math.exp(-0.3 * 0)
    o_diff = _diff_attn(prm["t5"], dq.reshape(b, sq, -1), dk_all, dv_all, bkt, prm["lam_vecs"], prm["subln"],
                        tq=tq, tk=tk, q_off=q_off, near_back=near_back, far_bucket=T5_BUCKETS // 2 - 1,
                        lam_init=lam_init)
    x2 = _post(x2, o_mla.reshape(m, -1), o_diff.reshape(m, -1), ew["wo_mla"], ew["wo_diff"],
               prm["norm_ff"][0:1], prm["w_ff1"][0], prm["w_ff2"][0], prm["final_norm"],
               tm=tm, tf=tf, final_norm=False)
    new_even = (ckv.reshape(1, b, sq, MLA_KV_RANK), kr.reshape(1, b, sq, MLA_ROPE),
                dk.reshape(1, b, sq, DIFF_HEADS, 2 * DIFF_QK), dv.reshape(1, b, sq, DIFF_HEADS, DIFF_V))

    sq_, sk_, skb, sv_, svb, cq, ck, ckb, cv, cvb = _odd_proj(x2, prm["norm_mix"][1:2], prm["w_in_odd"], tm)
    past = (None,) * 4 if caches is None else tuple(c[0] for c in caches[4:])
    sk_all = _pad_rows(with_past(past[0], skb, BF16), skp)
    sv_all = _pad_rows(with_past(past[1], svb, BF16), skp)
    n_masked = (q_off + tq - 1) // tk - q_off // tk + 1
    o_sb = _sb_attn(sq_.reshape(b, sq, -1), sk_all, sv_all, prm["tri"][tk], tq=tq, tk=tk, q_off=q_off,
                    n_masked=n_masked)
    win_real = tq + CA_BAND
    win = _round_up(win_real, LANES)
    if caches is None:
        ck_all = _pad_rows(ckb.reshape(b, sq, -1), sq + CA_BAND + win - win_real, front=CA_BAND)
        cv_all = _pad_rows(cvb.reshape(b, sq, -1), sq + CA_BAND + win - win_real, front=CA_BAND)
    else:
        ck_all = _pad_rows(with_past(past[2], ckb, BF16), win)
        cv_all = _pad_rows(with_past(past[3], cvb, BF16), win)
    o_ca = _ca_attn(prm["ca_tab_t"], cq.reshape(b, sq, -1), ck_all, cv_all, tq=tq, win=win, win_real=win_real,
                    q_off=q_off)
    x2 = _post(x2, o_sb.reshape(m, -1), o_ca.reshape(m, -1), prm["wo_sb"], prm["wo_ca"],
               prm["norm_ff"][1:2], prm["w_ff1"][1], prm["w_ff2"][1], prm["final_norm"],
               tm=tm, tf=tf, final_norm=True)

    heads = lambda a: a.reshape(b, sq, SB_HEADS, SB_DIM)
    if caches is None:
        nb = min(CA_BAND, sq)
        cak, cav = heads(ck)[:, sq - nb:], heads(cv)[:, sq - nb:]
    else:
        nb = past[2].shape[1]
        cak = jnp.concatenate([past[2], heads(ck)], axis=1)[:, sq:]
        cav = jnp.concatenate([past[3], heads(cv)], axis=1)[:, sq:]
        assert cak.shape[1] == nb
    new_odd = (heads(sk_)[None], heads(sv_)[None], cak[None], cav[None])
    return x2.reshape(b, sq, D_MODEL), new_even + new_odd


def _tri(tk):
    j = np.arange(tk)[:, None]
    s = np.arange(tk)[None, :]
    return jnp.asarray((j > s).astype(np.float32), BF16)


def kernel(x_prompt, x_sample, cache_mla_ckv, cache_mla_krope, cache_diff_k, cache_diff_v, cache_sb_k, cache_sb_v, cache_ca_k, cache_ca_v, norm_mix, norm_ff, w_in_even, mla_q_norm, mla_kv_norm, mla_w_uq, mla_w_ukv, diff_lambda_vecs, diff_subln, t5_bias, w_out_even, w_in_odd, ca_rel_bias, w_out_odd, w_ff1, w_ff2, final_norm):
    seq = x_prompt.shape[1]
    dec_seq = x_sample.shape[1]
    past_len = cache_mla_ckv.shape[2]
    assert cache_ca_k.shape[2] == CA_BAND and past_len % CHUNK == 0

    cfg_p = {"tq": 256, "tk": 256, "tm": min(512, seq), "tm_kv": min(512, seq), "tf": 512}
    cfg_s = {"tq": dec_seq, "tk": 128, "tm": x_sample.shape[0] * dec_seq, "tm_kv": 128, "tf": 512}
    n_sb = SB_HEADS * SB_DIM
    prm = {
        "norm_mix": norm_mix, "norm_ff": norm_ff, "final_norm": final_norm.reshape(1, -1),
        "even": _even_weights(w_in_even[0], mla_q_norm[0], mla_kv_norm[0], mla_w_uq[0], mla_w_ukv[0],
                              w_out_even[0]),
        "t5": t5_bias, "lam_vecs": diff_lambda_vecs[0], "subln": diff_subln[0].reshape(1, -1),
        "w_in_odd": w_in_odd[0].astype(BF16), "ca_tab_t": ca_rel_bias[0].T,
        "wo_sb": w_out_odd[0][:n_sb].astype(BF16), "wo_ca": w_out_odd[0][n_sb:].astype(BF16),
        "w_ff1": w_ff1.astype(BF16), "w_ff2": w_ff2.astype(BF16),
        "tri": {tk: _tri(tk) for tk in {cfg_p["tk"], cfg_s["tk"]}},
    }
    y_prompt, new_p = _trunk(x_prompt, 0, None, prm, cfg_p)
    caches = (cache_mla_ckv, cache_mla_krope, cache_diff_k, cache_diff_v,
              cache_sb_k, cache_sb_v, cache_ca_k, cache_ca_v)
    y_sample, new_s = _trunk(x_sample, past_len, caches, prm, cfg_s)
    return (y_prompt, y_sample) + tuple(new_p) + tuple(new_s)
```

```python
import functools
import math

import numpy as np
import jax
import jax.numpy as jnp
from jax import lax
from jax.experimental import pallas as pl
from jax.experimental.pallas import tpu as pltpu

F32 = jnp.float32
BF16 = jnp.bfloat16

D_MODEL = 1024
CHUNK = 64
CHUNK_SHIFT = 6
EPS = 1e-6
NEG = -1e30

MLA_HEADS = 8
MLA_Q_RANK = 256
MLA_KV_RANK = 128
MLA_NOPE = 64
MLA_ROPE = 32
MLA_V = 64
ROPE_BASE = 10000.0
DIFF_HEADS = 4
DIFF_QK = 64
DIFF_V = 2 * DIFF_QK
T5_BUCKETS = 32
T5_MAX_DIST = 128
SB_HEADS = 8
SB_DIM = 64
CA_HEADS = 8
CA_DIM = 64
CA_LEFT_CHUNKS = 8
CA_BAND = CA_LEFT_CHUNKS * CHUNK
CA_MAX_REL = 128

LANES = 128
VMEM_LIMIT = 48 * 1024 * 1024
SB_EXIT = -104.0


def _cparams(n_axes):
    return pltpu.CompilerParams(dimension_semantics=("arbitrary",) * n_axes,
                                vmem_limit_bytes=VMEM_LIMIT)


def _rms(x, g):
    return x * lax.rsqrt(jnp.mean(x * x, axis=-1, keepdims=True) + EPS) * g


def _dot(a, b):
    return jnp.dot(a, b, preferred_element_type=F32)


def _dot_nt(a, b):
    return lax.dot_general(a, b, (((1,), (1,)), ((), ())), preferred_element_type=F32)


def _keep_lanes(q, keep):
    return jnp.where(keep, q.astype(F32), 0.0).astype(BF16)


def _full(shape):
    n = len(shape)
    return pl.BlockSpec(shape, lambda *_: (0,) * n)


def _rows(tm, width):
    return pl.BlockSpec((tm, width), lambda i: (i, 0))


def _even_proj_kernel(x_ref, g_ref, wcq_ref, wckv_ref, wkr_ref, wkrs_ref, wdq_ref, wdk_ref, wdv_ref,
                      qn_ref, kvn_ref, wq_ref, wqs_ref, cosq_ref, sinq_ref, cosk_ref, sink_ref,
                      qext_ref, ckv_ref, kr_ref, dq_ref, dk_ref, dkb_ref, dv_ref, dvb_ref):
    hn = _rms(x_ref[...], g_ref[...]).astype(BF16)
    cq = _rms(_dot(hn, wcq_ref[...]), qn_ref[...]).astype(BF16)
    cosq = jnp.concatenate([cosq_ref[...]] * MLA_HEADS, axis=1)
    sinq = jnp.concatenate([sinq_ref[...]] * MLA_HEADS, axis=1)
    qext_ref[...] = (_dot(cq, wq_ref[...]) * cosq + _dot(cq, wqs_ref[...]) * sinq).astype(BF16)
    ckv_ref[...] = _rms(_dot(hn, wckv_ref[...]), kvn_ref[...])
    kr_ref[...] = _dot(hn, wkr_ref[...]) * cosk_ref[...] + _dot(hn, wkrs_ref[...]) * sink_ref[...]
    dq_ref[...] = (_dot(hn, wdq_ref[...]) * (DIFF_QK ** -0.5)).astype(BF16)
    dk = _dot(hn, wdk_ref[...])
    dk_ref[...] = dk
    dkb_ref[...] = dk.astype(BF16)
    dv = _dot(hn, wdv_ref[...])
    dv_ref[...] = dv
    dvb_ref[...] = dv.astype(BF16)


def _even_proj(x, g, w, tabs, tm):
    m = x.shape[0]
    ins = [x, g, w["wcq"], w["wckv"], w["wkr"], w["wkrs"], w["wdq"], w["wdk"], w["wdv"],
           w["qn"], w["kvn"], w["wq"], w["wqs"], tabs["cosq"], tabs["sinq"], tabs["cosk"], tabs["sink"]]
    row_in = {0: D_MODEL, 13: LANES, 14: LANES, 15: MLA_ROPE, 16: MLA_ROPE}
    in_specs = [_rows(tm, row_in[i]) if i in row_in else _full(a.shape) for i, a in enumerate(ins)]
    outs = [(MLA_HEADS * LANES, BF16), (MLA_KV_RANK, F32), (MLA_ROPE, F32),
            (DIFF_HEADS * DIFF_V, BF16), (DIFF_HEADS * DIFF_V, F32), (DIFF_HEADS * DIFF_V, BF16),
            (DIFF_HEADS * DIFF_V, F32), (DIFF_HEADS * DIFF_V, BF16)]
    return pl.pallas_call(
        _even_proj_kernel,
        grid=(m // tm,),
        in_specs=in_specs,
        out_specs=[_rows(tm, n) for n, _ in outs],
        out_shape=[jax.ShapeDtypeStruct((m, n), dt) for n, dt in outs],
        compiler_params=_cparams(1),
        name="even_proj",
    )(*ins)


def _odd_proj_kernel(x_ref, g_ref, w_ref, sq_ref, sk_ref, skb_ref, sv_ref, svb_ref,
                     cq_ref, ck_ref, ckb_ref, cv_ref, cvb_ref):
    hn = _rms(x_ref[...], g_ref[...]).astype(BF16)
    width = SB_HEADS * SB_DIM

    def seg(i):
        return _dot(hn, w_ref[:, i * width:(i + 1) * width])

    sq_ref[...] = (seg(0) * (SB_DIM ** -0.5)).astype(BF16)
    for i, (f_ref, b_ref) in ((1, (sk_ref, skb_ref)), (2, (sv_ref, svb_ref)),
                              (4, (ck_ref, ckb_ref)), (5, (cv_ref, cvb_ref))):
        y = seg(i)
        f_ref[...] = y
        b_ref[...] = y.astype(BF16)
    cq_ref[...] = (seg(3) * (CA_DIM ** -0.5)).astype(BF16)


def _odd_proj(x, g, w, tm):
    m = x.shape[0]
    width = SB_HEADS * SB_DIM
    dts = [BF16, F32, BF16, F32, BF16, BF16, F32, BF16, F32, BF16]
    return pl.pallas_call(
        _odd_proj_kernel,
        grid=(m // tm,),
        in_specs=[_rows(tm, D_MODEL), _full(g.shape), _full(w.shape)],
        out_specs=[_rows(tm, width) for _ in dts],
        out_shape=[jax.ShapeDtypeStruct((m, width), dt) for dt in dts],
        compiler_params=_cparams(1),
        name="odd_proj",
    )(x, g, w)


def _kv_up_kernel(ckv_ref, kr_ref, wk_ref, wv_ref, place_ref, ones_ref, kext_ref, vext_ref):
    c = ckv_ref[...].astype(BF16)
    r = kr_ref[...].astype(BF16)
    kext_ref[...] = (_dot(c, wk_ref[...]) + _dot(r, place_ref[...])).astype(BF16)
    vext_ref[...] = (_dot(c, wv_ref[...]) + ones_ref[...]).astype(BF16)


def _kv_up(ckv, kr, w, tm):
    m = ckv.shape[0]
    width = MLA_HEADS * LANES
    return pl.pallas_call(
        _kv_up_kernel,
        grid=(m // tm,),
        in_specs=[_rows(tm, MLA_KV_RANK), _rows(tm, MLA_ROPE), _full(w["wk"].shape), _full(w["wv"].shape),
                  _full(w["place"].shape), _full(w["ones"].shape)],
        out_specs=[_rows(tm, width)] * 2,
        out_shape=[jax.ShapeDtypeStruct((m, width), BF16)] * 2,
        compiler_params=_cparams(1),
        name="mla_kv_up",
    )(ckv, kr, w["wk"], w["wv"], w["place"], w["ones"])


def _attn_specs(tq, skp):
    q_spec = pl.BlockSpec((None, tq, LANES), lambda b, h, qi: (b, qi, h))
    kv_spec = pl.BlockSpec((None, skp, LANES), lambda b, h, qi: (b, 0, h))
    return q_spec, kv_spec


def _softmax_block(s, m_ref):
    m_old = m_ref[...]
    m_new = jnp.maximum(m_old, jnp.max(s, axis=-1, keepdims=True))
    m_ref[...] = m_new
    return jnp.exp(m_old - m_new), jnp.exp(s - jnp.tile(m_new, (1, s.shape[1] // LANES)))


def _sweep_blocks(near0, n_near, tkw, tkn, do_block):
    first = jnp.maximum(near0, 0)
    n_wide = (first * tkn) // tkw

    def wide(j, carry):
        do_block(pl.multiple_of(j * tkw, tkw), tkw, None)
        return carry

    def narrow(j, carry):
        do_block(pl.multiple_of(j * tkn, tkn), tkn, None)
        return carry

    lax.fori_loop(0, n_wide, wide, 0)
    lax.fori_loop(n_wide * (tkw // tkn), first, narrow, 0)
    for r in range(n_near):
        @pl.when(near0 + r >= 0)
        def _special(r=r):
            do_block(pl.multiple_of((near0 + r) * tkn, tkn), tkn, r)


def _mla_kernel(q_ref, k_ref, v_ref, o_ref, m_ref, acc_ref, *, tq, tkw, tkn, q_off, sk_real, n_near):
    qi = pl.program_id(2)
    qpos0 = q_off + qi * tq
    q = q_ref[...]
    scale = (MLA_NOPE + MLA_ROPE) ** -0.5
    m_ref[...] = jnp.full(m_ref.shape, NEG, F32)
    acc_ref[...] = jnp.zeros(acc_ref.shape, F32)

    def block(start, width, near):
        s = _dot_nt(q, k_ref[pl.ds(start, width), :]) * scale
        if near is not None:
            qp = qpos0 + lax.broadcasted_iota(jnp.int32, (tq, width), 0)
            kp = start + lax.broadcasted_iota(jnp.int32, (tq, width), 1)
            ok = ((kp >> CHUNK_SHIFT) <= (qp >> CHUNK_SHIFT)) & (kp < sk_real)
            s = jnp.where(ok, s, NEG)
        alpha, p = _softmax_block(s, m_ref)
        acc_ref[...] = acc_ref[...] * alpha + _dot(p.astype(BF16), v_ref[pl.ds(start, width), :])

    _sweep_blocks(qpos0 // tkn, n_near, tkw, tkn, block)

    acc = acc_ref[...]
    lane = lax.broadcasted_iota(jnp.int32, acc.shape, 1)
    denom = jnp.sum(jnp.where(lane == MLA_V, acc, 0.0), axis=-1, keepdims=True)
    o_ref[...] = jnp.where(lane < MLA_V, acc / denom, 0.0).astype(BF16)


def _mla_attn(qext, kext, vext, *, tq, tkw, tkn, q_off, sk_real, n_near):
    b, sq, _ = qext.shape
    skp = kext.shape[1]
    q_spec, kv_spec = _attn_specs(tq, skp)
    kern = functools.partial(_mla_kernel, tq=tq, tkw=tkw, tkn=tkn, q_off=q_off, sk_real=sk_real, n_near=n_near)
    return pl.pallas_call(
        kern,
        grid=(b, MLA_HEADS, sq // tq),
        in_specs=[q_spec, kv_spec, kv_spec],
        out_specs=q_spec,
        out_shape=jax.ShapeDtypeStruct((b, sq, MLA_HEADS * LANES), BF16),
        scratch_shapes=[pltpu.VMEM((tq, LANES), F32), pltpu.VMEM((tq, LANES), F32)],
        compiler_params=_cparams(3),
        name="mla_attn",
    )(qext, kext, vext)


def _diff_kernel(t5_ref, q_ref, k_ref, v_ref, bkt_ref, lamv_ref, subln_ref, o_ref,
                 bias_ref, m_ref, l_ref, acc_ref, *, tq, tkw, tkn, q_off, n_near, near_back, far_bucket,
                 lam_init):
    first = (pl.program_id(0) == 0) & (pl.program_id(1) == 0) & (pl.program_id(2) == 0)

    @pl.when(first)
    def _build_bias():
        for r in range(n_near):
            bkt = bkt_ref[r]
            vals = [jnp.full((tq, tkn), NEG, F32) for _ in range(DIFF_HEADS)]
            for t in range(T5_BUCKETS):
                hit = bkt == t
                for hh in range(DIFF_HEADS):
                    vals[hh] = jnp.where(hit, t5_ref[t, hh] - t5_ref[far_bucket, hh], vals[hh])
            for hh in range(DIFF_HEADS):
                bias_ref[r, hh] = vals[hh]

    h = pl.program_id(1)
    qi = pl.program_id(2)
    qpos0 = q_off + qi * tq
    q = q_ref[...]
    lane = lax.broadcasted_iota(jnp.int32, q.shape, 1)
    qm = [_keep_lanes(q, lane < DIFF_QK), _keep_lanes(q, lane >= DIFF_QK)]
    m_ref[...] = jnp.full(m_ref.shape, NEG, F32)
    l_ref[...] = jnp.zeros(l_ref.shape, F32)
    acc_ref[...] = jnp.zeros(acc_ref.shape, F32)

    def block(start, width, near):
        k = k_ref[pl.ds(start, width), :]
        v = v_ref[pl.ds(start, width), :]
        ones = jnp.ones((width, LANES), BF16)
        for mi in range(2):
            s = _dot_nt(qm[mi], k)
            if near is not None:
                s = s + bias_ref[near, h]
            alpha, p = _softmax_block(s, m_ref.at[mi])
            p = p.astype(BF16)
            acc_ref[mi] = acc_ref[mi] * alpha + _dot(p, v)
            l_ref[mi] = l_ref[mi] * alpha + _dot(p, ones)

    _sweep_blocks(qpos0 // tkn - near_back, n_near, tkw, tkn, block)

    lv = lamv_ref[...]
    lam = (jnp.exp(jnp.sum(lv[0:1] * lv[1:2], axis=-1, keepdims=True))
           - jnp.exp(jnp.sum(lv[2:3] * lv[3:4], axis=-1, keepdims=True)) + lam_init)
    o = acc_ref[0] / l_ref[0] - lam * (acc_ref[1] / l_ref[1])
    o_ref[...] = (_rms(o, subln_ref[...]) * (1.0 - lam_init)).astype(BF16)


def _diff_attn(t5, dq, dk, dv, bkt, lamv, subln, *, tq, tkw, tkn, q_off, near_back, far_bucket, lam_init):
    b, sq, _ = dq.shape
    skp = dk.shape[1]
    n_near = bkt.shape[0]
    q_spec, kv_spec = _attn_specs(tq, skp)
    kern = functools.partial(_diff_kernel, tq=tq, tkw=tkw, tkn=tkn, q_off=q_off, n_near=n_near,
                             near_back=near_back, far_bucket=far_bucket, lam_init=lam_init)
    return pl.pallas_call(
        kern,
        grid=(b, DIFF_HEADS, sq // tq),
        in_specs=[pl.BlockSpec(memory_space=pltpu.SMEM), q_spec, kv_spec, kv_spec,
                  _full(bkt.shape), _full(lamv.shape), _full(subln.shape)],
        out_specs=q_spec,
        out_shape=jax.ShapeDtypeStruct((b, sq, DIFF_HEADS * DIFF_V), BF16),
        scratch_shapes=[pltpu.VMEM((n_near, DIFF_HEADS, tq, tkn), F32), pltpu.VMEM((2, tq, LANES), F32),
                        pltpu.VMEM((2, tq, LANES), F32), pltpu.VMEM((2, tq, LANES), F32)],
        compiler_params=_cparams(3),
        name="diff_attn",
    )(t5, dq, dk, dv, bkt, lamv, subln)


def _sb_kernel(q_ref, k_ref, v_ref, tri_ref, o_ref, run_ref, acc_ref, *, tq, tk, q_off, n_masked):
    qi = pl.program_id(2)
    qpos0 = q_off + qi * tq
    top = (qpos0 + tq - 1) // tk
    q = q_ref[...]
    lane = lax.broadcasted_iota(jnp.int32, q.shape, 1)
    tri = tri_ref[...]

    def block(kj, qh, masked):
        ks = pl.multiple_of(kj * tk, tk)
        z = _dot_nt(qh, k_ref[pl.ds(ks, tk), :])
        t = jnp.log1p(jnp.exp(-jnp.abs(z)))
        log_1m = -(jnp.maximum(z, 0.0) + t)
        log_beta = jnp.minimum(z, 0.0) - t
        if masked:
            qp = qpos0 + lax.broadcasted_iota(jnp.int32, (tq, tk), 0)
            kp = kj * tk + lax.broadcasted_iota(jnp.int32, (tq, tk), 1)
            ok = kp < qp
            log_1m = jnp.where(ok, log_1m, 0.0)
        hi = log_1m.astype(BF16)
        lo = (log_1m - hi.astype(F32)).astype(BF16)
        after = _dot(hi, tri) + _dot(lo, tri) + run_ref[...]
        a = jnp.exp(log_beta + after)
        if masked:
            a = jnp.where(ok, a, 0.0)
        acc_ref[...] += _dot(a.astype(BF16), v_ref[pl.ds(ks, tk), :])
        run_ref[...] += jnp.sum(log_1m, axis=-1, keepdims=True)

    outs = []
    for hh in range(2):
        qh = _keep_lanes(q, (lane < SB_DIM) if hh == 0 else (lane >= SB_DIM))
        run_ref[...] = jnp.zeros(run_ref.shape, F32)
        acc_ref[...] = jnp.zeros(acc_ref.shape, F32)
        for r in range(n_masked):
            block(top - r, qh, True)

        def cond(kj):
            return (kj >= 0) & (jnp.max(run_ref[...]) > SB_EXIT)

        def body(kj, qh=qh):
            block(kj, qh, False)
            return kj - 1

        lax.while_loop(cond, body, top - n_masked)
        outs.append(acc_ref[...])
    o_ref[...] = jnp.where(lane < SB_DIM, outs[0], outs[1]).astype(BF16)


def _sb_attn(sq_, sk_, sv_, tri, *, tq, tk, q_off, n_masked):
    b, sq, _ = sq_.shape
    skp = sk_.shape[1]
    q_spec, kv_spec = _attn_specs(tq, skp)
    kern = functools.partial(_sb_kernel, tq=tq, tk=tk, q_off=q_off, n_masked=n_masked)
    return pl.pallas_call(
        kern,
        grid=(b, SB_HEADS // 2, sq // tq),
        in_specs=[q_spec, kv_spec, kv_spec, _full(tri.shape)],
        out_specs=q_spec,
        out_shape=jax.ShapeDtypeStruct((b, sq, SB_HEADS * SB_DIM), BF16),
        scratch_shapes=[pltpu.VMEM((tq, 1), F32), pltpu.VMEM((tq, LANES), F32)],
        compiler_params=_cparams(3),
        name="sb_attn",
    )(sq_, sk_, sv_, tri)


CA_ROW_GROUP = 16


def _ca_kernel(tab_ref, q_ref, k_ref, v_ref, o_ref, bias_ref, *, tq, win, win_real, q_off):
    first = (pl.program_id(0) == 0) & (pl.program_id(1) == 0) & (pl.program_id(2) == 0)
    n_shift = -(-(tq + win) // LANES) * LANES
    n_rel = n_shift + LANES

    @pl.when(first)
    def _build_bias():
        x = lax.broadcasted_iota(jnp.int32, (CA_HEADS, n_rel), 1)
        idx = jnp.clip(x - (tq - 1) - CA_BAND, -CA_MAX_REL, CA_MAX_REL) + CA_MAX_REL
        f = jnp.zeros((CA_HEADS, n_rel), F32)
        for t in range(2 * CA_MAX_REL + 1):
            f = jnp.where(idx == t, tab_ref[:, t:t + 1], f)
        g = CA_ROW_GROUP
        i_loc = lax.broadcasted_iota(jnp.int32, (g, win), 0)
        j_loc = lax.broadcasted_iota(jnp.int32, (g, win), 1)
        for hh in range(CA_HEADS):
            fh = f[hh:hh + 1, :]
            shifted = jnp.concatenate([fh[:, g - 1 - bb:g - 1 - bb + n_shift] for bb in range(g)], axis=0)
            for a in range(tq // g):
                start = tq - g * a - g
                tile = shifted[:, start:start + win]
                i = i_loc + g * a
                kc = (j_loc >> CHUNK_SHIFT) - CA_LEFT_CHUNKS
                qc = i >> CHUNK_SHIFT
                ok = (kc <= qc) & (kc >= qc - CA_LEFT_CHUNKS) & (j_loc < win_real)
                bias_ref[hh, g * a:g * a + g, :] = jnp.where(ok, tile, NEG)

    hp = pl.program_id(1)
    qi = pl.program_id(2)
    qpos0 = q_off + qi * tq
    ws = pl.multiple_of(qi * tq, tq)
    q = q_ref[...]
    k = k_ref[pl.ds(ws, win), :]
    v = v_ref[pl.ds(ws, win), :]
    lane = lax.broadcasted_iota(jnp.int32, q.shape, 1)
    kpos = qpos0 - CA_BAND + lax.broadcasted_iota(jnp.int32, (tq, win), 1)
    outs = []
    for hh in range(2):
        qh = _keep_lanes(q, (lane < CA_DIM) if hh == 0 else (lane >= CA_DIM))
        s = _dot_nt(qh, k) + bias_ref[2 * hp + hh]
        s = jnp.where(kpos >= 0, s, NEG)
        m = jnp.max(s, axis=-1, keepdims=True)
        p = jnp.exp(s - m)
        denom = jnp.sum(p, axis=-1, keepdims=True)
        outs.append(_dot(p.astype(BF16), v) / denom)
    o_ref[...] = jnp.where(lane < CA_DIM, outs[0], outs[1]).astype(BF16)


def _ca_attn(tab_t, cq, ck, cv, *, tq, win, win_real, q_off):
    b, sq, _ = cq.shape
    skp = ck.shape[1]
    q_spec, kv_spec = _attn_specs(tq, skp)
    kern = functools.partial(_ca_kernel, tq=tq, win=win, win_real=win_real, q_off=q_off)
    return pl.pallas_call(
        kern,
        grid=(b, CA_HEADS // 2, sq // tq),
        in_specs=[_full(tab_t.shape), q_spec, kv_spec, kv_spec],
        out_specs=q_spec,
        out_shape=jax.ShapeDtypeStruct((b, sq, CA_HEADS * CA_DIM), BF16),
        scratch_shapes=[pltpu.VMEM((CA_HEADS, tq, win), F32)],
        compiler_params=_cparams(3),
        name="ca_attn",
    )(tab_t, cq, ck, cv)


def _post_kernel(x_ref, oa_ref, ob_ref, wa_ref, wb_ref, g_ref, w1_ref, w2_ref, gf_ref, y_ref,
                 x1_ref, hn_ref, acc_ref, *, final_norm):
    j = pl.program_id(1)

    @pl.when(j == 0)
    def _mix():
        x1 = x_ref[...] + _dot(oa_ref[...], wa_ref[...]) + _dot(ob_ref[...], wb_ref[...])
        x1_ref[...] = x1
        hn_ref[...] = _rms(x1, g_ref[...]).astype(BF16)
        acc_ref[...] = jnp.zeros(acc_ref.shape, F32)

    a = jnp.maximum(_dot(hn_ref[...], w1_ref[...]), 0.0)
    acc_ref[...] += _dot((a * a).astype(BF16), w2_ref[...])

    @pl.when(j == pl.num_programs(1) - 1)
    def _finish():
        y = x1_ref[...] + acc_ref[...]
        if final_norm:
            y = _rms(y, gf_ref[...])
        y_ref[...] = y


def _post(x, oa, ob, wa, wb, g, w1, w2, gf, *, tm, tf, final_norm):
    m = x.shape[0]
    d_ff = w1.shape[1]
    rows = lambda width: pl.BlockSpec((tm, width), lambda i, j: (i, 0))
    return pl.pallas_call(
        functools.partial(_post_kernel, final_norm=final_norm),
        grid=(m // tm, d_ff // tf),
        in_specs=[rows(D_MODEL), rows(oa.shape[1]), rows(ob.shape[1]), _full(wa.shape), _full(wb.shape),
                  _full(g.shape), pl.BlockSpec((D_MODEL, tf), lambda i, j: (0, j)),
                  pl.BlockSpec((tf, D_MODEL), lambda i, j: (j, 0)), _full(gf.shape)],
        out_specs=rows(D_MODEL),
        out_shape=jax.ShapeDtypeStruct((m, D_MODEL), F32),
        scratch_shapes=[pltpu.VMEM((tm, D_MODEL), F32), pltpu.VMEM((tm, D_MODEL), BF16),
                        pltpu.VMEM((tm, D_MODEL), F32)],
        compiler_params=_cparams(2),
        name="post_mlp",
    )(x, oa, ob, wa, wb, g, w1, w2, gf)


def _t5_bucket(rel):
    nb = T5_BUCKETS // 2
    max_exact = nb // 2
    ret = jnp.where(rel > 0, nb, 0)
    n = jnp.abs(rel)
    nf = jnp.maximum(n, 1).astype(F32)
    large = max_exact + (jnp.log(nf / max_exact) / math.log(T5_MAX_DIST / max_exact) * (nb - max_exact)).astype(jnp.int32)
    large = jnp.minimum(large, nb - 1)
    return ret + jnp.where(n < max_exact, n, large)


def _rope_tables(pos):
    half = MLA_ROPE // 2
    inv = ROPE_BASE ** (-jnp.arange(half, dtype=F32) / half)
    ang = pos.astype(F32)[:, None] * inv[None, :]
    cos, sin = jnp.cos(ang), jnp.sin(ang)
    cosk = jnp.concatenate([cos, cos], axis=1)
    sink = jnp.concatenate([-sin, sin], axis=1)
    n = pos.shape[0]
    pad = jnp.zeros((n, LANES - MLA_NOPE - MLA_ROPE), F32)
    cosq = jnp.concatenate([jnp.ones((n, MLA_NOPE), F32), cosk, pad], axis=1)
    sinq = jnp.concatenate([jnp.zeros((n, MLA_NOPE), F32), sink, pad], axis=1)
    return {"cosq": cosq, "sinq": sinq, "cosk": cosk, "sink": sink}


def _swap_halves(w):
    half = w.shape[-1] // 2
    return jnp.concatenate([w[..., half:], w[..., :half]], axis=-1)


def _even_weights(w_in, q_norm, kv_norm, w_uq, w_ukv, w_out):
    sizes = [MLA_Q_RANK, MLA_KV_RANK, MLA_ROPE, DIFF_HEADS * 2 * DIFF_QK, DIFF_HEADS * 2 * DIFF_QK,
             DIFF_HEADS * DIFF_V]
    offs = np.cumsum([0] + sizes)
    wcq, wckv, wkr, wdq, wdk, wdv = (w_in[:, offs[i]:offs[i + 1]].astype(BF16) for i in range(6))
    uq = w_uq.reshape(MLA_Q_RANK, MLA_HEADS, MLA_NOPE + MLA_ROPE)
    zq = jnp.zeros((MLA_Q_RANK, MLA_HEADS, LANES - MLA_NOPE - MLA_ROPE), F32)
    wq = jnp.concatenate([uq, zq], axis=-1)
    wqs = jnp.concatenate([jnp.zeros_like(uq[..., :MLA_NOPE]), _swap_halves(uq[..., MLA_NOPE:]), zq], axis=-1)
    ukv = w_ukv.reshape(MLA_KV_RANK, MLA_HEADS, MLA_NOPE + MLA_V)
    zk = jnp.zeros((MLA_KV_RANK, MLA_HEADS, LANES - MLA_NOPE), F32)
    wk = jnp.concatenate([ukv[..., :MLA_NOPE], zk], axis=-1)
    wv = jnp.concatenate([ukv[..., MLA_NOPE:], jnp.zeros((MLA_KV_RANK, MLA_HEADS, LANES - MLA_V), F32)], axis=-1)
    place = np.zeros((MLA_ROPE, MLA_HEADS, LANES), np.float32)
    ones = np.zeros((1, MLA_HEADS, LANES), np.float32)
    for hh in range(MLA_HEADS):
        place[np.arange(MLA_ROPE), hh, MLA_NOPE + np.arange(MLA_ROPE)] = 1.0
        ones[0, hh, MLA_V] = 1.0
    flat = lambda a: a.reshape(a.shape[0], MLA_HEADS * LANES)
    wo_mla = w_out[:MLA_HEADS * MLA_V].reshape(MLA_HEADS, MLA_V, D_MODEL)
    wo_mla = jnp.concatenate([wo_mla, jnp.zeros((MLA_HEADS, LANES - MLA_V, D_MODEL), F32)], axis=1)
    return {
        "wcq": wcq, "wckv": wckv, "wkr": wkr, "wkrs": _swap_halves(wkr), "wdq": wdq, "wdk": wdk, "wdv": wdv,
        "qn": q_norm.reshape(1, -1), "kvn": kv_norm.reshape(1, -1),
        "wq": flat(wq).astype(BF16), "wqs": flat(wqs).astype(BF16),
        "wk": flat(wk).astype(BF16), "wv": flat(wv).astype(BF16),
        "place": jnp.asarray(flat(place), BF16), "ones": jnp.asarray(flat(ones), F32),
        "wo_mla": wo_mla.reshape(MLA_HEADS * LANES, D_MODEL).astype(BF16),
        "wo_diff": w_out[MLA_HEADS * MLA_V:].astype(BF16),
    }


def _pad_rows(a, total, front=0):
    back = total - front - a.shape[1]
    return jnp.pad(a, ((0, 0), (front, back), (0, 0)))


def _round_up(n, mult):
    return -(-n // mult) * mult


def _diff_buckets(tq, tk, q_off, sk_real, skp):
    near_back = -((q_off - (T5_MAX_DIST - 1)) // tk - q_off // tk)
    last = (_round_up(q_off + tq, CHUNK) - 1) // tk
    last = min(last, skp // tk - 1)
    n_near = last - (q_off // tk - near_back) + 1
    i = np.arange(tq)[:, None]
    mats = []
    for r in range(n_near):
        kp = (q_off // tk - near_back + r) * tk + np.arange(tk)[None, :]
        qp = q_off + i
        ok = ((kp >> CHUNK_SHIFT) <= (qp >> CHUNK_SHIFT)) & (kp < sk_real)
        bkt = _t5_bucket(jnp.asarray(kp - qp, jnp.int32))
        mats.append(jnp.where(jnp.asarray(ok), bkt, -1))
    return jnp.stack(mats).astype(jnp.int32), near_back


def _trunk(x, q_off, caches, prm, cfg):
    b, sq, _ = x.shape
    m = b * sq
    tq, tk, tm, tf = cfg["tq"], cfg["tk"], cfg["tm"], cfg["tf"]
    sk_real = q_off + sq
    skp = _round_up(sk_real, tk)
    pos = q_off + jnp.arange(sq, dtype=jnp.int32)
    tabs = {k: jnp.tile(v, (b, 1)) for k, v in _rope_tables(pos).items()}
    x2 = x.reshape(m, D_MODEL)

    def with_past(past, new, dtype):
        new = new.reshape(b, sq, -1)
        if past is None:
            return new.astype(dtype)
        return jnp.concatenate([past.reshape(b, past.shape[1], -1).astype(dtype), new.astype(dtype)], axis=1)

    ew = prm["even"]
    qext, ckv, kr, dq, dk, dkb, dv, dvb = _even_proj(x2, prm["norm_mix"][0:1], ew, tabs, tm)
    past = (None,) * 4 if caches is None else tuple(c[0] for c in caches[:4])
    ckv_all = _pad_rows(with_past(past[0], ckv, F32), skp)
    kr_all = _pad_rows(with_past(past[1], kr, F32), skp)
    kext, vext = _kv_up(ckv_all.reshape(b * skp, -1), kr_all.reshape(b * skp, -1), ew, cfg["tm_kv"])
    kext = kext.reshape(b, skp, -1)
    vext = vext.reshape(b, skp, -1)
    n_near_mla = (_round_up(q_off + tq, CHUNK) - 1) // tk - q_off // tk + 1
    o_mla = _mla_attn(qext.reshape(b, sq, -1), kext, vext, tq=tq, tkw=cfg["tkw"], tkn=tk, q_off=q_off,
                      sk_real=sk_real, n_near=n_near_mla)
    dk_all = _pad_rows(with_past(past[2], dkb, BF16), skp)
    dv_all = _pad_rows(with_past(past[3], dvb, BF16), skp)
    bkt, near_back = _diff_buckets(tq, tk, q_off, sk_real, skp)
    lam_init = 0.8 - 0.6 * math.exp(-0.3 * 0)
    o_diff = _diff_attn(prm["t5"], dq.reshape(b, sq, -1), dk_all, dv_all, bkt, prm["lam_vecs"], prm["subln"],
                        tq=tq, tkw=cfg["tkw"], tkn=tk, q_off=q_off, near_back=near_back,
                        far_bucket=T5_BUCKETS // 2 - 1,
                        lam_init=lam_init)
    x2 = _post(x2, o_mla.reshape(m, -1), o_diff.reshape(m, -1), ew["wo_mla"], ew["wo_diff"],
               prm["norm_ff"][0:1], prm["w_ff1"][0], prm["w_ff2"][0], prm["final_norm"],
               tm=tm, tf=tf, final_norm=False)
    new_even = (ckv.reshape(1, b, sq, MLA_KV_RANK), kr.reshape(1, b, sq, MLA_ROPE),
                dk.reshape(1, b, sq, DIFF_HEADS, 2 * DIFF_QK), dv.reshape(1, b, sq, DIFF_HEADS, DIFF_V))

    sq_, sk_, skb, sv_, svb, cq, ck, ckb, cv, cvb = _odd_proj(x2, prm["norm_mix"][1:2], prm["w_in_odd"], tm)
    past = (None,) * 4 if caches is None else tuple(c[0] for c in caches[4:])
    sk_all = _pad_rows(with_past(past[0], skb, BF16), skp)
    sv_all = _pad_rows(with_past(past[1], svb, BF16), skp)
    n_masked = (q_off + tq - 1) // tk - q_off // tk + 1
    o_sb = _sb_attn(sq_.reshape(b, sq, -1), sk_all, sv_all, prm["tri"][tk], tq=tq, tk=tk, q_off=q_off,
                    n_masked=n_masked)
    win_real = tq + CA_BAND
    win = _round_up(win_real, LANES)
    if caches is None:
        ck_all = _pad_rows(ckb.reshape(b, sq, -1), sq + CA_BAND + win - win_real, front=CA_BAND)
        cv_all = _pad_rows(cvb.reshape(b, sq, -1), sq + CA_BAND + win - win_real, front=CA_BAND)
    else:
        ck_all = _pad_rows(with_past(past[2], ckb, BF16), win)
        cv_all = _pad_rows(with_past(past[3], cvb, BF16), win)
    o_ca = _ca_attn(prm["ca_tab_t"], cq.reshape(b, sq, -1), ck_all, cv_all, tq=tq, win=win, win_real=win_real,
                    q_off=q_off)
    x2 = _post(x2, o_sb.reshape(m, -1), o_ca.reshape(m, -1), prm["wo_sb"], prm["wo_ca"],
               prm["norm_ff"][1:2], prm["w_ff1"][1], prm["w_ff2"][1], prm["final_norm"],
               tm=tm, tf=tf, final_norm=True)

    heads = lambda a: a.reshape(b, sq, SB_HEADS, SB_DIM)
    if caches is None:
        nb = min(CA_BAND, sq)
        cak, cav = heads(ck)[:, sq - nb:], heads(cv)[:, sq - nb:]
    else:
        nb = past[2].shape[1]
        cak = jnp.concatenate([past[2], heads(ck)], axis=1)[:, sq:]
        cav = jnp.concatenate([past[3], heads(cv)], axis=1)[:, sq:]
        assert cak.shape[1] == nb
    new_odd = (heads(sk_)[None], heads(sv_)[None], cak[None], cav[None])
    return x2.reshape(b, sq, D_MODEL), new_even + new_odd


def _tri(tk):
    j = np.arange(tk)[:, None]
    s = np.arange(tk)[None, :]
    return jnp.asarray((j > s).astype(np.float32), BF16)


def kernel(x_prompt, x_sample, cache_mla_ckv, cache_mla_krope, cache_diff_k, cache_diff_v, cache_sb_k, cache_sb_v, cache_ca_k, cache_ca_v, norm_mix, norm_ff, w_in_even, mla_q_norm, mla_kv_norm, mla_w_uq, mla_w_ukv, diff_lambda_vecs, diff_subln, t5_bias, w_out_even, w_in_odd, ca_rel_bias, w_out_odd, w_ff1, w_ff2, final_norm):
    seq = x_prompt.shape[1]
    dec_seq = x_sample.shape[1]
    past_len = cache_mla_ckv.shape[2]
    assert cache_ca_k.shape[2] == CA_BAND and past_len % CHUNK == 0

    cfg_p = {"tq": 256, "tk": 256, "tkw": 1024, "tm": min(512, seq), "tm_kv": min(512, seq), "tf": 512}
    cfg_s = {"tq": dec_seq, "tk": 128, "tkw": 512, "tm": x_sample.shape[0] * dec_seq, "tm_kv": 128, "tf": 512}
    n_sb = SB_HEADS * SB_DIM
    prm = {
        "norm_mix": norm_mix, "norm_ff": norm_ff, "final_norm": final_norm.reshape(1, -1),
        "even": _even_weights(w_in_even[0], mla_q_norm[0], mla_kv_norm[0], mla_w_uq[0], mla_w_ukv[0],
                              w_out_even[0]),
        "t5": t5_bias, "lam_vecs": diff_lambda_vecs[0], "subln": diff_subln[0].reshape(1, -1),
        "w_in_odd": w_in_odd[0].astype(BF16), "ca_tab_t": ca_rel_bias[0].T,
        "wo_sb": w_out_odd[0][:n_sb].astype(BF16), "wo_ca": w_out_odd[0][n_sb:].astype(BF16),
        "w_ff1": w_ff1.astype(BF16), "w_ff2": w_ff2.astype(BF16),
        "tri": {tk: _tri(tk) for tk in {cfg_p["tk"], cfg_s["tk"]}},
    }
    y_prompt, new_p = _trunk(x_prompt, 0, None, prm, cfg_p)
    caches = (cache_mla_ckv, cache_mla_krope, cache_diff_k, cache_diff_v,
              cache_sb_k, cache_sb_v, cache_ca_k, cache_ca_v)
    y_sample, new_s = _trunk(x_sample, past_len, caches, prm, cfg_s)
    return (y_prompt, y_sample) + tuple(new_p) + tuple(new_s)
```

```python
import functools
import math

import numpy as np
import jax
import jax.numpy as jnp
from jax import lax
from jax.experimental import pallas as pl
from jax.experimental.pallas import tpu as pltpu

F32 = jnp.float32
BF16 = jnp.bfloat16

D_MODEL = 1024
CHUNK = 64
CHUNK_SHIFT = 6
EPS = 1e-6
NEG = -1e30

MLA_HEADS = 8
MLA_Q_RANK = 256
MLA_KV_RANK = 128
MLA_NOPE = 64
MLA_ROPE = 32
MLA_V = 64
ROPE_BASE = 10000.0
DIFF_HEADS = 4
DIFF_QK = 64
DIFF_V = 2 * DIFF_QK
T5_BUCKETS = 32
T5_MAX_DIST = 128
SB_HEADS = 8
SB_DIM = 64
CA_HEADS = 8
CA_DIM = 64
CA_LEFT_CHUNKS = 8
CA_BAND = CA_LEFT_CHUNKS * CHUNK
CA_MAX_REL = 128

LANES = 128
VMEM_LIMIT = 48 * 1024 * 1024
LOG2E = math.log2(math.e)
SB_EXIT = -104.0


def _cparams(n_axes):
    return pltpu.CompilerParams(dimension_semantics=("arbitrary",) * n_axes,
                                vmem_limit_bytes=VMEM_LIMIT)


def _rms(x, g):
    return x * lax.rsqrt(jnp.mean(x * x, axis=-1, keepdims=True) + EPS) * g


def _dot(a, b):
    return jnp.dot(a, b, preferred_element_type=F32)


def _dot_nt(a, b):
    return lax.dot_general(a, b, (((1,), (1,)), ((), ())), preferred_element_type=F32)


def _keep_lanes(q, keep):
    return jnp.where(keep, q.astype(F32), 0.0).astype(BF16)


def _full(shape):
    n = len(shape)
    return pl.BlockSpec(shape, lambda *_: (0,) * n)


def _rows(tm, width):
    return pl.BlockSpec((tm, width), lambda i: (i, 0))


def _even_proj_kernel(x_ref, g_ref, wcq_ref, wckv_ref, wkr_ref, wkrs_ref, wdq_ref, wdk_ref, wdv_ref,
                      qn_ref, kvn_ref, wq_ref, wqs_ref, cosq_ref, sinq_ref, cosk_ref, sink_ref,
                      qext_ref, ckv_ref, kr_ref, dq_ref, dk_ref, dkb_ref, dv_ref, dvb_ref):
    hn = _rms(x_ref[...], g_ref[...]).astype(BF16)
    cq = _rms(_dot(hn, wcq_ref[...]), qn_ref[...]).astype(BF16)
    cosq = jnp.concatenate([cosq_ref[...]] * MLA_HEADS, axis=1)
    sinq = jnp.concatenate([sinq_ref[...]] * MLA_HEADS, axis=1)
    qext_ref[...] = (_dot(cq, wq_ref[...]) * cosq + _dot(cq, wqs_ref[...]) * sinq).astype(BF16)
    ckv_ref[...] = _rms(_dot(hn, wckv_ref[...]), kvn_ref[...])
    kr_ref[...] = _dot(hn, wkr_ref[...]) * cosk_ref[...] + _dot(hn, wkrs_ref[...]) * sink_ref[...]
    dq_ref[...] = (_dot(hn, wdq_ref[...]) * (DIFF_QK ** -0.5 * LOG2E)).astype(BF16)
    dk = _dot(hn, wdk_ref[...])
    dk_ref[...] = dk
    dkb_ref[...] = dk.astype(BF16)
    dv = _dot(hn, wdv_ref[...])
    dv_ref[...] = dv
    dvb = dv.astype(BF16)
    ones = jnp.ones((dvb.shape[0], DIFF_V), BF16)
    dvb_ref[...] = jnp.concatenate(
        [piece for hh in range(DIFF_HEADS) for piece in (dvb[:, hh * DIFF_V:(hh + 1) * DIFF_V], ones)], axis=1)


def _even_proj(x, g, w, tabs, tm):
    m = x.shape[0]
    ins = [x, g, w["wcq"], w["wckv"], w["wkr"], w["wkrs"], w["wdq"], w["wdk"], w["wdv"],
           w["qn"], w["kvn"], w["wq"], w["wqs"], tabs["cosq"], tabs["sinq"], tabs["cosk"], tabs["sink"]]
    row_in = {0: D_MODEL, 13: LANES, 14: LANES, 15: MLA_ROPE, 16: MLA_ROPE}
    in_specs = [_rows(tm, row_in[i]) if i in row_in else _full(a.shape) for i, a in enumerate(ins)]
    outs = [(MLA_HEADS * LANES, BF16), (MLA_KV_RANK, F32), (MLA_ROPE, F32),
            (DIFF_HEADS * DIFF_V, BF16), (DIFF_HEADS * DIFF_V, F32), (DIFF_HEADS * DIFF_V, BF16),
            (DIFF_HEADS * DIFF_V, F32), (DIFF_HEADS * 2 * DIFF_V, BF16)]
    return pl.pallas_call(
        _even_proj_kernel,
        grid=(m // tm,),
        in_specs=in_specs,
        out_specs=[_rows(tm, n) for n, _ in outs],
        out_shape=[jax.ShapeDtypeStruct((m, n), dt) for n, dt in outs],
        compiler_params=_cparams(1),
        name="even_proj",
    )(*ins)


def _odd_proj_kernel(x_ref, g_ref, w_ref, sq_ref, sk_ref, skb_ref, sv_ref, svb_ref,
                     cq_ref, ck_ref, ckb_ref, cv_ref, cvb_ref):
    hn = _rms(x_ref[...], g_ref[...]).astype(BF16)
    width = SB_HEADS * SB_DIM

    def seg(i):
        return _dot(hn, w_ref[:, i * width:(i + 1) * width])

    sq_ref[...] = (seg(0) * (SB_DIM ** -0.5)).astype(BF16)
    for i, (f_ref, b_ref) in ((1, (sk_ref, skb_ref)), (2, (sv_ref, svb_ref)),
                              (4, (ck_ref, ckb_ref)), (5, (cv_ref, cvb_ref))):
        y = seg(i)
        f_ref[...] = y
        b_ref[...] = y.astype(BF16)
    cq_ref[...] = (seg(3) * (CA_DIM ** -0.5)).astype(BF16)


def _odd_proj(x, g, w, tm):
    m = x.shape[0]
    width = SB_HEADS * SB_DIM
    dts = [BF16, F32, BF16, F32, BF16, BF16, F32, BF16, F32, BF16]
    return pl.pallas_call(
        _odd_proj_kernel,
        grid=(m // tm,),
        in_specs=[_rows(tm, D_MODEL), _full(g.shape), _full(w.shape)],
        out_specs=[_rows(tm, width) for _ in dts],
        out_shape=[jax.ShapeDtypeStruct((m, width), dt) for dt in dts],
        compiler_params=_cparams(1),
        name="odd_proj",
    )(x, g, w)


def _kv_up_kernel(ckv_ref, kr_ref, wk_ref, wv_ref, place_ref, ones_ref, kext_ref, vext_ref):
    c = ckv_ref[...].astype(BF16)
    r = kr_ref[...].astype(BF16)
    kext_ref[...] = (_dot(c, wk_ref[...]) + _dot(r, place_ref[...])).astype(BF16)
    vext_ref[...] = (_dot(c, wv_ref[...]) + ones_ref[...]).astype(BF16)


def _kv_up(ckv, kr, w, tm):
    m = ckv.shape[0]
    width = MLA_HEADS * LANES
    return pl.pallas_call(
        _kv_up_kernel,
        grid=(m // tm,),
        in_specs=[_rows(tm, MLA_KV_RANK), _rows(tm, MLA_ROPE), _full(w["wk"].shape), _full(w["wv"].shape),
                  _full(w["place"].shape), _full(w["ones"].shape)],
        out_specs=[_rows(tm, width)] * 2,
        out_shape=[jax.ShapeDtypeStruct((m, width), BF16)] * 2,
        compiler_params=_cparams(1),
        name="mla_kv_up",
    )(ckv, kr, w["wk"], w["wv"], w["place"], w["ones"])


def _attn_specs(tq, skp, v_width=LANES):
    q_spec = pl.BlockSpec((None, tq, LANES), lambda b, h, qi: (b, qi, h))
    k_spec = pl.BlockSpec((None, skp, LANES), lambda b, h, qi: (b, 0, h))
    v_spec = pl.BlockSpec((None, skp, v_width), lambda b, h, qi: (b, 0, h))
    return q_spec, k_spec, v_spec


def _softmax_block(s, m_ref):
    m_old = m_ref[...]
    m_new = jnp.maximum(m_old, jnp.max(s, axis=-1, keepdims=True))
    m_ref[...] = m_new
    return jnp.exp2(m_old - m_new), jnp.exp2(s - jnp.tile(m_new, (1, s.shape[1] // LANES)))


def _sweep_blocks(nb0, near_back, n_sub, n_diag, tkw, tkn, do_block):
    first = jnp.maximum(nb0 - near_back, 0)
    n_wide = (first * tkn) // tkw

    def wide(j, carry):
        do_block(pl.multiple_of(j * tkw, tkw), tkw, None)
        return carry

    def narrow(j, carry):
        do_block(pl.multiple_of(j * tkn, tkn), tkn, None)
        return carry

    lax.fori_loop(0, n_wide, wide, 0)
    lax.fori_loop(n_wide * (tkw // tkn), first, narrow, 0)
    for c in range(-near_back, n_diag):
        kinds = []
        for r in range(n_sub):
            d = c - r
            kinds.append("skip" if d > 0 else None if d < -near_back else d + near_back)

        @pl.when(nb0 + c >= 0)
        def _special(c=c, kinds=kinds):
            do_block(pl.multiple_of((nb0 + c) * tkn, tkn), tkn, kinds)


def _mla_kernel(q_ref, k_ref, v_ref, o_ref, m_ref, acc_ref, *, tqs, n_sub, n_diag, tkw, tkn, q_off, sk_real):
    qi = pl.program_id(2)
    qpos0 = q_off + qi * (tqs * n_sub)
    m_ref[...] = jnp.full(m_ref.shape, NEG, F32)
    acc_ref[...] = jnp.zeros(acc_ref.shape, F32)

    def block(start, width, kinds):
        k = k_ref[pl.ds(start, width), :]
        v = v_ref[pl.ds(start, width), :]
        for r in range(n_sub):
            kind = None if kinds is None else kinds[r]
            if kind == "skip":
                continue
            s = _dot_nt(q_ref[r * tqs:(r + 1) * tqs, :], k)
            if kind is not None:
                qp = qpos0 + r * tqs + lax.broadcasted_iota(jnp.int32, (tqs, width), 0)
                kp = start + lax.broadcasted_iota(jnp.int32, (tqs, width), 1)
                ok = ((kp >> CHUNK_SHIFT) <= (qp >> CHUNK_SHIFT)) & (kp < sk_real)
                s = jnp.where(ok, s, NEG)
            alpha, p = _softmax_block(s, m_ref.at[r])
            acc_ref[r] = acc_ref[r] * alpha + _dot(p.astype(BF16), v)

    _sweep_blocks(qpos0 // tkn, 0, n_sub, n_diag, tkw, tkn, block)

    for r in range(n_sub):
        acc = acc_ref[r]
        lane = lax.broadcasted_iota(jnp.int32, acc.shape, 1)
        denom = jnp.sum(jnp.where(lane == MLA_V, acc, 0.0), axis=-1, keepdims=True)
        o_ref[r * tqs:(r + 1) * tqs, :] = jnp.where(lane < MLA_V, acc / denom, 0.0).astype(BF16)


def _mla_attn(qext, kext, vext, *, tqs, n_sub, n_diag, tkw, tkn, q_off, sk_real):
    b, sq, _ = qext.shape
    skp = kext.shape[1]
    tq = tqs * n_sub
    q_spec, k_spec, v_spec = _attn_specs(tq, skp)
    kern = functools.partial(_mla_kernel, tqs=tqs, n_sub=n_sub, n_diag=n_diag, tkw=tkw, tkn=tkn, q_off=q_off,
                             sk_real=sk_real)
    return pl.pallas_call(
        kern,
        grid=(b, MLA_HEADS, sq // tq),
        in_specs=[q_spec, k_spec, v_spec],
        out_specs=q_spec,
        out_shape=jax.ShapeDtypeStruct((b, sq, MLA_HEADS * LANES), BF16),
        scratch_shapes=[pltpu.VMEM((n_sub, tqs, LANES), F32), pltpu.VMEM((n_sub, tqs, LANES), F32)],
        compiler_params=_cparams(3),
        name="mla_attn",
    )(qext, kext, vext)


def _diff_kernel(t5_ref, q_ref, k_ref, v_ref, bkt_ref, lamv_ref, subln_ref, o_ref,
                 bias_ref, m_ref, acc_ref, *, tqs, n_sub, n_diag, tkw, tkn, q_off, n_near, near_back, far_bucket,
                 lam_init):
    first = (pl.program_id(0) == 0) & (pl.program_id(1) == 0) & (pl.program_id(2) == 0)

    @pl.when(first)
    def _build_bias():
        for r in range(n_near):
            bkt = bkt_ref[r]
            vals = [jnp.full((tqs, tkn), NEG, F32) for _ in range(DIFF_HEADS)]
            for t in range(T5_BUCKETS):
                hit = bkt == t
                for hh in range(DIFF_HEADS):
                    vals[hh] = jnp.where(hit, (t5_ref[t, hh] - t5_ref[far_bucket, hh]) * LOG2E, vals[hh])
            for hh in range(DIFF_HEADS):
                bias_ref[r, hh] = vals[hh]

    h = pl.program_id(1)
    qi = pl.program_id(2)
    qpos0 = q_off + qi * (tqs * n_sub)
    q = q_ref[...]
    lane = lax.broadcasted_iota(jnp.int32, q.shape, 1)
    qm = [_keep_lanes(q, lane < DIFF_QK), _keep_lanes(q, lane >= DIFF_QK)]
    m_ref[...] = jnp.full(m_ref.shape, NEG, F32)
    acc_ref[...] = jnp.zeros(acc_ref.shape, F32)

    def block(start, width, kinds):
        k = k_ref[pl.ds(start, width), :]
        v = v_ref[pl.ds(start, width), :]
        for r in range(n_sub):
            kind = None if kinds is None else kinds[r]
            if kind == "skip":
                continue
            for mi in range(2):
                s = _dot_nt(qm[mi][r * tqs:(r + 1) * tqs, :], k)
                if kind is not None:
                    s = s + bias_ref[kind, h]
                alpha, p = _softmax_block(s, m_ref.at[mi, r])
                acc_ref[mi, r] = acc_ref[mi, r] * jnp.tile(alpha, (1, 2)) + _dot(p.astype(BF16), v)

    _sweep_blocks(qpos0 // tkn, near_back, n_sub, n_diag, tkw, tkn, block)

    lv = lamv_ref[...]
    lam = (jnp.exp(jnp.sum(lv[0:1] * lv[1:2], axis=-1, keepdims=True))
           - jnp.exp(jnp.sum(lv[2:3] * lv[3:4], axis=-1, keepdims=True)) + lam_init)
    for r in range(n_sub):
        a0, a1 = acc_ref[0, r], acc_ref[1, r]
        o = a0[:, :DIFF_V] / a0[:, DIFF_V:] - lam * (a1[:, :DIFF_V] / a1[:, DIFF_V:])
        o_ref[r * tqs:(r + 1) * tqs, :] = (_rms(o, subln_ref[...]) * (1.0 - lam_init)).astype(BF16)


def _diff_attn(t5, dq, dk, dv, bkt, lamv, subln, *, tqs, n_sub, n_diag, tkw, tkn, q_off, near_back, far_bucket,
               lam_init):
    b, sq, _ = dq.shape
    skp = dk.shape[1]
    n_near = bkt.shape[0]
    tq = tqs * n_sub
    q_spec, k_spec, v_spec = _attn_specs(tq, skp, 2 * DIFF_V)
    kern = functools.partial(_diff_kernel, tqs=tqs, n_sub=n_sub, n_diag=n_diag, tkw=tkw, tkn=tkn, q_off=q_off,
                             n_near=n_near, near_back=near_back, far_bucket=far_bucket, lam_init=lam_init)
    return pl.pallas_call(
        kern,
        grid=(b, DIFF_HEADS, sq // tq),
        in_specs=[pl.BlockSpec(memory_space=pltpu.SMEM), q_spec, k_spec, v_spec,
                  _full(bkt.shape), _full(lamv.shape), _full(subln.shape)],
        out_specs=q_spec,
        out_shape=jax.ShapeDtypeStruct((b, sq, DIFF_HEADS * DIFF_V), BF16),
        scratch_shapes=[pltpu.VMEM((n_near, DIFF_HEADS, tqs, tkn), F32), pltpu.VMEM((2, n_sub, tqs, LANES), F32),
                        pltpu.VMEM((2, n_sub, tqs, 2 * DIFF_V), F32)],
        compiler_params=_cparams(3),
        name="diff_attn",
    )(t5, dq, dk, dv, bkt, lamv, subln)


def _sb_kernel(q_ref, k_ref, v_ref, tri_ref, o_ref, run_ref, acc_ref, *, tq, tk, q_off, n_masked):
    qi = pl.program_id(2)
    qpos0 = q_off + qi * tq
    top = (qpos0 + tq - 1) // tk
    q = q_ref[...]
    lane = lax.broadcasted_iota(jnp.int32, q.shape, 1)
    tri = tri_ref[...]

    def block(kj, qh, masked):
        ks = pl.multiple_of(kj * tk, tk)
        z = _dot_nt(qh, k_ref[pl.ds(ks, tk), :])
        t = jnp.log1p(jnp.exp(-jnp.abs(z)))
        log_1m = -(jnp.maximum(z, 0.0) + t)
        log_beta = jnp.minimum(z, 0.0) - t
        if masked:
            qp = qpos0 + lax.broadcasted_iota(jnp.int32, (tq, tk), 0)
            kp = kj * tk + lax.broadcasted_iota(jnp.int32, (tq, tk), 1)
            ok = kp < qp
            log_1m = jnp.where(ok, log_1m, 0.0)
        hi = log_1m.astype(BF16)
        lo = (log_1m - hi.astype(F32)).astype(BF16)
        after = _dot(hi, tri) + _dot(lo, tri) + run_ref[...]
        a = jnp.exp(log_beta + after)
        if masked:
            a = jnp.where(ok, a, 0.0)
        acc_ref[...] += _dot(a.astype(BF16), v_ref[pl.ds(ks, tk), :])
        run_ref[...] += jnp.sum(log_1m, axis=-1, keepdims=True)

    outs = []
    for hh in range(2):
        qh = _keep_lanes(q, (lane < SB_DIM) if hh == 0 else (lane >= SB_DIM))
        run_ref[...] = jnp.zeros(run_ref.shape, F32)
        acc_ref[...] = jnp.zeros(acc_ref.shape, F32)
        for r in range(n_masked):
            block(top - r, qh, True)

        def cond(kj):
            return (kj >= 0) & (jnp.max(run_ref[...]) > SB_EXIT)

        def body(kj, qh=qh):
            block(kj, qh, False)
            return kj - 1

        lax.while_loop(cond, body, top - n_masked)
        outs.append(acc_ref[...])
    o_ref[...] = jnp.where(lane < SB_DIM, outs[0], outs[1]).astype(BF16)


def _sb_attn(sq_, sk_, sv_, tri, *, tq, tk, q_off, n_masked):
    b, sq, _ = sq_.shape
    skp = sk_.shape[1]
    q_spec, kv_spec, _ = _attn_specs(tq, skp)
    kern = functools.partial(_sb_kernel, tq=tq, tk=tk, q_off=q_off, n_masked=n_masked)
    return pl.pallas_call(
        kern,
        grid=(b, SB_HEADS // 2, sq // tq),
        in_specs=[q_spec, kv_spec, kv_spec, _full(tri.shape)],
        out_specs=q_spec,
        out_shape=jax.ShapeDtypeStruct((b, sq, SB_HEADS * SB_DIM), BF16),
        scratch_shapes=[pltpu.VMEM((tq, 1), F32), pltpu.VMEM((tq, LANES), F32)],
        compiler_params=_cparams(3),
        name="sb_attn",
    )(sq_, sk_, sv_, tri)


CA_ROW_GROUP = 16


def _ca_kernel(tab_ref, q_ref, k_ref, v_ref, o_ref, bias_ref, *, tq, win, win_real, q_off):
    first = (pl.program_id(0) == 0) & (pl.program_id(1) == 0) & (pl.program_id(2) == 0)
    n_shift = -(-(tq + win) // LANES) * LANES
    n_rel = n_shift + LANES

    @pl.when(first)
    def _build_bias():
        x = lax.broadcasted_iota(jnp.int32, (CA_HEADS, n_rel), 1)
        idx = jnp.clip(x - (tq - 1) - CA_BAND, -CA_MAX_REL, CA_MAX_REL) + CA_MAX_REL
        f = jnp.zeros((CA_HEADS, n_rel), F32)
        for t in range(2 * CA_MAX_REL + 1):
            f = jnp.where(idx == t, tab_ref[:, t:t + 1], f)
        g = CA_ROW_GROUP
        i_loc = lax.broadcasted_iota(jnp.int32, (g, win), 0)
        j_loc = lax.broadcasted_iota(jnp.int32, (g, win), 1)
        for hh in range(CA_HEADS):
            fh = f[hh:hh + 1, :]
            shifted = jnp.concatenate([fh[:, g - 1 - bb:g - 1 - bb + n_shift] for bb in range(g)], axis=0)
            for a in range(tq // g):
                start = tq - g * a - g
                tile = shifted[:, start:start + win]
                i = i_loc + g * a
                kc = (j_loc >> CHUNK_SHIFT) - CA_LEFT_CHUNKS
                qc = i >> CHUNK_SHIFT
                ok = (kc <= qc) & (kc >= qc - CA_LEFT_CHUNKS) & (j_loc < win_real)
                bias_ref[hh, g * a:g * a + g, :] = jnp.where(ok, tile, NEG)

    hp = pl.program_id(1)
    qi = pl.program_id(2)
    qpos0 = q_off + qi * tq
    ws = pl.multiple_of(qi * tq, tq)
    q = q_ref[...]
    k = k_ref[pl.ds(ws, win), :]
    v = v_ref[pl.ds(ws, win), :]
    lane = lax.broadcasted_iota(jnp.int32, q.shape, 1)
    kpos = qpos0 - CA_BAND + lax.broadcasted_iota(jnp.int32, (tq, win), 1)
    outs = []
    for hh in range(2):
        qh = _keep_lanes(q, (lane < CA_DIM) if hh == 0 else (lane >= CA_DIM))
        s = _dot_nt(qh, k) + bias_ref[2 * hp + hh]
        s = jnp.where(kpos >= 0, s, NEG)
        m = jnp.max(s, axis=-1, keepdims=True)
        p = jnp.exp(s - m)
        denom = jnp.sum(p, axis=-1, keepdims=True)
        outs.append(_dot(p.astype(BF16), v) / denom)
    o_ref[...] = jnp.where(lane < CA_DIM, outs[0], outs[1]).astype(BF16)


def _ca_attn(tab_t, cq, ck, cv, *, tq, win, win_real, q_off):
    b, sq, _ = cq.shape
    skp = ck.shape[1]
    q_spec, kv_spec, _ = _attn_specs(tq, skp)
    kern = functools.partial(_ca_kernel, tq=tq, win=win, win_real=win_real, q_off=q_off)
    return pl.pallas_call(
        kern,
        grid=(b, CA_HEADS // 2, sq // tq),
        in_specs=[_full(tab_t.shape), q_spec, kv_spec, kv_spec],
        out_specs=q_spec,
        out_shape=jax.ShapeDtypeStruct((b, sq, CA_HEADS * CA_DIM), BF16),
        scratch_shapes=[pltpu.VMEM((CA_HEADS, tq, win), F32)],
        compiler_params=_cparams(3),
        name="ca_attn",
    )(tab_t, cq, ck, cv)


def _post_kernel(x_ref, oa_ref, ob_ref, wa_ref, wb_ref, g_ref, w1_ref, w2_ref, gf_ref, y_ref,
                 x1_ref, hn_ref, acc_ref, *, final_norm):
    j = pl.program_id(1)

    @pl.when(j == 0)
    def _mix():
        x1 = x_ref[...] + _dot(oa_ref[...], wa_ref[...]) + _dot(ob_ref[...], wb_ref[...])
        x1_ref[...] = x1
        hn_ref[...] = _rms(x1, g_ref[...]).astype(BF16)
        acc_ref[...] = jnp.zeros(acc_ref.shape, F32)

    a = jnp.maximum(_dot(hn_ref[...], w1_ref[...]), 0.0)
    acc_ref[...] += _dot((a * a).astype(BF16), w2_ref[...])

    @pl.when(j == pl.num_programs(1) - 1)
    def _finish():
        y = x1_ref[...] + acc_ref[...]
        if final_norm:
            y = _rms(y, gf_ref[...])
        y_ref[...] = y


def _post(x, oa, ob, wa, wb, g, w1, w2, gf, *, tm, tf, final_norm):
    m = x.shape[0]
    d_ff = w1.shape[1]
    rows = lambda width: pl.BlockSpec((tm, width), lambda i, j: (i, 0))
    return pl.pallas_call(
        functools.partial(_post_kernel, final_norm=final_norm),
        grid=(m // tm, d_ff // tf),
        in_specs=[rows(D_MODEL), rows(oa.shape[1]), rows(ob.shape[1]), _full(wa.shape), _full(wb.shape),
                  _full(g.shape), pl.BlockSpec((D_MODEL, tf), lambda i, j: (0, j)),
                  pl.BlockSpec((tf, D_MODEL), lambda i, j: (j, 0)), _full(gf.shape)],
        out_specs=rows(D_MODEL),
        out_shape=jax.ShapeDtypeStruct((m, D_MODEL), F32),
        scratch_shapes=[pltpu.VMEM((tm, D_MODEL), F32), pltpu.VMEM((tm, D_MODEL), BF16),
                        pltpu.VMEM((tm, D_MODEL), F32)],
        compiler_params=_cparams(2),
        name="post_mlp",
    )(x, oa, ob, wa, wb, g, w1, w2, gf)


def _t5_bucket(rel):
    nb = T5_BUCKETS // 2
    max_exact = nb // 2
    ret = jnp.where(rel > 0, nb, 0)
    n = jnp.abs(rel)
    nf = jnp.maximum(n, 1).astype(F32)
    large = max_exact + (jnp.log(nf / max_exact) / math.log(T5_MAX_DIST / max_exact) * (nb - max_exact)).astype(jnp.int32)
    large = jnp.minimum(large, nb - 1)
    return ret + jnp.where(n < max_exact, n, large)


def _rope_tables(pos):
    half = MLA_ROPE // 2
    inv = ROPE_BASE ** (-jnp.arange(half, dtype=F32) / half)
    ang = pos.astype(F32)[:, None] * inv[None, :]
    cos, sin = jnp.cos(ang), jnp.sin(ang)
    cosk = jnp.concatenate([cos, cos], axis=1)
    sink = jnp.concatenate([-sin, sin], axis=1)
    n = pos.shape[0]
    pad = jnp.zeros((n, LANES - MLA_NOPE - MLA_ROPE), F32)
    qscale = (MLA_NOPE + MLA_ROPE) ** -0.5 * LOG2E
    cosq = jnp.concatenate([jnp.ones((n, MLA_NOPE), F32), cosk, pad], axis=1) * qscale
    sinq = jnp.concatenate([jnp.zeros((n, MLA_NOPE), F32), sink, pad], axis=1) * qscale
    return {"cosq": cosq, "sinq": sinq, "cosk": cosk, "sink": sink}


def _swap_halves(w):
    half = w.shape[-1] // 2
    return jnp.concatenate([w[..., half:], w[..., :half]], axis=-1)


def _even_weights(w_in, q_norm, kv_norm, w_uq, w_ukv, w_out):
    sizes = [MLA_Q_RANK, MLA_KV_RANK, MLA_ROPE, DIFF_HEADS * 2 * DIFF_QK, DIFF_HEADS * 2 * DIFF_QK,
             DIFF_HEADS * DIFF_V]
    offs = np.cumsum([0] + sizes)
    wcq, wckv, wkr, wdq, wdk, wdv = (w_in[:, offs[i]:offs[i + 1]].astype(BF16) for i in range(6))
    uq = w_uq.reshape(MLA_Q_RANK, MLA_HEADS, MLA_NOPE + MLA_ROPE)
    zq = jnp.zeros((MLA_Q_RANK, MLA_HEADS, LANES - MLA_NOPE - MLA_ROPE), F32)
    wq = jnp.concatenate([uq, zq], axis=-1)
    wqs = jnp.concatenate([jnp.zeros_like(uq[..., :MLA_NOPE]), _swap_halves(uq[..., MLA_NOPE:]), zq], axis=-1)
    ukv = w_ukv.reshape(MLA_KV_RANK, MLA_HEADS, MLA_NOPE + MLA_V)
    zk = jnp.zeros((MLA_KV_RANK, MLA_HEADS, LANES - MLA_NOPE), F32)
    wk = jnp.concatenate([ukv[..., :MLA_NOPE], zk], axis=-1)
    wv = jnp.concatenate([ukv[..., MLA_NOPE:], jnp.zeros((MLA_KV_RANK, MLA_HEADS, LANES - MLA_V), F32)], axis=-1)
    place = np.zeros((MLA_ROPE, MLA_HEADS, LANES), np.float32)
    ones = np.zeros((1, MLA_HEADS, LANES), np.float32)
    for hh in range(MLA_HEADS):
        place[np.arange(MLA_ROPE), hh, MLA_NOPE + np.arange(MLA_ROPE)] = 1.0
        ones[0, hh, MLA_V] = 1.0
    flat = lambda a: a.reshape(a.shape[0], MLA_HEADS * LANES)
    wo_mla = w_out[:MLA_HEADS * MLA_V].reshape(MLA_HEADS, MLA_V, D_MODEL)
    wo_mla = jnp.concatenate([wo_mla, jnp.zeros((MLA_HEADS, LANES - MLA_V, D_MODEL), F32)], axis=1)
    return {
        "wcq": wcq, "wckv": wckv, "wkr": wkr, "wkrs": _swap_halves(wkr), "wdq": wdq, "wdk": wdk, "wdv": wdv,
        "qn": q_norm.reshape(1, -1), "kvn": kv_norm.reshape(1, -1),
        "wq": flat(wq).astype(BF16), "wqs": flat(wqs).astype(BF16),
        "wk": flat(wk).astype(BF16), "wv": flat(wv).astype(BF16),
        "place": jnp.asarray(flat(place), BF16), "ones": jnp.asarray(flat(ones), F32),
        "wo_mla": wo_mla.reshape(MLA_HEADS * LANES, D_MODEL).astype(BF16),
        "wo_diff": w_out[MLA_HEADS * MLA_V:].astype(BF16),
    }


def _pad_rows(a, total, front=0):
    back = total - front - a.shape[1]
    return jnp.pad(a, ((0, 0), (front, back), (0, 0)))


def _round_up(n, mult):
    return -(-n // mult) * mult


def _diff_buckets(tq, tk, q_off, sk_real, skp):
    near_back = -((q_off - (T5_MAX_DIST - 1)) // tk - q_off // tk)
    last = (_round_up(q_off + tq, CHUNK) - 1) // tk
    last = min(last, skp // tk - 1)
    n_near = last - (q_off // tk - near_back) + 1
    i = np.arange(tq)[:, None]
    mats = []
    for r in range(n_near):
        kp = (q_off // tk - near_back + r) * tk + np.arange(tk)[None, :]
        qp = q_off + i
        ok = ((kp >> CHUNK_SHIFT) <= (qp >> CHUNK_SHIFT)) & (kp < sk_real)
        bkt = _t5_bucket(jnp.asarray(kp - qp, jnp.int32))
        mats.append(jnp.where(jnp.asarray(ok), bkt, -1))
    return jnp.stack(mats).astype(jnp.int32), near_back


def _trunk(x, q_off, caches, prm, cfg):
    b, sq, _ = x.shape
    m = b * sq
    tq, tk, tm, tf = cfg["tq"], cfg["tk"], cfg["tm"], cfg["tf"]
    sk_real = q_off + sq
    skp = _round_up(sk_real, tk)
    pos = q_off + jnp.arange(sq, dtype=jnp.int32)
    tabs = {k: jnp.tile(v, (b, 1)) for k, v in _rope_tables(pos).items()}
    x2 = x.reshape(m, D_MODEL)

    def with_past(past, new, dtype):
        new = new.reshape(b, sq, -1)
        if past is None:
            return new.astype(dtype)
        return jnp.concatenate([past.reshape(b, past.shape[1], -1).astype(dtype), new.astype(dtype)], axis=1)

    ew = prm["even"]
    qext, ckv, kr, dq, dk, dkb, dv, dvb = _even_proj(x2, prm["norm_mix"][0:1], ew, tabs, tm)
    past = (None,) * 4 if caches is None else tuple(c[0] for c in caches[:4])
    ckv_all = _pad_rows(with_past(past[0], ckv, F32), skp)
    kr_all = _pad_rows(with_past(past[1], kr, F32), skp)
    kext, vext = _kv_up(ckv_all.reshape(b * skp, -1), kr_all.reshape(b * skp, -1), ew, cfg["tm_kv"])
    kext = kext.reshape(b, skp, -1)
    vext = vext.reshape(b, skp, -1)
    tqs, n_sub = cfg["tqs"], cfg["n_sub"]
    assert n_sub == 1 or tqs == tk
    n_diag = (_round_up(q_off + tqs * n_sub, CHUNK) - 1) // tk - q_off // tk + 1
    tiles = dict(tqs=tqs, n_sub=n_sub, n_diag=n_diag, tkw=cfg["tkw"], tkn=tk, q_off=q_off)
    o_mla = _mla_attn(qext.reshape(b, sq, -1), kext, vext, sk_real=sk_real, **tiles)
    dk_all = _pad_rows(with_past(past[2], dkb, BF16), skp)
    past_dv = past[3]
    if past_dv is not None:
        past_dv = jnp.concatenate([past_dv.astype(BF16), jnp.ones(past_dv.shape, BF16)], axis=-1)
    dv_all = _pad_rows(with_past(past_dv, dvb, BF16), skp)
    bkt, near_back = _diff_buckets(tqs, tk, q_off, sk_real, skp)
    assert bkt.shape[0] == near_back + n_diag - (n_sub - 1)
    lam_init = 0.8 - 0.6 * math.exp(-0.3 * 0)
    o_diff = _diff_attn(prm["t5"], dq.reshape(b, sq, -1), dk_all, dv_all, bkt, prm["lam_vecs"], prm["subln"],
                        near_back=near_back, far_bucket=T5_BUCKETS // 2 - 1, lam_init=lam_init, **tiles)
    x2 = _post(x2, o_mla.reshape(m, -1), o_diff.reshape(m, -1), ew["wo_mla"], ew["wo_diff"],
               prm["norm_ff"][0:1], prm["w_ff1"][0], prm["w_ff2"][0], prm["final_norm"],
               tm=tm, tf=tf, final_norm=False)
    new_even = (ckv.reshape(1, b, sq, MLA_KV_RANK), kr.reshape(1, b, sq, MLA_ROPE),
                dk.reshape(1, b, sq, DIFF_HEADS, 2 * DIFF_QK), dv.reshape(1, b, sq, DIFF_HEADS, DIFF_V))

    sq_, sk_, skb, sv_, svb, cq, ck, ckb, cv, cvb = _odd_proj(x2, prm["norm_mix"][1:2], prm["w_in_odd"], tm)
    past = (None,) * 4 if caches is None else tuple(c[0] for c in caches[4:])
    sk_all = _pad_rows(with_past(past[0], skb, BF16), skp)
    sv_all = _pad_rows(with_past(past[1], svb, BF16), skp)
    n_masked = (q_off + tq - 1) // tk - q_off // tk + 1
    o_sb = _sb_attn(sq_.reshape(b, sq, -1), sk_all, sv_all, prm["tri"][tk], tq=tq, tk=tk, q_off=q_off,
                    n_masked=n_masked)
    win_real = tq + CA_BAND
    win = _round_up(win_real, LANES)
    if caches is None:
        ck_all = _pad_rows(ckb.reshape(b, sq, -1), sq + CA_BAND + win - win_real, front=CA_BAND)
        cv_all = _pad_rows(cvb.reshape(b, sq, -1), sq + CA_BAND + win - win_real, front=CA_BAND)
    else:
        ck_all = _pad_rows(with_past(past[2], ckb, BF16), win)
        cv_all = _pad_rows(with_past(past[3], cvb, BF16), win)
    o_ca = _ca_attn(prm["ca_tab_t"], cq.reshape(b, sq, -1), ck_all, cv_all, tq=tq, win=win, win_real=win_real,
                    q_off=q_off)
    x2 = _post(x2, o_sb.reshape(m, -1), o_ca.reshape(m, -1), prm["wo_sb"], prm["wo_ca"],
               prm["norm_ff"][1:2], prm["w_ff1"][1], prm["w_ff2"][1], prm["final_norm"],
               tm=tm, tf=tf, final_norm=True)

    heads = lambda a: a.reshape(b, sq, SB_HEADS, SB_DIM)
    if caches is None:
        nb = min(CA_BAND, sq)
        cak, cav = heads(ck)[:, sq - nb:], heads(cv)[:, sq - nb:]
    else:
        nb = past[2].shape[1]
        cak = jnp.concatenate([past[2], heads(ck)], axis=1)[:, sq:]
        cav = jnp.concatenate([past[3], heads(cv)], axis=1)[:, sq:]
        assert cak.shape[1] == nb
    new_odd = (heads(sk_)[None], heads(sv_)[None], cak[None], cav[None])
    return x2.reshape(b, sq, D_MODEL), new_even + new_odd


def _tri(tk):
    j = np.arange(tk)[:, None]
    s = np.arange(tk)[None, :]
    return jnp.asarray((j > s).astype(np.float32), BF16)


def kernel(x_prompt, x_sample, cache_mla_ckv, cache_mla_krope, cache_diff_k, cache_diff_v, cache_sb_k, cache_sb_v, cache_ca_k, cache_ca_v, norm_mix, norm_ff, w_in_even, mla_q_norm, mla_kv_norm, mla_w_uq, mla_w_ukv, diff_lambda_vecs, diff_subln, t5_bias, w_out_even, w_in_odd, ca_rel_bias, w_out_odd, w_ff1, w_ff2, final_norm):
    seq = x_prompt.shape[1]
    dec_seq = x_sample.shape[1]
    past_len = cache_mla_ckv.shape[2]
    assert cache_ca_k.shape[2] == CA_BAND and past_len % CHUNK == 0

    cfg_p = {"tq": 256, "tqs": 256, "n_sub": 4, "tk": 256, "tkw": 1024, "tm": min(512, seq),
             "tm_kv": min(512, seq), "tf": 512}
    cfg_s = {"tq": dec_seq, "tqs": dec_seq, "n_sub": 1, "tk": 128, "tkw": 512,
             "tm": x_sample.shape[0] * dec_seq, "tm_kv": 128, "tf": 512}
    n_sb = SB_HEADS * SB_DIM
    prm = {
        "norm_mix": norm_mix, "norm_ff": norm_ff, "final_norm": final_norm.reshape(1, -1),
        "even": _even_weights(w_in_even[0], mla_q_norm[0], mla_kv_norm[0], mla_w_uq[0], mla_w_ukv[0],
                              w_out_even[0]),
        "t5": t5_bias, "lam_vecs": diff_lambda_vecs[0], "subln": diff_subln[0].reshape(1, -1),
        "w_in_odd": w_in_odd[0].astype(BF16), "ca_tab_t": ca_rel_bias[0].T,
        "wo_sb": w_out_odd[0][:n_sb].astype(BF16), "wo_ca": w_out_odd[0][n_sb:].astype(BF16),
        "w_ff1": w_ff1.astype(BF16), "w_ff2": w_ff2.astype(BF16),
        "tri": {tk: _tri(tk) for tk in {cfg_p["tk"], cfg_s["tk"]}},
    }
    y_prompt, new_p = _trunk(x_prompt, 0, None, prm, cfg_p)
    caches = (cache_mla_ckv, cache_mla_krope, cache_diff_k, cache_diff_v,
              cache_sb_k, cache_sb_v, cache_ca_k, cache_ca_v)
    y_sample, new_s = _trunk(x_sample, past_len, caches, prm, cfg_s)
    return (y_prompt, y_sample) + tuple(new_p) + tuple(new_s)
```

```python
import functools
import math

import numpy as np
import jax
import jax.numpy as jnp
from jax import lax
from jax.experimental import pallas as pl
from jax.experimental.pallas import tpu as pltpu

F32 = jnp.float32
BF16 = jnp.bfloat16

D_MODEL = 1024
CHUNK = 64
CHUNK_SHIFT = 6
EPS = 1e-6
NEG = -1e30

MLA_HEADS = 8
MLA_Q_RANK = 256
MLA_KV_RANK = 128
MLA_NOPE = 64
MLA_ROPE = 32
MLA_V = 64
ROPE_BASE = 10000.0
DIFF_HEADS = 4
DIFF_QK = 64
DIFF_V = 2 * DIFF_QK
T5_BUCKETS = 32
T5_MAX_DIST = 128
SB_HEADS = 8
SB_DIM = 64
CA_HEADS = 8
CA_DIM = 64
CA_LEFT_CHUNKS = 8
CA_BAND = CA_LEFT_CHUNKS * CHUNK
CA_MAX_REL = 128

LANES = 128
VMEM_LIMIT = 48 * 1024 * 1024
LOG2E = math.log2(math.e)
SB_EXIT = -104.0


def _cparams(n_axes):
    return pltpu.CompilerParams(dimension_semantics=("arbitrary",) * n_axes,
                                vmem_limit_bytes=VMEM_LIMIT)


def _rms(x, g):
    return x * lax.rsqrt(jnp.mean(x * x, axis=-1, keepdims=True) + EPS) * g


def _dot(a, b):
    return jnp.dot(a, b, preferred_element_type=F32)


def _dot_nt(a, b):
    return lax.dot_general(a, b, (((1,), (1,)), ((), ())), preferred_element_type=F32)


def _keep_lanes(q, keep):
    return jnp.where(keep, q.astype(F32), 0.0).astype(BF16)


def _full(shape):
    n = len(shape)
    return pl.BlockSpec(shape, lambda *_: (0,) * n)


def _rows(tm, width):
    return pl.BlockSpec((tm, width), lambda i: (i, 0))


def _even_proj_kernel(x_ref, g_ref, wcq_ref, wckv_ref, wkr_ref, wkrs_ref, wdq_ref, wdk_ref, wdv_ref,
                      qn_ref, kvn_ref, wq_ref, wqs_ref, cosq_ref, sinq_ref, cosk_ref, sink_ref,
                      qext_ref, ckv_ref, kr_ref, dq_ref, dk_ref, dkb_ref, dv_ref, dvb_ref):
    hn = _rms(x_ref[...], g_ref[...]).astype(BF16)
    cq = _rms(_dot(hn, wcq_ref[...]), qn_ref[...]).astype(BF16)
    cosq = jnp.concatenate([cosq_ref[...]] * MLA_HEADS, axis=1)
    sinq = jnp.concatenate([sinq_ref[...]] * MLA_HEADS, axis=1)
    qext_ref[...] = (_dot(cq, wq_ref[...]) * cosq + _dot(cq, wqs_ref[...]) * sinq).astype(BF16)
    ckv_ref[...] = _rms(_dot(hn, wckv_ref[...]), kvn_ref[...])
    kr_ref[...] = _dot(hn, wkr_ref[...]) * cosk_ref[...] + _dot(hn, wkrs_ref[...]) * sink_ref[...]
    dq_ref[...] = (_dot(hn, wdq_ref[...]) * (DIFF_QK ** -0.5 * LOG2E)).astype(BF16)
    dk = _dot(hn, wdk_ref[...])
    dk_ref[...] = dk
    dkb_ref[...] = dk.astype(BF16)
    dv = _dot(hn, wdv_ref[...])
    dv_ref[...] = dv
    dvb = dv.astype(BF16)
    ones = jnp.ones((dvb.shape[0], DIFF_V), BF16)
    dvb_ref[...] = jnp.concatenate(
        [piece for hh in range(DIFF_HEADS) for piece in (dvb[:, hh * DIFF_V:(hh + 1) * DIFF_V], ones)], axis=1)


def _even_proj(x, g, w, tabs, tm):
    m = x.shape[0]
    ins = [x, g, w["wcq"], w["wckv"], w["wkr"], w["wkrs"], w["wdq"], w["wdk"], w["wdv"],
           w["qn"], w["kvn"], w["wq"], w["wqs"], tabs["cosq"], tabs["sinq"], tabs["cosk"], tabs["sink"]]
    row_in = {0: D_MODEL, 13: LANES, 14: LANES, 15: MLA_ROPE, 16: MLA_ROPE}
    in_specs = [_rows(tm, row_in[i]) if i in row_in else _full(a.shape) for i, a in enumerate(ins)]
    outs = [(MLA_HEADS * LANES, BF16), (MLA_KV_RANK, F32), (MLA_ROPE, F32),
            (DIFF_HEADS * DIFF_V, BF16), (DIFF_HEADS * DIFF_V, F32), (DIFF_HEADS * DIFF_V, BF16),
            (DIFF_HEADS * DIFF_V, F32), (DIFF_HEADS * 2 * DIFF_V, BF16)]
    return pl.pallas_call(
        _even_proj_kernel,
        grid=(m // tm,),
        in_specs=in_specs,
        out_specs=[_rows(tm, n) for n, _ in outs],
        out_shape=[jax.ShapeDtypeStruct((m, n), dt) for n, dt in outs],
        compiler_params=_cparams(1),
        name="even_proj",
    )(*ins)


def _odd_proj_kernel(x_ref, g_ref, w_ref, sq_ref, sk_ref, skb_ref, sv_ref, svb_ref,
                     cq_ref, ck_ref, ckb_ref, cv_ref, cvb_ref):
    hn = _rms(x_ref[...], g_ref[...]).astype(BF16)
    width = SB_HEADS * SB_DIM

    def seg(i):
        return _dot(hn, w_ref[:, i * width:(i + 1) * width])

    sq_ref[...] = (seg(0) * (SB_DIM ** -0.5)).astype(BF16)
    for i, (f_ref, b_ref) in ((1, (sk_ref, skb_ref)), (2, (sv_ref, svb_ref)),
                              (4, (ck_ref, ckb_ref)), (5, (cv_ref, cvb_ref))):
        y = seg(i)
        f_ref[...] = y
        b_ref[...] = y.astype(BF16)
    cq_ref[...] = (seg(3) * (CA_DIM ** -0.5)).astype(BF16)


def _odd_proj(x, g, w, tm):
    m = x.shape[0]
    width = SB_HEADS * SB_DIM
    dts = [BF16, F32, BF16, F32, BF16, BF16, F32, BF16, F32, BF16]
    return pl.pallas_call(
        _odd_proj_kernel,
        grid=(m // tm,),
        in_specs=[_rows(tm, D_MODEL), _full(g.shape), _full(w.shape)],
        out_specs=[_rows(tm, width) for _ in dts],
        out_shape=[jax.ShapeDtypeStruct((m, width), dt) for dt in dts],
        compiler_params=_cparams(1),
        name="odd_proj",
    )(x, g, w)


def _kv_up_kernel(ckv_ref, kr_ref, wk_ref, wv_ref, place_ref, ones_ref, kext_ref, vext_ref):
    c = ckv_ref[...].astype(BF16)
    r = kr_ref[...].astype(BF16)
    kext_ref[...] = (_dot(c, wk_ref[...]) + _dot(r, place_ref[...])).astype(BF16)
    vext_ref[...] = (_dot(c, wv_ref[...]) + ones_ref[...]).astype(BF16)


def _kv_up(ckv, kr, w, tm):
    m = ckv.shape[0]
    width = MLA_HEADS * LANES
    return pl.pallas_call(
        _kv_up_kernel,
        grid=(m // tm,),
        in_specs=[_rows(tm, MLA_KV_RANK), _rows(tm, MLA_ROPE), _full(w["wk"].shape), _full(w["wv"].shape),
                  _full(w["place"].shape), _full(w["ones"].shape)],
        out_specs=[_rows(tm, width)] * 2,
        out_shape=[jax.ShapeDtypeStruct((m, width), BF16)] * 2,
        compiler_params=_cparams(1),
        name="mla_kv_up",
    )(ckv, kr, w["wk"], w["wv"], w["place"], w["ones"])


def _attn_specs(tq, skp, width=LANES, v_width=None, kv_buffers=None):
    kw = {} if kv_buffers is None else {"pipeline_mode": pl.Buffered(kv_buffers)}
    q_spec = pl.BlockSpec((None, tq, width), lambda b, h, qi: (b, qi, h))
    k_spec = pl.BlockSpec((None, skp, width), lambda b, h, qi: (b, 0, h), **kw)
    v_spec = pl.BlockSpec((None, skp, v_width or width), lambda b, h, qi: (b, 0, h), **kw)
    return q_spec, k_spec, v_spec


def _softmax_block(s, m_ref):
    m_old = m_ref[...]
    m_new = jnp.maximum(m_old, jnp.max(s, axis=-1, keepdims=True))
    m_ref[...] = m_new
    return jnp.exp2(m_old - m_new), jnp.exp2(s - jnp.tile(m_new, (1, s.shape[1] // LANES)))


def _sweep_blocks(nb0, near_back, n_sub, n_diag, tkw, tkn, do_block):
    first = jnp.maximum(nb0 - near_back, 0)
    n_wide = (first * tkn) // tkw

    def wide(j, carry):
        do_block(pl.multiple_of(j * tkw, tkw), tkw, None)
        return carry

    def narrow(j, carry):
        do_block(pl.multiple_of(j * tkn, tkn), tkn, None)
        return carry

    lax.fori_loop(0, n_wide, wide, 0)
    lax.fori_loop(n_wide * (tkw // tkn), first, narrow, 0)
    for c in range(-near_back, n_diag):
        kinds = []
        for r in range(n_sub):
            d = c - r
            kinds.append("skip" if d > 0 else None if d < -near_back else d + near_back)

        @pl.when(nb0 + c >= 0)
        def _special(c=c, kinds=kinds):
            do_block(pl.multiple_of((nb0 + c) * tkn, tkn), tkn, kinds)


def _mla_kernel(q_ref, k_ref, v_ref, o_ref, m_ref, acc_ref, *, hs, tqs, n_sub, n_diag, tkw, tkn, q_off, sk_real):
    qi = pl.program_id(2)
    qpos0 = q_off + qi * (tqs * n_sub)
    m_ref[...] = jnp.full(m_ref.shape, NEG, F32)
    acc_ref[...] = jnp.zeros(acc_ref.shape, F32)

    def block(start, width, kinds):
        for hh in range(hs):
            cols = slice(hh * LANES, (hh + 1) * LANES)
            k = k_ref[pl.ds(start, width), cols]
            v = v_ref[pl.ds(start, width), cols]
            for r in range(n_sub):
                kind = None if kinds is None else kinds[r]
                if kind == "skip":
                    continue
                s = _dot_nt(q_ref[r * tqs:(r + 1) * tqs, cols], k)
                if kind is not None:
                    qp = qpos0 + r * tqs + lax.broadcasted_iota(jnp.int32, (tqs, width), 0)
                    kp = start + lax.broadcasted_iota(jnp.int32, (tqs, width), 1)
                    ok = ((kp >> CHUNK_SHIFT) <= (qp >> CHUNK_SHIFT)) & (kp < sk_real)
                    s = jnp.where(ok, s, NEG)
                alpha, p = _softmax_block(s, m_ref.at[hh, r])
                acc_ref[hh, r] = acc_ref[hh, r] * alpha + _dot(p.astype(BF16), v)

    _sweep_blocks(qpos0 // tkn, 0, n_sub, n_diag, tkw, tkn, block)

    for hh in range(hs):
        for r in range(n_sub):
            acc = acc_ref[hh, r]
            lane = lax.broadcasted_iota(jnp.int32, acc.shape, 1)
            denom = jnp.sum(jnp.where(lane == MLA_V, acc, 0.0), axis=-1, keepdims=True)
            o_ref[r * tqs:(r + 1) * tqs, hh * LANES:(hh + 1) * LANES] = (
                jnp.where(lane < MLA_V, acc / denom, 0.0).astype(BF16))


def _mla_attn(qext, kext, vext, *, hs, tqs, n_sub, n_diag, tkw, tkn, q_off, sk_real):
    b, sq, _ = qext.shape
    skp = kext.shape[1]
    tq = tqs * n_sub
    q_spec, k_spec, v_spec = _attn_specs(tq, skp, width=hs * LANES, kv_buffers=1)
    kern = functools.partial(_mla_kernel, hs=hs, tqs=tqs, n_sub=n_sub, n_diag=n_diag, tkw=tkw, tkn=tkn,
                             q_off=q_off, sk_real=sk_real)
    return pl.pallas_call(
        kern,
        grid=(b, MLA_HEADS // hs, sq // tq),
        in_specs=[q_spec, k_spec, v_spec],
        out_specs=q_spec,
        out_shape=jax.ShapeDtypeStruct((b, sq, MLA_HEADS * LANES), BF16),
        scratch_shapes=[pltpu.VMEM((hs, n_sub, tqs, LANES), F32), pltpu.VMEM((hs, n_sub, tqs, LANES), F32)],
        compiler_params=_cparams(3),
        name="mla_attn",
    )(qext, kext, vext)


def _diff_kernel(t5_ref, q_ref, k_ref, v_ref, bkt_ref, lamv_ref, subln_ref, o_ref,
                 bias_ref, m_ref, acc_ref, *, tqs, n_sub, n_diag, tkw, tkn, q_off, n_near, near_back, far_bucket,
                 lam_init):
    first = (pl.program_id(0) == 0) & (pl.program_id(1) == 0) & (pl.program_id(2) == 0)

    @pl.when(first)
    def _build_bias():
        for r in range(n_near):
            bkt = bkt_ref[r]
            vals = [jnp.full((tqs, tkn), NEG, F32) for _ in range(DIFF_HEADS)]
            for t in range(T5_BUCKETS):
                hit = bkt == t
                for hh in range(DIFF_HEADS):
                    vals[hh] = jnp.where(hit, (t5_ref[t, hh] - t5_ref[far_bucket, hh]) * LOG2E, vals[hh])
            for hh in range(DIFF_HEADS):
                bias_ref[r, hh] = vals[hh]

    h = pl.program_id(1)
    qi = pl.program_id(2)
    qpos0 = q_off + qi * (tqs * n_sub)
    q = q_ref[...]
    lane = lax.broadcasted_iota(jnp.int32, q.shape, 1)
    qm = [_keep_lanes(q, lane < DIFF_QK), _keep_lanes(q, lane >= DIFF_QK)]
    m_ref[...] = jnp.full(m_ref.shape, NEG, F32)
    acc_ref[...] = jnp.zeros(acc_ref.shape, F32)

    def block(start, width, kinds):
        k = k_ref[pl.ds(start, width), :]
        v = v_ref[pl.ds(start, width), :]
        for r in range(n_sub):
            kind = None if kinds is None else kinds[r]
            if kind == "skip":
                continue
            for mi in range(2):
                s = _dot_nt(qm[mi][r * tqs:(r + 1) * tqs, :], k)
                if kind is not None:
                    s = s + bias_ref[kind, h]
                alpha, p = _softmax_block(s, m_ref.at[mi, r])
                acc_ref[mi, r] = acc_ref[mi, r] * jnp.tile(alpha, (1, 2)) + _dot(p.astype(BF16), v)

    _sweep_blocks(qpos0 // tkn, near_back, n_sub, n_diag, tkw, tkn, block)

    lv = lamv_ref[...]
    lam = (jnp.exp(jnp.sum(lv[0:1] * lv[1:2], axis=-1, keepdims=True))
           - jnp.exp(jnp.sum(lv[2:3] * lv[3:4], axis=-1, keepdims=True)) + lam_init)
    for r in range(n_sub):
        a0, a1 = acc_ref[0, r], acc_ref[1, r]
        o = a0[:, :DIFF_V] / a0[:, DIFF_V:] - lam * (a1[:, :DIFF_V] / a1[:, DIFF_V:])
        o_ref[r * tqs:(r + 1) * tqs, :] = (_rms(o, subln_ref[...]) * (1.0 - lam_init)).astype(BF16)


def _diff_attn(t5, dq, dk, dv, bkt, lamv, subln, *, tqs, n_sub, n_diag, tkw, tkn, q_off, near_back, far_bucket,
               lam_init):
    b, sq, _ = dq.shape
    skp = dk.shape[1]
    n_near = bkt.shape[0]
    tq = tqs * n_sub
    q_spec, k_spec, v_spec = _attn_specs(tq, skp, v_width=2 * DIFF_V)
    kern = functools.partial(_diff_kernel, tqs=tqs, n_sub=n_sub, n_diag=n_diag, tkw=tkw, tkn=tkn, q_off=q_off,
                             n_near=n_near, near_back=near_back, far_bucket=far_bucket, lam_init=lam_init)
    return pl.pallas_call(
        kern,
        grid=(b, DIFF_HEADS, sq // tq),
        in_specs=[pl.BlockSpec(memory_space=pltpu.SMEM), q_spec, k_spec, v_spec,
                  _full(bkt.shape), _full(lamv.shape), _full(subln.shape)],
        out_specs=q_spec,
        out_shape=jax.ShapeDtypeStruct((b, sq, DIFF_HEADS * DIFF_V), BF16),
        scratch_shapes=[pltpu.VMEM((n_near, DIFF_HEADS, tqs, tkn), F32), pltpu.VMEM((2, n_sub, tqs, LANES), F32),
                        pltpu.VMEM((2, n_sub, tqs, 2 * DIFF_V), F32)],
        compiler_params=_cparams(3),
        name="diff_attn",
    )(t5, dq, dk, dv, bkt, lamv, subln)


def _sb_kernel(q_ref, k_ref, v_ref, tri_ref, o_ref, run_ref, acc_ref, *, tq, tk, q_off, n_masked):
    qi = pl.program_id(2)
    qpos0 = q_off + qi * tq
    top = (qpos0 + tq - 1) // tk
    q = q_ref[...]
    lane = lax.broadcasted_iota(jnp.int32, q.shape, 1)
    tri = tri_ref[...]
    qhs = [_keep_lanes(q, lane < SB_DIM), _keep_lanes(q, lane >= SB_DIM)]

    def block(kj, masked):
        ks = pl.multiple_of(kj * tk, tk)
        k = k_ref[pl.ds(ks, tk), :]
        v = v_ref[pl.ds(ks, tk), :]
        for hh in range(2):
            z = _dot_nt(qhs[hh], k)
            t = jnp.log1p(jnp.exp(-jnp.abs(z)))
            log_1m = -(jnp.maximum(z, 0.0) + t)
            log_beta = jnp.minimum(z, 0.0) - t
            if masked:
                qp = qpos0 + lax.broadcasted_iota(jnp.int32, (tq, tk), 0)
                kp = kj * tk + lax.broadcasted_iota(jnp.int32, (tq, tk), 1)
                ok = kp < qp
                log_1m = jnp.where(ok, log_1m, 0.0)
            hi = log_1m.astype(BF16)
            lo = (log_1m - hi.astype(F32)).astype(BF16)
            after = _dot(hi, tri) + _dot(lo, tri) + run_ref[hh]
            a = jnp.exp(log_beta + after)
            if masked:
                a = jnp.where(ok, a, 0.0)
            acc_ref[hh] += _dot(a.astype(BF16), v)
            run_ref[hh] += jnp.sum(log_1m, axis=-1, keepdims=True)

    run_ref[...] = jnp.zeros(run_ref.shape, F32)
    acc_ref[...] = jnp.zeros(acc_ref.shape, F32)
    for r in range(n_masked):
        block(top - r, True)

    def cond(kj):
        return (kj >= 0) & (jnp.max(run_ref[...]) > SB_EXIT)

    def body(kj):
        block(kj, False)
        return kj - 1

    lax.while_loop(cond, body, top - n_masked)
    o_ref[...] = jnp.where(lane < SB_DIM, acc_ref[0], acc_ref[1]).astype(BF16)


def _sb_attn(sq_, sk_, sv_, tri, *, tq, tk, q_off, n_masked):
    b, sq, _ = sq_.shape
    skp = sk_.shape[1]
    q_spec, kv_spec, _ = _attn_specs(tq, skp)
    kern = functools.partial(_sb_kernel, tq=tq, tk=tk, q_off=q_off, n_masked=n_masked)
    return pl.pallas_call(
        kern,
        grid=(b, SB_HEADS // 2, sq // tq),
        in_specs=[q_spec, kv_spec, kv_spec, _full(tri.shape)],
        out_specs=q_spec,
        out_shape=jax.ShapeDtypeStruct((b, sq, SB_HEADS * SB_DIM), BF16),
        scratch_shapes=[pltpu.VMEM((2, tq, 1), F32), pltpu.VMEM((2, tq, LANES), F32)],
        compiler_params=_cparams(3),
        name="sb_attn",
    )(sq_, sk_, sv_, tri)


CA_ROW_GROUP = 16


def _ca_kernel(tab_ref, q_ref, k_ref, v_ref, o_ref, bias_ref, *, tq, win, win_real, q_off):
    first = (pl.program_id(0) == 0) & (pl.program_id(1) == 0) & (pl.program_id(2) == 0)
    n_shift = -(-(tq + win) // LANES) * LANES
    n_rel = n_shift + LANES

    @pl.when(first)
    def _build_bias():
        x = lax.broadcasted_iota(jnp.int32, (CA_HEADS, n_rel), 1)
        idx = jnp.clip(x - (tq - 1) - CA_BAND, -CA_MAX_REL, CA_MAX_REL) + CA_MAX_REL
        f = jnp.zeros((CA_HEADS, n_rel), F32)
        for t in range(2 * CA_MAX_REL + 1):
            f = jnp.where(idx == t, tab_ref[:, t:t + 1], f)
        g = CA_ROW_GROUP
        i_loc = lax.broadcasted_iota(jnp.int32, (g, win), 0)
        j_loc = lax.broadcasted_iota(jnp.int32, (g, win), 1)
        for hh in range(CA_HEADS):
            fh = f[hh:hh + 1, :]
            shifted = jnp.concatenate([fh[:, g - 1 - bb:g - 1 - bb + n_shift] for bb in range(g)], axis=0)
            for a in range(tq // g):
                start = tq - g * a - g
                tile = shifted[:, start:start + win]
                i = i_loc + g * a
                kc = (j_loc >> CHUNK_SHIFT) - CA_LEFT_CHUNKS
                qc = i >> CHUNK_SHIFT
                ok = (kc <= qc) & (kc >= qc - CA_LEFT_CHUNKS) & (j_loc < win_real)
                bias_ref[hh, g * a:g * a + g, :] = jnp.where(ok, tile, NEG)

    hp = pl.program_id(1)
    qi = pl.program_id(2)
    qpos0 = q_off + qi * tq
    ws = pl.multiple_of(qi * tq, tq)
    q = q_ref[...]
    k = k_ref[pl.ds(ws, win), :]
    v = v_ref[pl.ds(ws, win), :]
    lane = lax.broadcasted_iota(jnp.int32, q.shape, 1)
    kpos = qpos0 - CA_BAND + lax.broadcasted_iota(jnp.int32, (tq, win), 1)
    outs = []
    for hh in range(2):
        qh = _keep_lanes(q, (lane < CA_DIM) if hh == 0 else (lane >= CA_DIM))
        s = _dot_nt(qh, k) + bias_ref[2 * hp + hh]
        s = jnp.where(kpos >= 0, s, NEG)
        m = jnp.max(s, axis=-1, keepdims=True)
        p = jnp.exp(s - m)
        denom = jnp.sum(p, axis=-1, keepdims=True)
        outs.append(_dot(p.astype(BF16), v) / denom)
    o_ref[...] = jnp.where(lane < CA_DIM, outs[0], outs[1]).astype(BF16)


def _ca_attn(tab_t, cq, ck, cv, *, tq, win, win_real, q_off):
    b, sq, _ = cq.shape
    skp = ck.shape[1]
    q_spec, kv_spec, _ = _attn_specs(tq, skp)
    kern = functools.partial(_ca_kernel, tq=tq, win=win, win_real=win_real, q_off=q_off)
    return pl.pallas_call(
        kern,
        grid=(b, CA_HEADS // 2, sq // tq),
        in_specs=[_full(tab_t.shape), q_spec, kv_spec, kv_spec],
        out_specs=q_spec,
        out_shape=jax.ShapeDtypeStruct((b, sq, CA_HEADS * CA_DIM), BF16),
        scratch_shapes=[pltpu.VMEM((CA_HEADS, tq, win), F32)],
        compiler_params=_cparams(3),
        name="ca_attn",
    )(tab_t, cq, ck, cv)


def _post_kernel(x_ref, oa_ref, ob_ref, wa_ref, wb_ref, g_ref, w1_ref, w2_ref, gf_ref, y_ref,
                 x1_ref, hn_ref, acc_ref, *, final_norm):
    j = pl.program_id(1)

    @pl.when(j == 0)
    def _mix():
        x1 = x_ref[...] + _dot(oa_ref[...], wa_ref[...]) + _dot(ob_ref[...], wb_ref[...])
        x1_ref[...] = x1
        hn_ref[...] = _rms(x1, g_ref[...]).astype(BF16)
        acc_ref[...] = jnp.zeros(acc_ref.shape, F32)

    a = jnp.maximum(_dot(hn_ref[...], w1_ref[...]), 0.0)
    acc_ref[...] += _dot((a * a).astype(BF16), w2_ref[...])

    @pl.when(j == pl.num_programs(1) - 1)
    def _finish():
        y = x1_ref[...] + acc_ref[...]
        if final_norm:
            y = _rms(y, gf_ref[...])
        y_ref[...] = y


def _post(x, oa, ob, wa, wb, g, w1, w2, gf, *, tm, tf, final_norm):
    m = x.shape[0]
    d_ff = w1.shape[1]
    rows = lambda width: pl.BlockSpec((tm, width), lambda i, j: (i, 0))
    return pl.pallas_call(
        functools.partial(_post_kernel, final_norm=final_norm),
        grid=(m // tm, d_ff // tf),
        in_specs=[rows(D_MODEL), rows(oa.shape[1]), rows(ob.shape[1]), _full(wa.shape), _full(wb.shape),
                  _full(g.shape), pl.BlockSpec((D_MODEL, tf), lambda i, j: (0, j)),
                  pl.BlockSpec((tf, D_MODEL), lambda i, j: (j, 0)), _full(gf.shape)],
        out_specs=rows(D_MODEL),
        out_shape=jax.ShapeDtypeStruct((m, D_MODEL), F32),
        scratch_shapes=[pltpu.VMEM((tm, D_MODEL), F32), pltpu.VMEM((tm, D_MODEL), BF16),
                        pltpu.VMEM((tm, D_MODEL), F32)],
        compiler_params=_cparams(2),
        name="post_mlp",
    )(x, oa, ob, wa, wb, g, w1, w2, gf)


def _t5_bucket(rel):
    nb = T5_BUCKETS // 2
    max_exact = nb // 2
    ret = jnp.where(rel > 0, nb, 0)
    n = jnp.abs(rel)
    nf = jnp.maximum(n, 1).astype(F32)
    large = max_exact + (jnp.log(nf / max_exact) / math.log(T5_MAX_DIST / max_exact) * (nb - max_exact)).astype(jnp.int32)
    large = jnp.minimum(large, nb - 1)
    return ret + jnp.where(n < max_exact, n, large)


def _rope_tables(pos):
    half = MLA_ROPE // 2
    inv = ROPE_BASE ** (-jnp.arange(half, dtype=F32) / half)
    ang = pos.astype(F32)[:, None] * inv[None, :]
    cos, sin = jnp.cos(ang), jnp.sin(ang)
    cosk = jnp.concatenate([cos, cos], axis=1)
    sink = jnp.concatenate([-sin, sin], axis=1)
    n = pos.shape[0]
    pad = jnp.zeros((n, LANES - MLA_NOPE - MLA_ROPE), F32)
    qscale = (MLA_NOPE + MLA_ROPE) ** -0.5 * LOG2E
    cosq = jnp.concatenate([jnp.ones((n, MLA_NOPE), F32), cosk, pad], axis=1) * qscale
    sinq = jnp.concatenate([jnp.zeros((n, MLA_NOPE), F32), sink, pad], axis=1) * qscale
    return {"cosq": cosq, "sinq": sinq, "cosk": cosk, "sink": sink}


def _swap_halves(w):
    half = w.shape[-1] // 2
    return jnp.concatenate([w[..., half:], w[..., :half]], axis=-1)


def _even_weights(w_in, q_norm, kv_norm, w_uq, w_ukv, w_out):
    sizes = [MLA_Q_RANK, MLA_KV_RANK, MLA_ROPE, DIFF_HEADS * 2 * DIFF_QK, DIFF_HEADS * 2 * DIFF_QK,
             DIFF_HEADS * DIFF_V]
    offs = np.cumsum([0] + sizes)
    wcq, wckv, wkr, wdq, wdk, wdv = (w_in[:, offs[i]:offs[i + 1]].astype(BF16) for i in range(6))
    uq = w_uq.reshape(MLA_Q_RANK, MLA_HEADS, MLA_NOPE + MLA_ROPE)
    zq = jnp.zeros((MLA_Q_RANK, MLA_HEADS, LANES - MLA_NOPE - MLA_ROPE), F32)
    wq = jnp.concatenate([uq, zq], axis=-1)
    wqs = jnp.concatenate([jnp.zeros_like(uq[..., :MLA_NOPE]), _swap_halves(uq[..., MLA_NOPE:]), zq], axis=-1)
    ukv = w_ukv.reshape(MLA_KV_RANK, MLA_HEADS, MLA_NOPE + MLA_V)
    zk = jnp.zeros((MLA_KV_RANK, MLA_HEADS, LANES - MLA_NOPE), F32)
    wk = jnp.concatenate([ukv[..., :MLA_NOPE], zk], axis=-1)
    wv = jnp.concatenate([ukv[..., MLA_NOPE:], jnp.zeros((MLA_KV_RANK, MLA_HEADS, LANES - MLA_V), F32)], axis=-1)
    place = np.zeros((MLA_ROPE, MLA_HEADS, LANES), np.float32)
    ones = np.zeros((1, MLA_HEADS, LANES), np.float32)
    for hh in range(MLA_HEADS):
        place[np.arange(MLA_ROPE), hh, MLA_NOPE + np.arange(MLA_ROPE)] = 1.0
        ones[0, hh, MLA_V] = 1.0
    flat = lambda a: a.reshape(a.shape[0], MLA_HEADS * LANES)
    wo_mla = w_out[:MLA_HEADS * MLA_V].reshape(MLA_HEADS, MLA_V, D_MODEL)
    wo_mla = jnp.concatenate([wo_mla, jnp.zeros((MLA_HEADS, LANES - MLA_V, D_MODEL), F32)], axis=1)
    return {
        "wcq": wcq, "wckv": wckv, "wkr": wkr, "wkrs": _swap_halves(wkr), "wdq": wdq, "wdk": wdk, "wdv": wdv,
        "qn": q_norm.reshape(1, -1), "kvn": kv_norm.reshape(1, -1),
        "wq": flat(wq).astype(BF16), "wqs": flat(wqs).astype(BF16),
        "wk": flat(wk).astype(BF16), "wv": flat(wv).astype(BF16),
        "place": jnp.asarray(flat(place), BF16), "ones": jnp.asarray(flat(ones), F32),
        "wo_mla": wo_mla.reshape(MLA_HEADS * LANES, D_MODEL).astype(BF16),
        "wo_diff": w_out[MLA_HEADS * MLA_V:].astype(BF16),
    }


def _pad_rows(a, total, front=0):
    back = total - front - a.shape[1]
    return jnp.pad(a, ((0, 0), (front, back), (0, 0)))


def _round_up(n, mult):
    return -(-n // mult) * mult


def _diff_buckets(tq, tk, q_off, sk_real, skp):
    near_back = -((q_off - (T5_MAX_DIST - 1)) // tk - q_off // tk)
    last = (_round_up(q_off + tq, CHUNK) - 1) // tk
    last = min(last, skp // tk - 1)
    n_near = last - (q_off // tk - near_back) + 1
    i = np.arange(tq)[:, None]
    mats = []
    for r in range(n_near):
        kp = (q_off // tk - near_back + r) * tk + np.arange(tk)[None, :]
        qp = q_off + i
        ok = ((kp >> CHUNK_SHIFT) <= (qp >> CHUNK_SHIFT)) & (kp < sk_real)
        bkt = _t5_bucket(jnp.asarray(kp - qp, jnp.int32))
        mats.append(jnp.where(jnp.asarray(ok), bkt, -1))
    return jnp.stack(mats).astype(jnp.int32), near_back


def _trunk(x, q_off, caches, prm, cfg):
    b, sq, _ = x.shape
    m = b * sq
    tq, tk, tm, tf = cfg["tq"], cfg["tk"], cfg["tm"], cfg["tf"]
    sk_real = q_off + sq
    skp = _round_up(sk_real, tk)
    pos = q_off + jnp.arange(sq, dtype=jnp.int32)
    tabs = {k: jnp.tile(v, (b, 1)) for k, v in _rope_tables(pos).items()}
    x2 = x.reshape(m, D_MODEL)

    def with_past(past, new, dtype):
        new = new.reshape(b, sq, -1)
        if past is None:
            return new.astype(dtype)
        return jnp.concatenate([past.reshape(b, past.shape[1], -1).astype(dtype), new.astype(dtype)], axis=1)

    ew = prm["even"]
    qext, ckv, kr, dq, dk, dkb, dv, dvb = _even_proj(x2, prm["norm_mix"][0:1], ew, tabs, tm)
    past = (None,) * 4 if caches is None else tuple(c[0] for c in caches[:4])
    ckv_all = _pad_rows(with_past(past[0], ckv, F32), skp)
    kr_all = _pad_rows(with_past(past[1], kr, F32), skp)
    kext, vext = _kv_up(ckv_all.reshape(b * skp, -1), kr_all.reshape(b * skp, -1), ew, cfg["tm_kv"])
    kext = kext.reshape(b, skp, -1)
    vext = vext.reshape(b, skp, -1)
    tqs, n_sub = cfg["tqs"], cfg["n_sub"]
    assert n_sub == 1 or tqs == tk
    n_diag = (_round_up(q_off + tqs * n_sub, CHUNK) - 1) // tk - q_off // tk + 1
    tiles = dict(tqs=tqs, n_sub=n_sub, n_diag=n_diag, tkw=cfg["tkw"], tkn=tk, q_off=q_off)
    o_mla = _mla_attn(qext.reshape(b, sq, -1), kext, vext, hs=cfg["mla_hs"], sk_real=sk_real, **tiles)
    dk_all = _pad_rows(with_past(past[2], dkb, BF16), skp)
    past_dv = past[3]
    if past_dv is not None:
        past_dv = jnp.concatenate([past_dv.astype(BF16), jnp.ones(past_dv.shape, BF16)], axis=-1)
    dv_all = _pad_rows(with_past(past_dv, dvb, BF16), skp)
    bkt, near_back = _diff_buckets(tqs, tk, q_off, sk_real, skp)
    assert bkt.shape[0] == near_back + n_diag - (n_sub - 1)
    lam_init = 0.8 - 0.6 * math.exp(-0.3 * 0)
    o_diff = _diff_attn(prm["t5"], dq.reshape(b, sq, -1), dk_all, dv_all, bkt, prm["lam_vecs"], prm["subln"],
                        near_back=near_back, far_bucket=T5_BUCKETS // 2 - 1, lam_init=lam_init, **tiles)
    x2 = _post(x2, o_mla.reshape(m, -1), o_diff.reshape(m, -1), ew["wo_mla"], ew["wo_diff"],
               prm["norm_ff"][0:1], prm["w_ff1"][0], prm["w_ff2"][0], prm["final_norm"],
               tm=tm, tf=tf, final_norm=False)
    new_even = (ckv.reshape(1, b, sq, MLA_KV_RANK), kr.reshape(1, b, sq, MLA_ROPE),
                dk.reshape(1, b, sq, DIFF_HEADS, 2 * DIFF_QK), dv.reshape(1, b, sq, DIFF_HEADS, DIFF_V))

    sq_, sk_, skb, sv_, svb, cq, ck, ckb, cv, cvb = _odd_proj(x2, prm["norm_mix"][1:2], prm["w_in_odd"], tm)
    past = (None,) * 4 if caches is None else tuple(c[0] for c in caches[4:])
    sk_all = _pad_rows(with_past(past[0], skb, BF16), skp)
    sv_all = _pad_rows(with_past(past[1], svb, BF16), skp)
    n_masked = (q_off + tq - 1) // tk - q_off // tk + 1
    o_sb = _sb_attn(sq_.reshape(b, sq, -1), sk_all, sv_all, prm["tri"][tk], tq=tq, tk=tk, q_off=q_off,
                    n_masked=n_masked)
    win_real = tq + CA_BAND
    win = _round_up(win_real, LANES)
    if caches is None:
        ck_all = _pad_rows(ckb.reshape(b, sq, -1), sq + CA_BAND + win - win_real, front=CA_BAND)
        cv_all = _pad_rows(cvb.reshape(b, sq, -1), sq + CA_BAND + win - win_real, front=CA_BAND)
    else:
        ck_all = _pad_rows(with_past(past[2], ckb, BF16), win)
        cv_all = _pad_rows(with_past(past[3], cvb, BF16), win)
    o_ca = _ca_attn(prm["ca_tab_t"], cq.reshape(b, sq, -1), ck_all, cv_all, tq=tq, win=win, win_real=win_real,
                    q_off=q_off)
    x2 = _post(x2, o_sb.reshape(m, -1), o_ca.reshape(m, -1), prm["wo_sb"], prm["wo_ca"],
               prm["norm_ff"][1:2], prm["w_ff1"][1], prm["w_ff2"][1], prm["final_norm"],
               tm=tm, tf=tf, final_norm=True)

    heads = lambda a: a.reshape(b, sq, SB_HEADS, SB_DIM)
    if caches is None:
        nb = min(CA_BAND, sq)
        cak, cav = heads(ck)[:, sq - nb:], heads(cv)[:, sq - nb:]
    else:
        nb = past[2].shape[1]
        cak = jnp.concatenate([past[2], heads(ck)], axis=1)[:, sq:]
        cav = jnp.concatenate([past[3], heads(cv)], axis=1)[:, sq:]
        assert cak.shape[1] == nb
    new_odd = (heads(sk_)[None], heads(sv_)[None], cak[None], cav[None])
    return x2.reshape(b, sq, D_MODEL), new_even + new_odd


def _tri(tk):
    j = np.arange(tk)[:, None]
    s = np.arange(tk)[None, :]
    return jnp.asarray((j > s).astype(np.float32), BF16)


def kernel(x_prompt, x_sample, cache_mla_ckv, cache_mla_krope, cache_diff_k, cache_diff_v, cache_sb_k, cache_sb_v, cache_ca_k, cache_ca_v, norm_mix, norm_ff, w_in_even, mla_q_norm, mla_kv_norm, mla_w_uq, mla_w_ukv, diff_lambda_vecs, diff_subln, t5_bias, w_out_even, w_in_odd, ca_rel_bias, w_out_odd, w_ff1, w_ff2, final_norm):
    seq = x_prompt.shape[1]
    dec_seq = x_sample.shape[1]
    past_len = cache_mla_ckv.shape[2]
    assert cache_ca_k.shape[2] == CA_BAND and past_len % CHUNK == 0

    cfg_p = {"tq": 256, "tqs": 256, "n_sub": 4, "tk": 256, "tkw": 1024, "mla_hs": 2, "tm": min(512, seq),
             "tm_kv": min(512, seq), "tf": 512}
    cfg_s = {"tq": dec_seq, "tqs": dec_seq, "n_sub": 1, "tk": 128, "tkw": 512, "mla_hs": 2,
             "tm": x_sample.shape[0] * dec_seq, "tm_kv": 128, "tf": 512}
    n_sb = SB_HEADS * SB_DIM
    prm = {
        "norm_mix": norm_mix, "norm_ff": norm_ff, "final_norm": final_norm.reshape(1, -1),
        "even": _even_weights(w_in_even[0], mla_q_norm[0], mla_kv_norm[0], mla_w_uq[0], mla_w_ukv[0],
                              w_out_even[0]),
        "t5": t5_bias, "lam_vecs": diff_lambda_vecs[0], "subln": diff_subln[0].reshape(1, -1),
        "w_in_odd": w_in_odd[0].astype(BF16), "ca_tab_t": ca_rel_bias[0].T,
        "wo_sb": w_out_odd[0][:n_sb].astype(BF16), "wo_ca": w_out_odd[0][n_sb:].astype(BF16),
        "w_ff1": w_ff1.astype(BF16), "w_ff2": w_ff2.astype(BF16),
        "tri": {tk: _tri(tk) for tk in {cfg_p["tk"], cfg_s["tk"]}},
    }
    y_prompt, new_p = _trunk(x_prompt, 0, None, prm, cfg_p)
    caches = (cache_mla_ckv, cache_mla_krope, cache_diff_k, cache_diff_v,
              cache_sb_k, cache_sb_v, cache_ca_k, cache_ca_v)
    y_sample, new_s = _trunk(x_sample, past_len, caches, prm, cfg_s)
    return (y_prompt, y_sample) + tuple(new_p) + tuple(new_s)
```

```python
import functools
import math

import numpy as np
import jax
import jax.numpy as jnp
from jax import lax
from jax.experimental import pallas as pl
from jax.experimental.pallas import tpu as pltpu

F32 = jnp.float32
BF16 = jnp.bfloat16

D_MODEL = 1024
CHUNK = 64
CHUNK_SHIFT = 6
EPS = 1e-6
NEG = -1e30

MLA_HEADS = 8
MLA_Q_RANK = 256
MLA_KV_RANK = 128
MLA_NOPE = 64
MLA_ROPE = 32
MLA_V = 64
ROPE_BASE = 10000.0
DIFF_HEADS = 4
DIFF_QK = 64
DIFF_V = 2 * DIFF_QK
T5_BUCKETS = 32
T5_MAX_DIST = 128
SB_HEADS = 8
SB_DIM = 64
CA_HEADS = 8
CA_DIM = 64
CA_LEFT_CHUNKS = 8
CA_BAND = CA_LEFT_CHUNKS * CHUNK
CA_MAX_REL = 128

LANES = 128
VMEM_LIMIT = 48 * 1024 * 1024
LOG2E = math.log2(math.e)
SB_EXIT = -104.0


def _cparams(n_axes):
    return pltpu.CompilerParams(dimension_semantics=("arbitrary",) * n_axes,
                                vmem_limit_bytes=VMEM_LIMIT)


def _rms(x, g):
    return x * lax.rsqrt(jnp.mean(x * x, axis=-1, keepdims=True) + EPS) * g


def _dot(a, b):
    return jnp.dot(a, b, preferred_element_type=F32)


def _dot_nt(a, b):
    return lax.dot_general(a, b, (((1,), (1,)), ((), ())), preferred_element_type=F32)


def _keep_lanes(q, keep):
    return jnp.where(keep, q.astype(F32), 0.0).astype(BF16)


def _full(shape):
    n = len(shape)
    return pl.BlockSpec(shape, lambda *_: (0,) * n)


def _rows(tm, width):
    return pl.BlockSpec((tm, width), lambda i: (i, 0))


def _even_proj_kernel(x_ref, g_ref, wcq_ref, wckv_ref, wkr_ref, wkrs_ref, wdq_ref, wdk_ref, wdv_ref,
                      qn_ref, kvn_ref, wq_ref, wqs_ref, cosq_ref, sinq_ref, cosk_ref, sink_ref,
                      qext_ref, ckv_ref, kr_ref, dq_ref, dk_ref, dkb_ref, dv_ref, dvb_ref):
    hn = _rms(x_ref[...], g_ref[...]).astype(BF16)
    cq = _rms(_dot(hn, wcq_ref[...]), qn_ref[...]).astype(BF16)
    cosq = jnp.concatenate([cosq_ref[...]] * MLA_HEADS, axis=1)
    sinq = jnp.concatenate([sinq_ref[...]] * MLA_HEADS, axis=1)
    qext_ref[...] = (_dot(cq, wq_ref[...]) * cosq + _dot(cq, wqs_ref[...]) * sinq).astype(BF16)
    ckv_ref[...] = _rms(_dot(hn, wckv_ref[...]), kvn_ref[...])
    kr_ref[...] = _dot(hn, wkr_ref[...]) * cosk_ref[...] + _dot(hn, wkrs_ref[...]) * sink_ref[...]
    dq_ref[...] = (_dot(hn, wdq_ref[...]) * (DIFF_QK ** -0.5 * LOG2E)).astype(BF16)
    dk = _dot(hn, wdk_ref[...])
    dk_ref[...] = dk
    dkb_ref[...] = dk.astype(BF16)
    dv = _dot(hn, wdv_ref[...])
    dv_ref[...] = dv
    dvb = dv.astype(BF16)
    ones = jnp.ones((dvb.shape[0], DIFF_V), BF16)
    dvb_ref[...] = jnp.concatenate(
        [piece for hh in range(DIFF_HEADS) for piece in (dvb[:, hh * DIFF_V:(hh + 1) * DIFF_V], ones)], axis=1)


def _even_proj(x, g, w, tabs, tm):
    m = x.shape[0]
    ins = [x, g, w["wcq"], w["wckv"], w["wkr"], w["wkrs"], w["wdq"], w["wdk"], w["wdv"],
           w["qn"], w["kvn"], w["wq"], w["wqs"], tabs["cosq"], tabs["sinq"], tabs["cosk"], tabs["sink"]]
    row_in = {0: D_MODEL, 13: LANES, 14: LANES, 15: MLA_ROPE, 16: MLA_ROPE}
    in_specs = [_rows(tm, row_in[i]) if i in row_in else _full(a.shape) for i, a in enumerate(ins)]
    outs = [(MLA_HEADS * LANES, BF16), (MLA_KV_RANK, F32), (MLA_ROPE, F32),
            (DIFF_HEADS * DIFF_V, BF16), (DIFF_HEADS * DIFF_V, F32), (DIFF_HEADS * DIFF_V, BF16),
            (DIFF_HEADS * DIFF_V, F32), (DIFF_HEADS * 2 * DIFF_V, BF16)]
    return pl.pallas_call(
        _even_proj_kernel,
        grid=(m // tm,),
        in_specs=in_specs,
        out_specs=[_rows(tm, n) for n, _ in outs],
        out_shape=[jax.ShapeDtypeStruct((m, n), dt) for n, dt in outs],
        compiler_params=_cparams(1),
        name="even_proj",
    )(*ins)


def _odd_proj_kernel(x_ref, g_ref, w_ref, sq_ref, sk_ref, skb_ref, sv_ref, svb_ref,
                     cq_ref, ck_ref, ckb_ref, cv_ref, cvb_ref):
    hn = _rms(x_ref[...], g_ref[...]).astype(BF16)
    width = SB_HEADS * SB_DIM

    def seg(i):
        return _dot(hn, w_ref[:, i * width:(i + 1) * width])

    sq_ref[...] = (seg(0) * (SB_DIM ** -0.5)).astype(BF16)
    for i, (f_ref, b_ref) in ((1, (sk_ref, skb_ref)), (2, (sv_ref, svb_ref)),
                              (4, (ck_ref, ckb_ref)), (5, (cv_ref, cvb_ref))):
        y = seg(i)
        f_ref[...] = y
        b_ref[...] = y.astype(BF16)
    cq_ref[...] = (seg(3) * (CA_DIM ** -0.5)).astype(BF16)


def _odd_proj(x, g, w, tm):
    m = x.shape[0]
    width = SB_HEADS * SB_DIM
    dts = [BF16, F32, BF16, F32, BF16, BF16, F32, BF16, F32, BF16]
    return pl.pallas_call(
        _odd_proj_kernel,
        grid=(m // tm,),
        in_specs=[_rows(tm, D_MODEL), _full(g.shape), _full(w.shape)],
        out_specs=[_rows(tm, width) for _ in dts],
        out_shape=[jax.ShapeDtypeStruct((m, width), dt) for dt in dts],
        compiler_params=_cparams(1),
        name="odd_proj",
    )(x, g, w)


def _kv_up_kernel(ckv_ref, kr_ref, wk_ref, wv_ref, place_ref, ones_ref, kext_ref, vext_ref):
    c = ckv_ref[...].astype(BF16)
    r = kr_ref[...].astype(BF16)
    kext_ref[...] = (_dot(c, wk_ref[...]) + _dot(r, place_ref[...])).astype(BF16)
    vext_ref[...] = (_dot(c, wv_ref[...]) + ones_ref[...]).astype(BF16)


def _kv_up(ckv, kr, w, tm):
    m = ckv.shape[0]
    width = MLA_HEADS * LANES
    return pl.pallas_call(
        _kv_up_kernel,
        grid=(m // tm,),
        in_specs=[_rows(tm, MLA_KV_RANK), _rows(tm, MLA_ROPE), _full(w["wk"].shape), _full(w["wv"].shape),
                  _full(w["place"].shape), _full(w["ones"].shape)],
        out_specs=[_rows(tm, width)] * 2,
        out_shape=[jax.ShapeDtypeStruct((m, width), BF16)] * 2,
        compiler_params=_cparams(1),
        name="mla_kv_up",
    )(ckv, kr, w["wk"], w["wv"], w["place"], w["ones"])


def _attn_specs(tq, skp, width=LANES, v_width=None, kv_buffers=None):
    kw = {} if kv_buffers is None else {"pipeline_mode": pl.Buffered(kv_buffers)}
    q_spec = pl.BlockSpec((None, tq, width), lambda b, h, qi: (b, qi, h))
    k_spec = pl.BlockSpec((None, skp, width), lambda b, h, qi: (b, 0, h), **kw)
    v_spec = pl.BlockSpec((None, skp, v_width or width), lambda b, h, qi: (b, 0, h), **kw)
    return q_spec, k_spec, v_spec


def _softmax_block(s, m_ref):
    m_old = m_ref[...]
    m_new = jnp.maximum(m_old, jnp.max(s, axis=-1, keepdims=True))
    m_ref[...] = m_new
    return jnp.exp2(m_old - m_new), jnp.exp2(s - jnp.tile(m_new, (1, s.shape[1] // LANES)))


def _sweep_blocks(nb0, near_back, n_sub, n_diag, tkw, tkn, do_block):
    first = jnp.maximum(nb0 - near_back, 0)
    n_wide = (first * tkn) // tkw

    def wide(j, carry):
        do_block(pl.multiple_of(j * tkw, tkw), tkw, None)
        return carry

    def narrow(j, carry):
        do_block(pl.multiple_of(j * tkn, tkn), tkn, None)
        return carry

    lax.fori_loop(0, n_wide, wide, 0)
    lax.fori_loop(n_wide * (tkw // tkn), first, narrow, 0)
    for c in range(-near_back, n_diag):
        kinds = []
        for r in range(n_sub):
            d = c - r
            kinds.append("skip" if d > 0 else None if d < -near_back else d + near_back)

        @pl.when(nb0 + c >= 0)
        def _special(c=c, kinds=kinds):
            do_block(pl.multiple_of((nb0 + c) * tkn, tkn), tkn, kinds)


def _mla_kernel(q_ref, k_ref, v_ref, o_ref, m_ref, acc_ref, *, hs, tqs, n_sub, n_diag, tkw, tkn, q_off, sk_real):
    qi = pl.program_id(2)
    qpos0 = q_off + qi * (tqs * n_sub)
    m_ref[...] = jnp.full(m_ref.shape, NEG, F32)
    acc_ref[...] = jnp.zeros(acc_ref.shape, F32)

    def block(start, width, kinds):
        for hh in range(hs):
            cols = slice(hh * LANES, (hh + 1) * LANES)
            k = k_ref[pl.ds(start, width), cols]
            v = v_ref[pl.ds(start, width), cols]
            for r in range(n_sub):
                kind = None if kinds is None else kinds[r]
                if kind == "skip":
                    continue
                s = _dot_nt(q_ref[r * tqs:(r + 1) * tqs, cols], k)
                if kind is not None:
                    qp = qpos0 + r * tqs + lax.broadcasted_iota(jnp.int32, (tqs, width), 0)
                    kp = start + lax.broadcasted_iota(jnp.int32, (tqs, width), 1)
                    ok = ((kp >> CHUNK_SHIFT) <= (qp >> CHUNK_SHIFT)) & (kp < sk_real)
                    s = jnp.where(ok, s, NEG)
                alpha, p = _softmax_block(s, m_ref.at[hh, r])
                acc_ref[hh, r] = acc_ref[hh, r] * alpha + _dot(p.astype(BF16), v)

    _sweep_blocks(qpos0 // tkn, 0, n_sub, n_diag, tkw, tkn, block)

    for hh in range(hs):
        for r in range(n_sub):
            acc = acc_ref[hh, r]
            lane = lax.broadcasted_iota(jnp.int32, acc.shape, 1)
            denom = jnp.sum(jnp.where(lane == MLA_V, acc, 0.0), axis=-1, keepdims=True)
            o_ref[r * tqs:(r + 1) * tqs, hh * LANES:(hh + 1) * LANES] = (
                jnp.where(lane < MLA_V, acc / denom, 0.0).astype(BF16))


def _mla_attn(qext, kext, vext, *, hs, kv_buffers, tqs, n_sub, n_diag, tkw, tkn, q_off, sk_real):
    b, sq, _ = qext.shape
    skp = kext.shape[1]
    tq = tqs * n_sub
    q_spec, k_spec, v_spec = _attn_specs(tq, skp, width=hs * LANES, kv_buffers=kv_buffers)
    kern = functools.partial(_mla_kernel, hs=hs, tqs=tqs, n_sub=n_sub, n_diag=n_diag, tkw=tkw, tkn=tkn,
                             q_off=q_off, sk_real=sk_real)
    return pl.pallas_call(
        kern,
        grid=(b, MLA_HEADS // hs, sq // tq),
        in_specs=[q_spec, k_spec, v_spec],
        out_specs=q_spec,
        out_shape=jax.ShapeDtypeStruct((b, sq, MLA_HEADS * LANES), BF16),
        scratch_shapes=[pltpu.VMEM((hs, n_sub, tqs, LANES), F32), pltpu.VMEM((hs, n_sub, tqs, LANES), F32)],
        compiler_params=_cparams(3),
        name="mla_attn",
    )(qext, kext, vext)


def _diff_kernel(t5_ref, q_ref, k_ref, v_ref, bkt_ref, lamv_ref, subln_ref, o_ref,
                 bias_ref, m_ref, acc_ref, *, tqs, n_sub, n_diag, tkw, tkn, q_off, n_near, near_back, far_bucket,
                 lam_init):
    first = (pl.program_id(0) == 0) & (pl.program_id(1) == 0) & (pl.program_id(2) == 0)

    @pl.when(first)
    def _build_bias():
        for r in range(n_near):
            bkt = bkt_ref[r]
            vals = [jnp.full((tqs, tkn), NEG, F32) for _ in range(DIFF_HEADS)]
            for t in range(T5_BUCKETS):
                hit = bkt == t
                for hh in range(DIFF_HEADS):
                    vals[hh] = jnp.where(hit, (t5_ref[t, hh] - t5_ref[far_bucket, hh]) * LOG2E, vals[hh])
            for hh in range(DIFF_HEADS):
                bias_ref[r, hh] = vals[hh]

    h = pl.program_id(1)
    qi = pl.program_id(2)
    qpos0 = q_off + qi * (tqs * n_sub)
    q = q_ref[...]
    lane = lax.broadcasted_iota(jnp.int32, q.shape, 1)
    qm = [_keep_lanes(q, lane < DIFF_QK), _keep_lanes(q, lane >= DIFF_QK)]
    m_ref[...] = jnp.full(m_ref.shape, NEG, F32)
    acc_ref[...] = jnp.zeros(acc_ref.shape, F32)

    def block(start, width, kinds):
        k = k_ref[pl.ds(start, width), :]
        v = v_ref[pl.ds(start, width), :]
        for r in range(n_sub):
            kind = None if kinds is None else kinds[r]
            if kind == "skip":
                continue
            for mi in range(2):
                s = _dot_nt(qm[mi][r * tqs:(r + 1) * tqs, :], k)
                if kind is not None:
                    s = s + bias_ref[kind, h]
                alpha, p = _softmax_block(s, m_ref.at[mi, r])
                acc_ref[mi, r] = acc_ref[mi, r] * jnp.tile(alpha, (1, 2)) + _dot(p.astype(BF16), v)

    _sweep_blocks(qpos0 // tkn, near_back, n_sub, n_diag, tkw, tkn, block)

    lv = lamv_ref[...]
    lam = (jnp.exp(jnp.sum(lv[0:1] * lv[1:2], axis=-1, keepdims=True))
           - jnp.exp(jnp.sum(lv[2:3] * lv[3:4], axis=-1, keepdims=True)) + lam_init)
    for r in range(n_sub):
        a0, a1 = acc_ref[0, r], acc_ref[1, r]
        o = a0[:, :DIFF_V] / a0[:, DIFF_V:] - lam * (a1[:, :DIFF_V] / a1[:, DIFF_V:])
        o_ref[r * tqs:(r + 1) * tqs, :] = (_rms(o, subln_ref[...]) * (1.0 - lam_init)).astype(BF16)


def _diff_attn(t5, dq, dk, dv, bkt, lamv, subln, *, tqs, n_sub, n_diag, tkw, tkn, q_off, near_back, far_bucket,
               lam_init):
    b, sq, _ = dq.shape
    skp = dk.shape[1]
    n_near = bkt.shape[0]
    tq = tqs * n_sub
    q_spec, k_spec, v_spec = _attn_specs(tq, skp, v_width=2 * DIFF_V)
    kern = functools.partial(_diff_kernel, tqs=tqs, n_sub=n_sub, n_diag=n_diag, tkw=tkw, tkn=tkn, q_off=q_off,
                             n_near=n_near, near_back=near_back, far_bucket=far_bucket, lam_init=lam_init)
    return pl.pallas_call(
        kern,
        grid=(b, DIFF_HEADS, sq // tq),
        in_specs=[pl.BlockSpec(memory_space=pltpu.SMEM), q_spec, k_spec, v_spec,
                  _full(bkt.shape), _full(lamv.shape), _full(subln.shape)],
        out_specs=q_spec,
        out_shape=jax.ShapeDtypeStruct((b, sq, DIFF_HEADS * DIFF_V), BF16),
        scratch_shapes=[pltpu.VMEM((n_near, DIFF_HEADS, tqs, tkn), F32), pltpu.VMEM((2, n_sub, tqs, LANES), F32),
                        pltpu.VMEM((2, n_sub, tqs, 2 * DIFF_V), F32)],
        compiler_params=_cparams(3),
        name="diff_attn",
    )(t5, dq, dk, dv, bkt, lamv, subln)


def _sb_kernel(q_ref, k_ref, v_ref, tri_ref, o_ref, run_ref, acc_ref, *, hp, tq, tk, q_off, n_masked):
    qi = pl.program_id(2)
    qpos0 = q_off + qi * tq
    top = (qpos0 + tq - 1) // tk
    lane = lax.broadcasted_iota(jnp.int32, (tq, LANES), 1)
    tri = tri_ref[...]
    qhs = []
    for pp in range(hp):
        qpair = q_ref[:, pp * LANES:(pp + 1) * LANES]
        qhs.append([_keep_lanes(qpair, lane < SB_DIM), _keep_lanes(qpair, lane >= SB_DIM)])

    def block(kj, masked):
        ks = pl.multiple_of(kj * tk, tk)
        if masked:
            qp = qpos0 + lax.broadcasted_iota(jnp.int32, (tq, tk), 0)
            kp = kj * tk + lax.broadcasted_iota(jnp.int32, (tq, tk), 1)
            ok = kp < qp
        for pp in range(hp):
            cols = slice(pp * LANES, (pp + 1) * LANES)
            k = k_ref[pl.ds(ks, tk), cols]
            v = v_ref[pl.ds(ks, tk), cols]
            for hh in range(2):
                z = _dot_nt(qhs[pp][hh], k)
                t = jnp.log1p(jnp.exp(-jnp.abs(z)))
                log_1m = -(jnp.maximum(z, 0.0) + t)
                log_beta = jnp.minimum(z, 0.0) - t
                if masked:
                    log_1m = jnp.where(ok, log_1m, 0.0)
                hi = log_1m.astype(BF16)
                lo = (log_1m - hi.astype(F32)).astype(BF16)
                after = _dot(hi, tri) + _dot(lo, tri) + run_ref[pp, hh]
                a = jnp.exp(log_beta + after)
                if masked:
                    a = jnp.where(ok, a, 0.0)
                acc_ref[pp, hh] += _dot(a.astype(BF16), v)
                run_ref[pp, hh] += jnp.sum(log_1m, axis=-1, keepdims=True)

    run_ref[...] = jnp.zeros(run_ref.shape, F32)
    acc_ref[...] = jnp.zeros(acc_ref.shape, F32)
    for r in range(n_masked):
        block(top - r, True)

    def cond(kj):
        return (kj >= 0) & (jnp.max(run_ref[...]) > SB_EXIT)

    def body(kj):
        block(kj, False)
        return kj - 1

    lax.while_loop(cond, body, top - n_masked)
    for pp in range(hp):
        o_ref[:, pp * LANES:(pp + 1) * LANES] = jnp.where(lane < SB_DIM, acc_ref[pp, 0], acc_ref[pp, 1]).astype(BF16)


def _sb_attn(sq_, sk_, sv_, tri, *, hp, kv_buffers, tq, tk, q_off, n_masked):
    b, sq, _ = sq_.shape
    skp = sk_.shape[1]
    q_spec, kv_spec, _ = _attn_specs(tq, skp, width=hp * LANES, kv_buffers=kv_buffers)
    kern = functools.partial(_sb_kernel, hp=hp, tq=tq, tk=tk, q_off=q_off, n_masked=n_masked)
    return pl.pallas_call(
        kern,
        grid=(b, SB_HEADS // (2 * hp), sq // tq),
        in_specs=[q_spec, kv_spec, kv_spec, _full(tri.shape)],
        out_specs=q_spec,
        out_shape=jax.ShapeDtypeStruct((b, sq, SB_HEADS * SB_DIM), BF16),
        scratch_shapes=[pltpu.VMEM((hp, 2, tq, 1), F32), pltpu.VMEM((hp, 2, tq, LANES), F32)],
        compiler_params=_cparams(3),
        name="sb_attn",
    )(sq_, sk_, sv_, tri)


CA_ROW_GROUP = 16


def _ca_kernel(tab_ref, q_ref, k_ref, v_ref, o_ref, bias_ref, *, hp, tq, win, win_real, q_off):
    first = (pl.program_id(0) == 0) & (pl.program_id(1) == 0) & (pl.program_id(2) == 0)
    n_shift = -(-(tq + win) // LANES) * LANES
    n_rel = n_shift + LANES

    @pl.when(first)
    def _build_bias():
        x = lax.broadcasted_iota(jnp.int32, (CA_HEADS, n_rel), 1)
        idx = jnp.clip(x - (tq - 1) - CA_BAND, -CA_MAX_REL, CA_MAX_REL) + CA_MAX_REL
        f = jnp.zeros((CA_HEADS, n_rel), F32)
        for t in range(2 * CA_MAX_REL + 1):
            f = jnp.where(idx == t, tab_ref[:, t:t + 1], f)
        g = CA_ROW_GROUP
        i_loc = lax.broadcasted_iota(jnp.int32, (g, win), 0)
        j_loc = lax.broadcasted_iota(jnp.int32, (g, win), 1)
        for hh in range(CA_HEADS):
            fh = f[hh:hh + 1, :]
            shifted = jnp.concatenate([fh[:, g - 1 - bb:g - 1 - bb + n_shift] for bb in range(g)], axis=0)
            for a in range(tq // g):
                start = tq - g * a - g
                tile = shifted[:, start:start + win]
                i = i_loc + g * a
                kc = (j_loc >> CHUNK_SHIFT) - CA_LEFT_CHUNKS
                qc = i >> CHUNK_SHIFT
                ok = (kc <= qc) & (kc >= qc - CA_LEFT_CHUNKS) & (j_loc < win_real)
                bias_ref[hh, g * a:g * a + g, :] = jnp.where(ok, tile, NEG)

    group = pl.program_id(1)
    qi = pl.program_id(2)
    qpos0 = q_off + qi * tq
    ws = pl.multiple_of(qi * tq, tq)
    lane = lax.broadcasted_iota(jnp.int32, (tq, LANES), 1)
    kpos = qpos0 - CA_BAND + lax.broadcasted_iota(jnp.int32, (tq, win), 1)
    for pp in range(hp):
        cols = slice(pp * LANES, (pp + 1) * LANES)
        q = q_ref[:, cols]
        k = k_ref[pl.ds(ws, win), cols]
        v = v_ref[pl.ds(ws, win), cols]
        outs = []
        for hh in range(2):
            qh = _keep_lanes(q, (lane < CA_DIM) if hh == 0 else (lane >= CA_DIM))
            s = _dot_nt(qh, k) + bias_ref[2 * (group * hp + pp) + hh]
            s = jnp.where(kpos >= 0, s, NEG)
            m = jnp.max(s, axis=-1, keepdims=True)
            p = jnp.exp(s - m)
            denom = jnp.sum(p, axis=-1, keepdims=True)
            outs.append(_dot(p.astype(BF16), v) / denom)
        o_ref[:, cols] = jnp.where(lane < CA_DIM, outs[0], outs[1]).astype(BF16)


def _ca_attn(tab_t, cq, ck, cv, *, hp, kv_buffers, tq, win, win_real, q_off):
    b, sq, _ = cq.shape
    skp = ck.shape[1]
    q_spec, kv_spec, _ = _attn_specs(tq, skp, width=hp * LANES, kv_buffers=kv_buffers)
    kern = functools.partial(_ca_kernel, hp=hp, tq=tq, win=win, win_real=win_real, q_off=q_off)
    return pl.pallas_call(
        kern,
        grid=(b, CA_HEADS // (2 * hp), sq // tq),
        in_specs=[_full(tab_t.shape), q_spec, kv_spec, kv_spec],
        out_specs=q_spec,
        out_shape=jax.ShapeDtypeStruct((b, sq, CA_HEADS * CA_DIM), BF16),
        scratch_shapes=[pltpu.VMEM((CA_HEADS, tq, win), F32)],
        compiler_params=_cparams(3),
        name="ca_attn",
    )(tab_t, cq, ck, cv)


def _post_kernel(x_ref, oa_ref, ob_ref, wa_ref, wb_ref, g_ref, w1_ref, w2_ref, gf_ref, y_ref,
                 x1_ref, hn_ref, acc_ref, *, final_norm):
    j = pl.program_id(1)

    @pl.when(j == 0)
    def _mix():
        x1 = x_ref[...] + _dot(oa_ref[...], wa_ref[...]) + _dot(ob_ref[...], wb_ref[...])
        x1_ref[...] = x1
        hn_ref[...] = _rms(x1, g_ref[...]).astype(BF16)
        acc_ref[...] = jnp.zeros(acc_ref.shape, F32)

    a = jnp.maximum(_dot(hn_ref[...], w1_ref[...]), 0.0)
    acc_ref[...] += _dot((a * a).astype(BF16), w2_ref[...])

    @pl.when(j == pl.num_programs(1) - 1)
    def _finish():
        y = x1_ref[...] + acc_ref[...]
        if final_norm:
            y = _rms(y, gf_ref[...])
        y_ref[...] = y


def _post(x, oa, ob, wa, wb, g, w1, w2, gf, *, tm, tf, final_norm):
    m = x.shape[0]
    d_ff = w1.shape[1]
    rows = lambda width: pl.BlockSpec((tm, width), lambda i, j: (i, 0))
    return pl.pallas_call(
        functools.partial(_post_kernel, final_norm=final_norm),
        grid=(m // tm, d_ff // tf),
        in_specs=[rows(D_MODEL), rows(oa.shape[1]), rows(ob.shape[1]), _full(wa.shape), _full(wb.shape),
                  _full(g.shape), pl.BlockSpec((D_MODEL, tf), lambda i, j: (0, j)),
                  pl.BlockSpec((tf, D_MODEL), lambda i, j: (j, 0)), _full(gf.shape)],
        out_specs=rows(D_MODEL),
        out_shape=jax.ShapeDtypeStruct((m, D_MODEL), F32),
        scratch_shapes=[pltpu.VMEM((tm, D_MODEL), F32), pltpu.VMEM((tm, D_MODEL), BF16),
                        pltpu.VMEM((tm, D_MODEL), F32)],
        compiler_params=_cparams(2),
        name="post_mlp",
    )(x, oa, ob, wa, wb, g, w1, w2, gf)


def _t5_bucket(rel):
    nb = T5_BUCKETS // 2
    max_exact = nb // 2
    ret = jnp.where(rel > 0, nb, 0)
    n = jnp.abs(rel)
    nf = jnp.maximum(n, 1).astype(F32)
    large = max_exact + (jnp.log(nf / max_exact) / math.log(T5_MAX_DIST / max_exact) * (nb - max_exact)).astype(jnp.int32)
    large = jnp.minimum(large, nb - 1)
    return ret + jnp.where(n < max_exact, n, large)


def _rope_tables(pos):
    half = MLA_ROPE // 2
    inv = ROPE_BASE ** (-jnp.arange(half, dtype=F32) / half)
    ang = pos.astype(F32)[:, None] * inv[None, :]
    cos, sin = jnp.cos(ang), jnp.sin(ang)
    cosk = jnp.concatenate([cos, cos], axis=1)
    sink = jnp.concatenate([-sin, sin], axis=1)
    n = pos.shape[0]
    pad = jnp.zeros((n, LANES - MLA_NOPE - MLA_ROPE), F32)
    qscale = (MLA_NOPE + MLA_ROPE) ** -0.5 * LOG2E
    cosq = jnp.concatenate([jnp.ones((n, MLA_NOPE), F32), cosk, pad], axis=1) * qscale
    sinq = jnp.concatenate([jnp.zeros((n, MLA_NOPE), F32), sink, pad], axis=1) * qscale
    return {"cosq": cosq, "sinq": sinq, "cosk": cosk, "sink": sink}


def _swap_halves(w):
    half = w.shape[-1] // 2
    return jnp.concatenate([w[..., half:], w[..., :half]], axis=-1)


def _even_weights(w_in, q_norm, kv_norm, w_uq, w_ukv, w_out):
    sizes = [MLA_Q_RANK, MLA_KV_RANK, MLA_ROPE, DIFF_HEADS * 2 * DIFF_QK, DIFF_HEADS * 2 * DIFF_QK,
             DIFF_HEADS * DIFF_V]
    offs = np.cumsum([0] + sizes)
    wcq, wckv, wkr, wdq, wdk, wdv = (w_in[:, offs[i]:offs[i + 1]].astype(BF16) for i in range(6))
    uq = w_uq.reshape(MLA_Q_RANK, MLA_HEADS, MLA_NOPE + MLA_ROPE)
    zq = jnp.zeros((MLA_Q_RANK, MLA_HEADS, LANES - MLA_NOPE - MLA_ROPE), F32)
    wq = jnp.concatenate([uq, zq], axis=-1)
    wqs = jnp.concatenate([jnp.zeros_like(uq[..., :MLA_NOPE]), _swap_halves(uq[..., MLA_NOPE:]), zq], axis=-1)
    ukv = w_ukv.reshape(MLA_KV_RANK, MLA_HEADS, MLA_NOPE + MLA_V)
    zk = jnp.zeros((MLA_KV_RANK, MLA_HEADS, LANES - MLA_NOPE), F32)
    wk = jnp.concatenate([ukv[..., :MLA_NOPE], zk], axis=-1)
    wv = jnp.concatenate([ukv[..., MLA_NOPE:], jnp.zeros((MLA_KV_RANK, MLA_HEADS, LANES - MLA_V), F32)], axis=-1)
    place = np.zeros((MLA_ROPE, MLA_HEADS, LANES), np.float32)
    ones = np.zeros((1, MLA_HEADS, LANES), np.float32)
    for hh in range(MLA_HEADS):
        place[np.arange(MLA_ROPE), hh, MLA_NOPE + np.arange(MLA_ROPE)] = 1.0
        ones[0, hh, MLA_V] = 1.0
    flat = lambda a: a.reshape(a.shape[0], MLA_HEADS * LANES)
    wo_mla = w_out[:MLA_HEADS * MLA_V].reshape(MLA_HEADS, MLA_V, D_MODEL)
    wo_mla = jnp.concatenate([wo_mla, jnp.zeros((MLA_HEADS, LANES - MLA_V, D_MODEL), F32)], axis=1)
    return {
        "wcq": wcq, "wckv": wckv, "wkr": wkr, "wkrs": _swap_halves(wkr), "wdq": wdq, "wdk": wdk, "wdv": wdv,
        "qn": q_norm.reshape(1, -1), "kvn": kv_norm.reshape(1, -1),
        "wq": flat(wq).astype(BF16), "wqs": flat(wqs).astype(BF16),
        "wk": flat(wk).astype(BF16), "wv": flat(wv).astype(BF16),
        "place": jnp.asarray(flat(place), BF16), "ones": jnp.asarray(flat(ones), F32),
        "wo_mla": wo_mla.reshape(MLA_HEADS * LANES, D_MODEL).astype(BF16),
        "wo_diff": w_out[MLA_HEADS * MLA_V:].astype(BF16),
    }


def _pad_rows(a, total, front=0):
    back = total - front - a.shape[1]
    return jnp.pad(a, ((0, 0), (front, back), (0, 0)))


def _round_up(n, mult):
    return -(-n // mult) * mult


def _diff_buckets(tq, tk, q_off, sk_real, skp):
    near_back = -((q_off - (T5_MAX_DIST - 1)) // tk - q_off // tk)
    last = (_round_up(q_off + tq, CHUNK) - 1) // tk
    last = min(last, skp // tk - 1)
    n_near = last - (q_off // tk - near_back) + 1
    i = np.arange(tq)[:, None]
    mats = []
    for r in range(n_near):
        kp = (q_off // tk - near_back + r) * tk + np.arange(tk)[None, :]
        qp = q_off + i
        ok = ((kp >> CHUNK_SHIFT) <= (qp >> CHUNK_SHIFT)) & (kp < sk_real)
        bkt = _t5_bucket(jnp.asarray(kp - qp, jnp.int32))
        mats.append(jnp.where(jnp.asarray(ok), bkt, -1))
    return jnp.stack(mats).astype(jnp.int32), near_back


def _trunk(x, q_off, caches, prm, cfg):
    b, sq, _ = x.shape
    m = b * sq
    tq, tk, tm, tf = cfg["tq"], cfg["tk"], cfg["tm"], cfg["tf"]
    sk_real = q_off + sq
    skp = _round_up(sk_real, tk)
    pos = q_off + jnp.arange(sq, dtype=jnp.int32)
    tabs = {k: jnp.tile(v, (b, 1)) for k, v in _rope_tables(pos).items()}
    x2 = x.reshape(m, D_MODEL)

    def with_past(past, new, dtype):
        new = new.reshape(b, sq, -1)
        if past is None:
            return new.astype(dtype)
        return jnp.concatenate([past.reshape(b, past.shape[1], -1).astype(dtype), new.astype(dtype)], axis=1)

    ew = prm["even"]
    qext, ckv, kr, dq, dk, dkb, dv, dvb = _even_proj(x2, prm["norm_mix"][0:1], ew, tabs, tm)
    past = (None,) * 4 if caches is None else tuple(c[0] for c in caches[:4])
    ckv_all = _pad_rows(with_past(past[0], ckv, F32), skp)
    kr_all = _pad_rows(with_past(past[1], kr, F32), skp)
    kext, vext = _kv_up(ckv_all.reshape(b * skp, -1), kr_all.reshape(b * skp, -1), ew, cfg["tm_kv"])
    kext = kext.reshape(b, skp, -1)
    vext = vext.reshape(b, skp, -1)
    tqs, n_sub = cfg["tqs"], cfg["n_sub"]
    assert n_sub == 1 or tqs == tk
    n_diag = (_round_up(q_off + tqs * n_sub, CHUNK) - 1) // tk - q_off // tk + 1
    tiles = dict(tqs=tqs, n_sub=n_sub, n_diag=n_diag, tkw=cfg["tkw"], tkn=tk, q_off=q_off)
    o_mla = _mla_attn(qext.reshape(b, sq, -1), kext, vext, hs=cfg["heads_per_step"], kv_buffers=cfg["kv_buffers"],
                      sk_real=sk_real, **tiles)
    dk_all = _pad_rows(with_past(past[2], dkb, BF16), skp)
    past_dv = past[3]
    if past_dv is not None:
        past_dv = jnp.concatenate([past_dv.astype(BF16), jnp.ones(past_dv.shape, BF16)], axis=-1)
    dv_all = _pad_rows(with_past(past_dv, dvb, BF16), skp)
    bkt, near_back = _diff_buckets(tqs, tk, q_off, sk_real, skp)
    assert bkt.shape[0] == near_back + n_diag - (n_sub - 1)
    lam_init = 0.8 - 0.6 * math.exp(-0.3 * 0)
    o_diff = _diff_attn(prm["t5"], dq.reshape(b, sq, -1), dk_all, dv_all, bkt, prm["lam_vecs"], prm["subln"],
                        near_back=near_back, far_bucket=T5_BUCKETS // 2 - 1, lam_init=lam_init, **tiles)
    x2 = _post(x2, o_mla.reshape(m, -1), o_diff.reshape(m, -1), ew["wo_mla"], ew["wo_diff"],
               prm["norm_ff"][0:1], prm["w_ff1"][0], prm["w_ff2"][0], prm["final_norm"],
               tm=tm, tf=tf, final_norm=False)
    new_even = (ckv.reshape(1, b, sq, MLA_KV_RANK), kr.reshape(1, b, sq, MLA_ROPE),
                dk.reshape(1, b, sq, DIFF_HEADS, 2 * DIFF_QK), dv.reshape(1, b, sq, DIFF_HEADS, DIFF_V))

    sq_, sk_, skb, sv_, svb, cq, ck, ckb, cv, cvb = _odd_proj(x2, prm["norm_mix"][1:2], prm["w_in_odd"], tm)
    past = (None,) * 4 if caches is None else tuple(c[0] for c in caches[4:])
    sk_all = _pad_rows(with_past(past[0], skb, BF16), skp)
    sv_all = _pad_rows(with_past(past[1], svb, BF16), skp)
    n_masked = (q_off + tq - 1) // tk - q_off // tk + 1
    pairs = dict(hp=cfg["heads_per_step"], kv_buffers=cfg["kv_buffers"])
    o_sb = _sb_attn(sq_.reshape(b, sq, -1), sk_all, sv_all, prm["tri"][tk], tq=tq, tk=tk, q_off=q_off,
                    n_masked=n_masked, **pairs)
    win_real = tq + CA_BAND
    win = _round_up(win_real, LANES)
    if caches is None:
        ck_all = _pad_rows(ckb.reshape(b, sq, -1), sq + CA_BAND + win - win_real, front=CA_BAND)
        cv_all = _pad_rows(cvb.reshape(b, sq, -1), sq + CA_BAND + win - win_real, front=CA_BAND)
    else:
        ck_all = _pad_rows(with_past(past[2], ckb, BF16), win)
        cv_all = _pad_rows(with_past(past[3], cvb, BF16), win)
    o_ca = _ca_attn(prm["ca_tab_t"], cq.reshape(b, sq, -1), ck_all, cv_all, tq=tq, win=win, win_real=win_real,
                    q_off=q_off, **pairs)
    x2 = _post(x2, o_sb.reshape(m, -1), o_ca.reshape(m, -1), prm["wo_sb"], prm["wo_ca"],
               prm["norm_ff"][1:2], prm["w_ff1"][1], prm["w_ff2"][1], prm["final_norm"],
               tm=tm, tf=tf, final_norm=True)

    heads = lambda a: a.reshape(b, sq, SB_HEADS, SB_DIM)
    if caches is None:
        nb = min(CA_BAND, sq)
        cak, cav = heads(ck)[:, sq - nb:], heads(cv)[:, sq - nb:]
    else:
        nb = past[2].shape[1]
        cak = jnp.concatenate([past[2], heads(ck)], axis=1)[:, sq:]
        cav = jnp.concatenate([past[3], heads(cv)], axis=1)[:, sq:]
        assert cak.shape[1] == nb
    new_odd = (heads(sk_)[None], heads(sv_)[None], cak[None], cav[None])
    return x2.reshape(b, sq, D_MODEL), new_even + new_odd


def _tri(tk):
    j = np.arange(tk)[:, None]
    s = np.arange(tk)[None, :]
    return jnp.asarray((j > s).astype(np.float32), BF16)


def kernel(x_prompt, x_sample, cache_mla_ckv, cache_mla_krope, cache_diff_k, cache_diff_v, cache_sb_k, cache_sb_v, cache_ca_k, cache_ca_v, norm_mix, norm_ff, w_in_even, mla_q_norm, mla_kv_norm, mla_w_uq, mla_w_ukv, diff_lambda_vecs, diff_subln, t5_bias, w_out_even, w_in_odd, ca_rel_bias, w_out_odd, w_ff1, w_ff2, final_norm):
    seq = x_prompt.shape[1]
    dec_seq = x_sample.shape[1]
    past_len = cache_mla_ckv.shape[2]
    assert cache_ca_k.shape[2] == CA_BAND and past_len % CHUNK == 0

    cfg_p = {"tq": 256, "tqs": 256, "n_sub": 4, "tk": 256, "tkw": 1024, "heads_per_step": 2, "kv_buffers": 1,
             "tm": min(512, seq), "tm_kv": min(512, seq), "tf": 512}
    rows_s = x_sample.shape[0] * dec_seq
    cfg_s = {"tq": dec_seq, "tqs": dec_seq, "n_sub": 1, "tk": 128, "tkw": 512, "heads_per_step": 2,
             "kv_buffers": None, "tm": rows_s, "tm_kv": x_sample.shape[0] * 128 // 2, "tf": 512}
    n_sb = SB_HEADS * SB_DIM
    prm = {
        "norm_mix": norm_mix, "norm_ff": norm_ff, "final_norm": final_norm.reshape(1, -1),
        "even": _even_weights(w_in_even[0], mla_q_norm[0], mla_kv_norm[0], mla_w_uq[0], mla_w_ukv[0],
                              w_out_even[0]),
        "t5": t5_bias, "lam_vecs": diff_lambda_vecs[0], "subln": diff_subln[0].reshape(1, -1),
        "w_in_odd": w_in_odd[0].astype(BF16), "ca_tab_t": ca_rel_bias[0].T,
        "wo_sb": w_out_odd[0][:n_sb].astype(BF16), "wo_ca": w_out_odd[0][n_sb:].astype(BF16),
        "w_ff1": w_ff1.astype(BF16), "w_ff2": w_ff2.astype(BF16),
        "tri": {tk: _tri(tk) for tk in {cfg_p["tk"], cfg_s["tk"]}},
    }
    y_prompt, new_p = _trunk(x_prompt, 0, None, prm, cfg_p)
    caches = (cache_mla_ckv, cache_mla_krope, cache_diff_k, cache_diff_v,
              cache_sb_k, cache_sb_v, cache_ca_k, cache_ca_v)
    y_sample, new_s = _trunk(x_sample, past_len, caches, prm, cfg_s)
    return (y_prompt, y_sample) + tuple(new_p) + tuple(new_s)
```

```python
import functools
import math

import numpy as np
import jax
import jax.numpy as jnp
from jax import lax
from jax.experimental import pallas as pl
from jax.experimental.pallas import tpu as pltpu

F32 = jnp.float32
BF16 = jnp.bfloat16

D_MODEL = 1024
CHUNK = 64
CHUNK_SHIFT = 6
EPS = 1e-6
NEG = -1e30

MLA_HEADS = 8
MLA_Q_RANK = 256
MLA_KV_RANK = 128
MLA_NOPE = 64
MLA_ROPE = 32
MLA_V = 64
ROPE_BASE = 10000.0
DIFF_HEADS = 4
DIFF_QK = 64
DIFF_V = 2 * DIFF_QK
T5_BUCKETS = 32
T5_MAX_DIST = 128
SB_HEADS = 8
SB_DIM = 64
CA_HEADS = 8
CA_DIM = 64
CA_LEFT_CHUNKS = 8
CA_BAND = CA_LEFT_CHUNKS * CHUNK
CA_MAX_REL = 128

LANES = 128
VMEM_LIMIT = 48 * 1024 * 1024
LOG2E = math.log2(math.e)
SB_EXIT = -104.0


def _cparams(n_axes):
    return pltpu.CompilerParams(dimension_semantics=("arbitrary",) * n_axes,
                                vmem_limit_bytes=VMEM_LIMIT)


def _rms(x, g):
    return x * lax.rsqrt(jnp.mean(x * x, axis=-1, keepdims=True) + EPS) * g


def _dot(a, b):
    return jnp.dot(a, b, preferred_element_type=F32)


def _dot_nt(a, b):
    return lax.dot_general(a, b, (((1,), (1,)), ((), ())), preferred_element_type=F32)


def _keep_lanes(q, keep):
    return jnp.where(keep, q.astype(F32), 0.0).astype(BF16)


def _full(shape):
    n = len(shape)
    return pl.BlockSpec(shape, lambda *_: (0,) * n)


def _rows(tm, width):
    return pl.BlockSpec((tm, width), lambda i: (i, 0))


def _even_proj_kernel(x_ref, g_ref, wcq_ref, wckv_ref, wkr_ref, wkrs_ref, wdq_ref, wdk_ref, wdv_ref,
                      qn_ref, kvn_ref, wq_ref, wqs_ref, cosq_ref, sinq_ref, cosk_ref, sink_ref,
                      qext_ref, ckv_ref, kr_ref, dq_ref, dk_ref, dkb_ref, dv_ref, dvb_ref):
    hn = _rms(x_ref[...], g_ref[...]).astype(BF16)
    cq = _rms(_dot(hn, wcq_ref[...]), qn_ref[...]).astype(BF16)
    cosq = jnp.concatenate([cosq_ref[...]] * MLA_HEADS, axis=1)
    sinq = jnp.concatenate([sinq_ref[...]] * MLA_HEADS, axis=1)
    qext_ref[...] = (_dot(cq, wq_ref[...]) * cosq + _dot(cq, wqs_ref[...]) * sinq).astype(BF16)
    ckv_ref[...] = _rms(_dot(hn, wckv_ref[...]), kvn_ref[...])
    kr_ref[...] = _dot(hn, wkr_ref[...]) * cosk_ref[...] + _dot(hn, wkrs_ref[...]) * sink_ref[...]
    dq_ref[...] = (_dot(hn, wdq_ref[...]) * (DIFF_QK ** -0.5 * LOG2E)).astype(BF16)
    dk = _dot(hn, wdk_ref[...])
    dk_ref[...] = dk
    dkb_ref[...] = dk.astype(BF16)
    dv = _dot(hn, wdv_ref[...])
    dv_ref[...] = dv
    dvb = dv.astype(BF16)
    ones = jnp.ones((dvb.shape[0], DIFF_V), BF16)
    dvb_ref[...] = jnp.concatenate(
        [piece for hh in range(DIFF_HEADS) for piece in (dvb[:, hh * DIFF_V:(hh + 1) * DIFF_V], ones)], axis=1)


def _even_proj(x, g, w, tabs, tm):
    m = x.shape[0]
    ins = [x, g, w["wcq"], w["wckv"], w["wkr"], w["wkrs"], w["wdq"], w["wdk"], w["wdv"],
           w["qn"], w["kvn"], w["wq"], w["wqs"], tabs["cosq"], tabs["sinq"], tabs["cosk"], tabs["sink"]]
    row_in = {0: D_MODEL, 13: LANES, 14: LANES, 15: MLA_ROPE, 16: MLA_ROPE}
    in_specs = [_rows(tm, row_in[i]) if i in row_in else _full(a.shape) for i, a in enumerate(ins)]
    outs = [(MLA_HEADS * LANES, BF16), (MLA_KV_RANK, F32), (MLA_ROPE, F32),
            (DIFF_HEADS * DIFF_V, BF16), (DIFF_HEADS * DIFF_V, F32), (DIFF_HEADS * DIFF_V, BF16),
            (DIFF_HEADS * DIFF_V, F32), (DIFF_HEADS * 2 * DIFF_V, BF16)]
    return pl.pallas_call(
        _even_proj_kernel,
        grid=(m // tm,),
        in_specs=in_specs,
        out_specs=[_rows(tm, n) for n, _ in outs],
        out_shape=[jax.ShapeDtypeStruct((m, n), dt) for n, dt in outs],
        compiler_params=_cparams(1),
        name="even_proj",
    )(*ins)


def _odd_proj_kernel(x_ref, g_ref, w_ref, sq_ref, sk_ref, skb_ref, sv_ref, svb_ref,
                     cq_ref, ck_ref, ckb_ref, cv_ref, cvb_ref):
    hn = _rms(x_ref[...], g_ref[...]).astype(BF16)
    width = SB_HEADS * SB_DIM

    def seg(i):
        return _dot(hn, w_ref[:, i * width:(i + 1) * width])

    sq_ref[...] = (seg(0) * (SB_DIM ** -0.5)).astype(BF16)
    for i, (f_ref, b_ref) in ((1, (sk_ref, skb_ref)), (2, (sv_ref, svb_ref)),
                              (4, (ck_ref, ckb_ref)), (5, (cv_ref, cvb_ref))):
        y = seg(i)
        f_ref[...] = y
        b_ref[...] = y.astype(BF16)
    cq_ref[...] = (seg(3) * (CA_DIM ** -0.5)).astype(BF16)


def _odd_proj(x, g, w, tm):
    m = x.shape[0]
    width = SB_HEADS * SB_DIM
    dts = [BF16, F32, BF16, F32, BF16, BF16, F32, BF16, F32, BF16]
    return pl.pallas_call(
        _odd_proj_kernel,
        grid=(m // tm,),
        in_specs=[_rows(tm, D_MODEL), _full(g.shape), _full(w.shape)],
        out_specs=[_rows(tm, width) for _ in dts],
        out_shape=[jax.ShapeDtypeStruct((m, width), dt) for dt in dts],
        compiler_params=_cparams(1),
        name="odd_proj",
    )(x, g, w)


def _kv_up_kernel(ckv_ref, kr_ref, wk_ref, wv_ref, place_ref, ones_ref, kext_ref, vext_ref):
    c = ckv_ref[...].astype(BF16)
    r = kr_ref[...].astype(BF16)
    kext_ref[...] = (_dot(c, wk_ref[...]) + _dot(r, place_ref[...])).astype(BF16)
    vext_ref[...] = (_dot(c, wv_ref[...]) + ones_ref[...]).astype(BF16)


def _kv_up(ckv, kr, w, tm):
    m = ckv.shape[0]
    width = MLA_HEADS * LANES
    return pl.pallas_call(
        _kv_up_kernel,
        grid=(m // tm,),
        in_specs=[_rows(tm, MLA_KV_RANK), _rows(tm, MLA_ROPE), _full(w["wk"].shape), _full(w["wv"].shape),
                  _full(w["place"].shape), _full(w["ones"].shape)],
        out_specs=[_rows(tm, width)] * 2,
        out_shape=[jax.ShapeDtypeStruct((m, width), BF16)] * 2,
        compiler_params=_cparams(1),
        name="mla_kv_up",
    )(ckv, kr, w["wk"], w["wv"], w["place"], w["ones"])


def _attn_specs(tq, skp, width=LANES, v_width=None, kv_buffers=None):
    kw = {} if kv_buffers is None else {"pipeline_mode": pl.Buffered(kv_buffers)}
    q_spec = pl.BlockSpec((None, tq, width), lambda b, h, qi: (b, qi, h))
    k_spec = pl.BlockSpec((None, skp, width), lambda b, h, qi: (b, 0, h), **kw)
    v_spec = pl.BlockSpec((None, skp, v_width or width), lambda b, h, qi: (b, 0, h), **kw)
    return q_spec, k_spec, v_spec


def _softmax_block(s, m_ref):
    m_old = m_ref[...]
    m_new = jnp.maximum(m_old, jnp.max(s, axis=-1, keepdims=True))
    m_ref[...] = m_new
    return jnp.exp2(m_old - m_new), jnp.exp2(s - jnp.tile(m_new, (1, s.shape[1] // LANES)))


def _sweep_blocks(nb0, near_back, n_sub, n_diag, tkw, tkn, do_block):
    first = jnp.maximum(nb0 - near_back, 0)
    done = 0
    for width in tuple(tkw) + (tkn,):
        per = width // tkn
        count = (first - done) // per

        def plain(j, carry, width=width, per=per, done=done):
            do_block(pl.multiple_of((done + j * per) * tkn, width), width, None)
            return carry

        lax.fori_loop(0, count, plain, 0)
        done = done + count * per
    for c in range(-near_back, n_diag):
        kinds = []
        for r in range(n_sub):
            d = c - r
            kinds.append("skip" if d > 0 else None if d < -near_back else d + near_back)

        @pl.when(nb0 + c >= 0)
        def _special(c=c, kinds=kinds):
            do_block(pl.multiple_of((nb0 + c) * tkn, tkn), tkn, kinds)


def _mla_kernel(q_ref, k_ref, v_ref, o_ref, m_ref, acc_ref, *, hs, tqs, n_sub, n_diag, tkw, tkn, q_off, sk_real):
    qi = pl.program_id(2)
    qpos0 = q_off + qi * (tqs * n_sub)
    m_ref[...] = jnp.full(m_ref.shape, NEG, F32)
    acc_ref[...] = jnp.zeros(acc_ref.shape, F32)

    def block(start, width, kinds):
        for hh in range(hs):
            cols = slice(hh * LANES, (hh + 1) * LANES)
            k = k_ref[pl.ds(start, width), cols]
            v = v_ref[pl.ds(start, width), cols]
            for r in range(n_sub):
                kind = None if kinds is None else kinds[r]
                if kind == "skip":
                    continue
                s = _dot_nt(q_ref[r * tqs:(r + 1) * tqs, cols], k)
                if kind is not None:
                    qp = qpos0 + r * tqs + lax.broadcasted_iota(jnp.int32, (tqs, width), 0)
                    kp = start + lax.broadcasted_iota(jnp.int32, (tqs, width), 1)
                    ok = ((kp >> CHUNK_SHIFT) <= (qp >> CHUNK_SHIFT)) & (kp < sk_real)
                    s = jnp.where(ok, s, NEG)
                alpha, p = _softmax_block(s, m_ref.at[hh, r])
                acc_ref[hh, r] = acc_ref[hh, r] * alpha + _dot(p.astype(BF16), v)

    _sweep_blocks(qpos0 // tkn, 0, n_sub, n_diag, tkw, tkn, block)

    for hh in range(hs):
        for r in range(n_sub):
            acc = acc_ref[hh, r]
            lane = lax.broadcasted_iota(jnp.int32, acc.shape, 1)
            denom = jnp.sum(jnp.where(lane == MLA_V, acc, 0.0), axis=-1, keepdims=True)
            o_ref[r * tqs:(r + 1) * tqs, hh * LANES:(hh + 1) * LANES] = (
                jnp.where(lane < MLA_V, acc / denom, 0.0).astype(BF16))


def _mla_attn(qext, kext, vext, *, hs, kv_buffers, tqs, n_sub, n_diag, tkw, tkn, q_off, sk_real):
    b, sq, _ = qext.shape
    skp = kext.shape[1]
    tq = tqs * n_sub
    q_spec, k_spec, v_spec = _attn_specs(tq, skp, width=hs * LANES, kv_buffers=kv_buffers)
    kern = functools.partial(_mla_kernel, hs=hs, tqs=tqs, n_sub=n_sub, n_diag=n_diag, tkw=tkw, tkn=tkn,
                             q_off=q_off, sk_real=sk_real)
    return pl.pallas_call(
        kern,
        grid=(b, MLA_HEADS // hs, sq // tq),
        in_specs=[q_spec, k_spec, v_spec],
        out_specs=q_spec,
        out_shape=jax.ShapeDtypeStruct((b, sq, MLA_HEADS * LANES), BF16),
        scratch_shapes=[pltpu.VMEM((hs, n_sub, tqs, LANES), F32), pltpu.VMEM((hs, n_sub, tqs, LANES), F32)],
        compiler_params=_cparams(3),
        name="mla_attn",
    )(qext, kext, vext)


def _diff_kernel(t5_ref, q_ref, k_ref, v_ref, bkt_ref, lamv_ref, subln_ref, o_ref,
                 bias_ref, m_ref, acc_ref, *, tqs, n_sub, n_diag, tkw, tkn, q_off, n_near, near_back, far_bucket,
                 lam_init):
    first = (pl.program_id(0) == 0) & (pl.program_id(1) == 0) & (pl.program_id(2) == 0)

    @pl.when(first)
    def _build_bias():
        for r in range(n_near):
            bkt = bkt_ref[r]
            vals = [jnp.full((tqs, tkn), NEG, F32) for _ in range(DIFF_HEADS)]
            for t in range(T5_BUCKETS):
                hit = bkt == t
                for hh in range(DIFF_HEADS):
                    vals[hh] = jnp.where(hit, (t5_ref[t, hh] - t5_ref[far_bucket, hh]) * LOG2E, vals[hh])
            for hh in range(DIFF_HEADS):
                bias_ref[r, hh] = vals[hh]

    h = pl.program_id(1)
    qi = pl.program_id(2)
    qpos0 = q_off + qi * (tqs * n_sub)
    q = q_ref[...]
    lane = lax.broadcasted_iota(jnp.int32, q.shape, 1)
    qm = [_keep_lanes(q, lane < DIFF_QK), _keep_lanes(q, lane >= DIFF_QK)]
    m_ref[...] = jnp.full(m_ref.shape, NEG, F32)
    acc_ref[...] = jnp.zeros(acc_ref.shape, F32)

    def block(start, width, kinds):
        k = k_ref[pl.ds(start, width), :]
        v = v_ref[pl.ds(start, width), :]
        for r in range(n_sub):
            kind = None if kinds is None else kinds[r]
            if kind == "skip":
                continue
            for mi in range(2):
                s = _dot_nt(qm[mi][r * tqs:(r + 1) * tqs, :], k)
                if kind is not None:
                    s = s + bias_ref[kind, h]
                alpha, p = _softmax_block(s, m_ref.at[mi, r])
                acc_ref[mi, r] = acc_ref[mi, r] * jnp.tile(alpha, (1, 2)) + _dot(p.astype(BF16), v)

    _sweep_blocks(qpos0 // tkn, near_back, n_sub, n_diag, tkw, tkn, block)

    lv = lamv_ref[...]
    lam = (jnp.exp(jnp.sum(lv[0:1] * lv[1:2], axis=-1, keepdims=True))
           - jnp.exp(jnp.sum(lv[2:3] * lv[3:4], axis=-1, keepdims=True)) + lam_init)
    for r in range(n_sub):
        a0, a1 = acc_ref[0, r], acc_ref[1, r]
        o = a0[:, :DIFF_V] / a0[:, DIFF_V:] - lam * (a1[:, :DIFF_V] / a1[:, DIFF_V:])
        o_ref[r * tqs:(r + 1) * tqs, :] = (_rms(o, subln_ref[...]) * (1.0 - lam_init)).astype(BF16)


def _diff_attn(t5, dq, dk, dv, bkt, lamv, subln, *, tqs, n_sub, n_diag, tkw, tkn, q_off, near_back, far_bucket,
               lam_init):
    b, sq, _ = dq.shape
    skp = dk.shape[1]
    n_near = bkt.shape[0]
    tq = tqs * n_sub
    q_spec, k_spec, v_spec = _attn_specs(tq, skp, v_width=2 * DIFF_V)
    kern = functools.partial(_diff_kernel, tqs=tqs, n_sub=n_sub, n_diag=n_diag, tkw=tkw, tkn=tkn, q_off=q_off,
                             n_near=n_near, near_back=near_back, far_bucket=far_bucket, lam_init=lam_init)
    return pl.pallas_call(
        kern,
        grid=(b, DIFF_HEADS, sq // tq),
        in_specs=[pl.BlockSpec(memory_space=pltpu.SMEM), q_spec, k_spec, v_spec,
                  _full(bkt.shape), _full(lamv.shape), _full(subln.shape)],
        out_specs=q_spec,
        out_shape=jax.ShapeDtypeStruct((b, sq, DIFF_HEADS * DIFF_V), BF16),
        scratch_shapes=[pltpu.VMEM((n_near, DIFF_HEADS, tqs, tkn), F32), pltpu.VMEM((2, n_sub, tqs, LANES), F32),
                        pltpu.VMEM((2, n_sub, tqs, 2 * DIFF_V), F32)],
        compiler_params=_cparams(3),
        name="diff_attn",
    )(t5, dq, dk, dv, bkt, lamv, subln)


def _sb_kernel(q_ref, k_ref, v_ref, tri_ref, o_ref, run_ref, acc_ref, *, hp, tq, tk, q_off, n_masked):
    qi = pl.program_id(2)
    qpos0 = q_off + qi * tq
    top = (qpos0 + tq - 1) // tk
    lane = lax.broadcasted_iota(jnp.int32, (tq, LANES), 1)
    tri = tri_ref[...]
    qhs = []
    for pp in range(hp):
        qpair = q_ref[:, pp * LANES:(pp + 1) * LANES]
        qhs.append([_keep_lanes(qpair, lane < SB_DIM), _keep_lanes(qpair, lane >= SB_DIM)])

    def block(kj, masked):
        ks = pl.multiple_of(kj * tk, tk)
        if masked:
            qp = qpos0 + lax.broadcasted_iota(jnp.int32, (tq, tk), 0)
            kp = kj * tk + lax.broadcasted_iota(jnp.int32, (tq, tk), 1)
            ok = kp < qp
        for pp in range(hp):
            cols = slice(pp * LANES, (pp + 1) * LANES)
            k = k_ref[pl.ds(ks, tk), cols]
            v = v_ref[pl.ds(ks, tk), cols]
            for hh in range(2):
                z = _dot_nt(qhs[pp][hh], k)
                sp = jnp.maximum(z, 0.0) + jnp.log(1.0 + jnp.exp2(jnp.abs(z) * -LOG2E))
                log_beta = z - sp
                if masked:
                    sp = jnp.where(ok, sp, 0.0)
                hi = sp.astype(BF16)
                lo = (sp - hi.astype(F32)).astype(BF16)
                later = _dot(hi, tri) + _dot(lo, tri)
                a = jnp.exp2(((log_beta - run_ref[pp, hh]) - later) * LOG2E)
                if masked:
                    a = jnp.where(ok, a, 0.0)
                acc_ref[pp, hh] += _dot(a.astype(BF16), v)
                run_ref[pp, hh] += jnp.sum(sp, axis=-1, keepdims=True)

    run_ref[...] = jnp.zeros(run_ref.shape, F32)
    acc_ref[...] = jnp.zeros(acc_ref.shape, F32)
    for r in range(n_masked):
        block(top - r, True)

    def cond(kj):
        return (kj >= 0) & (jnp.min(run_ref[...]) < -SB_EXIT)

    def body(kj):
        block(kj, False)
        return kj - 1

    lax.while_loop(cond, body, top - n_masked)
    for pp in range(hp):
        o_ref[:, pp * LANES:(pp + 1) * LANES] = jnp.where(lane < SB_DIM, acc_ref[pp, 0], acc_ref[pp, 1]).astype(BF16)


def _sb_attn(sq_, sk_, sv_, tri, *, hp, kv_buffers, tq, tk, q_off, n_masked):
    b, sq, _ = sq_.shape
    skp = sk_.shape[1]
    q_spec, kv_spec, _ = _attn_specs(tq, skp, width=hp * LANES, kv_buffers=kv_buffers)
    kern = functools.partial(_sb_kernel, hp=hp, tq=tq, tk=tk, q_off=q_off, n_masked=n_masked)
    return pl.pallas_call(
        kern,
        grid=(b, SB_HEADS // (2 * hp), sq // tq),
        in_specs=[q_spec, kv_spec, kv_spec, _full(tri.shape)],
        out_specs=q_spec,
        out_shape=jax.ShapeDtypeStruct((b, sq, SB_HEADS * SB_DIM), BF16),
        scratch_shapes=[pltpu.VMEM((hp, 2, tq, 1), F32), pltpu.VMEM((hp, 2, tq, LANES), F32)],
        compiler_params=_cparams(3),
        name="sb_attn",
    )(sq_, sk_, sv_, tri)


CA_ROW_GROUP = 16


def _ca_kernel(tab_ref, q_ref, k_ref, v_ref, o_ref, bias_ref, *, hp, tq, win, win_real, q_off):
    first = (pl.program_id(0) == 0) & (pl.program_id(1) == 0) & (pl.program_id(2) == 0)
    n_shift = -(-(tq + win) // LANES) * LANES
    n_rel = n_shift + LANES

    @pl.when(first)
    def _build_bias():
        x = lax.broadcasted_iota(jnp.int32, (CA_HEADS, n_rel), 1)
        idx = jnp.clip(x - (tq - 1) - CA_BAND, -CA_MAX_REL, CA_MAX_REL) + CA_MAX_REL
        f = jnp.zeros((CA_HEADS, n_rel), F32)
        for t in range(2 * CA_MAX_REL + 1):
            f = jnp.where(idx == t, tab_ref[:, t:t + 1], f)
        g = CA_ROW_GROUP
        i_loc = lax.broadcasted_iota(jnp.int32, (g, win), 0)
        j_loc = lax.broadcasted_iota(jnp.int32, (g, win), 1)
        for hh in range(CA_HEADS):
            fh = f[hh:hh + 1, :]
            shifted = jnp.concatenate([fh[:, g - 1 - bb:g - 1 - bb + n_shift] for bb in range(g)], axis=0)
            for a in range(tq // g):
                start = tq - g * a - g
                tile = shifted[:, start:start + win]
                i = i_loc + g * a
                kc = (j_loc >> CHUNK_SHIFT) - CA_LEFT_CHUNKS
                qc = i >> CHUNK_SHIFT
                ok = (kc <= qc) & (kc >= qc - CA_LEFT_CHUNKS) & (j_loc < win_real)
                bias_ref[hh, g * a:g * a + g, :] = jnp.where(ok, tile, NEG)

    group = pl.program_id(1)
    qi = pl.program_id(2)
    qpos0 = q_off + qi * tq
    ws = pl.multiple_of(qi * tq, tq)
    lane = lax.broadcasted_iota(jnp.int32, (tq, LANES), 1)
    kpos = qpos0 - CA_BAND + lax.broadcasted_iota(jnp.int32, (tq, win), 1)
    for pp in range(hp):
        cols = slice(pp * LANES, (pp + 1) * LANES)
        q = q_ref[:, cols]
        k = k_ref[pl.ds(ws, win), cols]
        v = v_ref[pl.ds(ws, win), cols]
        outs = []
        for hh in range(2):
            qh = _keep_lanes(q, (lane < CA_DIM) if hh == 0 else (lane >= CA_DIM))
            s = _dot_nt(qh, k) + bias_ref[2 * (group * hp + pp) + hh]
            s = jnp.where(kpos >= 0, s, NEG)
            m = jnp.max(s, axis=-1, keepdims=True)
            p = jnp.exp(s - m)
            denom = jnp.sum(p, axis=-1, keepdims=True)
            outs.append(_dot(p.astype(BF16), v) / denom)
        o_ref[:, cols] = jnp.where(lane < CA_DIM, outs[0], outs[1]).astype(BF16)


def _ca_attn(tab_t, cq, ck, cv, *, hp, kv_buffers, tq, win, win_real, q_off):
    b, sq, _ = cq.shape
    skp = ck.shape[1]
    q_spec, kv_spec, _ = _attn_specs(tq, skp, width=hp * LANES, kv_buffers=kv_buffers)
    kern = functools.partial(_ca_kernel, hp=hp, tq=tq, win=win, win_real=win_real, q_off=q_off)
    return pl.pallas_call(
        kern,
        grid=(b, CA_HEADS // (2 * hp), sq // tq),
        in_specs=[_full(tab_t.shape), q_spec, kv_spec, kv_spec],
        out_specs=q_spec,
        out_shape=jax.ShapeDtypeStruct((b, sq, CA_HEADS * CA_DIM), BF16),
        scratch_shapes=[pltpu.VMEM((CA_HEADS, tq, win), F32)],
        compiler_params=_cparams(3),
        name="ca_attn",
    )(tab_t, cq, ck, cv)


def _post_kernel(x_ref, oa_ref, ob_ref, wa_ref, wb_ref, g_ref, w1_ref, w2_ref, gf_ref, y_ref,
                 x1_ref, hn_ref, acc_ref, *, final_norm):
    j = pl.program_id(1)

    @pl.when(j == 0)
    def _mix():
        x1 = x_ref[...] + _dot(oa_ref[...], wa_ref[...]) + _dot(ob_ref[...], wb_ref[...])
        x1_ref[...] = x1
        hn_ref[...] = _rms(x1, g_ref[...]).astype(BF16)
        acc_ref[...] = jnp.zeros(acc_ref.shape, F32)

    a = jnp.maximum(_dot(hn_ref[...], w1_ref[...]), 0.0)
    acc_ref[...] += _dot((a * a).astype(BF16), w2_ref[...])

    @pl.when(j == pl.num_programs(1) - 1)
    def _finish():
        y = x1_ref[...] + acc_ref[...]
        if final_norm:
            y = _rms(y, gf_ref[...])
        y_ref[...] = y


def _post(x, oa, ob, wa, wb, g, w1, w2, gf, *, tm, tf, final_norm):
    m = x.shape[0]
    d_ff = w1.shape[1]
    rows = lambda width: pl.BlockSpec((tm, width), lambda i, j: (i, 0))
    return pl.pallas_call(
        functools.partial(_post_kernel, final_norm=final_norm),
        grid=(m // tm, d_ff // tf),
        in_specs=[rows(D_MODEL), rows(oa.shape[1]), rows(ob.shape[1]), _full(wa.shape), _full(wb.shape),
                  _full(g.shape), pl.BlockSpec((D_MODEL, tf), lambda i, j: (0, j)),
                  pl.BlockSpec((tf, D_MODEL), lambda i, j: (j, 0)), _full(gf.shape)],
        out_specs=rows(D_MODEL),
        out_shape=jax.ShapeDtypeStruct((m, D_MODEL), F32),
        scratch_shapes=[pltpu.VMEM((tm, D_MODEL), F32), pltpu.VMEM((tm, D_MODEL), BF16),
                        pltpu.VMEM((tm, D_MODEL), F32)],
        compiler_params=_cparams(2),
        name="post_mlp",
    )(x, oa, ob, wa, wb, g, w1, w2, gf)


def _t5_bucket(rel):
    nb = T5_BUCKETS // 2
    max_exact = nb // 2
    ret = jnp.where(rel > 0, nb, 0)
    n = jnp.abs(rel)
    nf = jnp.maximum(n, 1).astype(F32)
    large = max_exact + (jnp.log(nf / max_exact) / math.log(T5_MAX_DIST / max_exact) * (nb - max_exact)).astype(jnp.int32)
    large = jnp.minimum(large, nb - 1)
    return ret + jnp.where(n < max_exact, n, large)


def _rope_tables(pos):
    half = MLA_ROPE // 2
    inv = ROPE_BASE ** (-jnp.arange(half, dtype=F32) / half)
    ang = pos.astype(F32)[:, None] * inv[None, :]
    cos, sin = jnp.cos(ang), jnp.sin(ang)
    cosk = jnp.concatenate([cos, cos], axis=1)
    sink = jnp.concatenate([-sin, sin], axis=1)
    n = pos.shape[0]
    pad = jnp.zeros((n, LANES - MLA_NOPE - MLA_ROPE), F32)
    qscale = (MLA_NOPE + MLA_ROPE) ** -0.5 * LOG2E
    cosq = jnp.concatenate([jnp.ones((n, MLA_NOPE), F32), cosk, pad], axis=1) * qscale
    sinq = jnp.concatenate([jnp.zeros((n, MLA_NOPE), F32), sink, pad], axis=1) * qscale
    return {"cosq": cosq, "sinq": sinq, "cosk": cosk, "sink": sink}


def _swap_halves(w):
    half = w.shape[-1] // 2
    return jnp.concatenate([w[..., half:], w[..., :half]], axis=-1)


def _even_weights(w_in, q_norm, kv_norm, w_uq, w_ukv, w_out):
    sizes = [MLA_Q_RANK, MLA_KV_RANK, MLA_ROPE, DIFF_HEADS * 2 * DIFF_QK, DIFF_HEADS * 2 * DIFF_QK,
             DIFF_HEADS * DIFF_V]
    offs = np.cumsum([0] + sizes)
    wcq, wckv, wkr, wdq, wdk, wdv = (w_in[:, offs[i]:offs[i + 1]].astype(BF16) for i in range(6))
    uq = w_uq.reshape(MLA_Q_RANK, MLA_HEADS, MLA_NOPE + MLA_ROPE)
    zq = jnp.zeros((MLA_Q_RANK, MLA_HEADS, LANES - MLA_NOPE - MLA_ROPE), F32)
    wq = jnp.concatenate([uq, zq], axis=-1)
    wqs = jnp.concatenate([jnp.zeros_like(uq[..., :MLA_NOPE]), _swap_halves(uq[..., MLA_NOPE:]), zq], axis=-1)
    ukv = w_ukv.reshape(MLA_KV_RANK, MLA_HEADS, MLA_NOPE + MLA_V)
    zk = jnp.zeros((MLA_KV_RANK, MLA_HEADS, LANES - MLA_NOPE), F32)
    wk = jnp.concatenate([ukv[..., :MLA_NOPE], zk], axis=-1)
    wv = jnp.concatenate([ukv[..., MLA_NOPE:], jnp.zeros((MLA_KV_RANK, MLA_HEADS, LANES - MLA_V), F32)], axis=-1)
    place = np.zeros((MLA_ROPE, MLA_HEADS, LANES), np.float32)
    ones = np.zeros((1, MLA_HEADS, LANES), np.float32)
    for hh in range(MLA_HEADS):
        place[np.arange(MLA_ROPE), hh, MLA_NOPE + np.arange(MLA_ROPE)] = 1.0
        ones[0, hh, MLA_V] = 1.0
    flat = lambda a: a.reshape(a.shape[0], MLA_HEADS * LANES)
    wo_mla = w_out[:MLA_HEADS * MLA_V].reshape(MLA_HEADS, MLA_V, D_MODEL)
    wo_mla = jnp.concatenate([wo_mla, jnp.zeros((MLA_HEADS, LANES - MLA_V, D_MODEL), F32)], axis=1)
    return {
        "wcq": wcq, "wckv": wckv, "wkr": wkr, "wkrs": _swap_halves(wkr), "wdq": wdq, "wdk": wdk, "wdv": wdv,
        "qn": q_norm.reshape(1, -1), "kvn": kv_norm.reshape(1, -1),
        "wq": flat(wq).astype(BF16), "wqs": flat(wqs).astype(BF16),
        "wk": flat(wk).astype(BF16), "wv": flat(wv).astype(BF16),
        "place": jnp.asarray(flat(place), BF16), "ones": jnp.asarray(flat(ones), F32),
        "wo_mla": wo_mla.reshape(MLA_HEADS * LANES, D_MODEL).astype(BF16),
        "wo_diff": w_out[MLA_HEADS * MLA_V:].astype(BF16),
    }


def _pad_rows(a, total, front=0):
    back = total - front - a.shape[1]
    return jnp.pad(a, ((0, 0), (front, back), (0, 0)))


def _round_up(n, mult):
    return -(-n // mult) * mult


def _diff_buckets(tq, tk, q_off, sk_real, skp):
    near_back = -((q_off - (T5_MAX_DIST - 1)) // tk - q_off // tk)
    last = (_round_up(q_off + tq, CHUNK) - 1) // tk
    last = min(last, skp // tk - 1)
    n_near = last - (q_off // tk - near_back) + 1
    i = np.arange(tq)[:, None]
    mats = []
    for r in range(n_near):
        kp = (q_off // tk - near_back + r) * tk + np.arange(tk)[None, :]
        qp = q_off + i
        ok = ((kp >> CHUNK_SHIFT) <= (qp >> CHUNK_SHIFT)) & (kp < sk_real)
        bkt = _t5_bucket(jnp.asarray(kp - qp, jnp.int32))
        mats.append(jnp.where(jnp.asarray(ok), bkt, -1))
    return jnp.stack(mats).astype(jnp.int32), near_back


def _trunk(x, q_off, caches, prm, cfg):
    b, sq, _ = x.shape
    m = b * sq
    tq, tk, tm, tf = cfg["tq"], cfg["tk"], cfg["tm"], cfg["tf"]
    sk_real = q_off + sq
    skp = _round_up(sk_real, tk)
    pos = q_off + jnp.arange(sq, dtype=jnp.int32)
    tabs = {k: jnp.tile(v, (b, 1)) for k, v in _rope_tables(pos).items()}
    x2 = x.reshape(m, D_MODEL)

    def with_past(past, new, dtype):
        new = new.reshape(b, sq, -1)
        if past is None:
            return new.astype(dtype)
        return jnp.concatenate([past.reshape(b, past.shape[1], -1).astype(dtype), new.astype(dtype)], axis=1)

    ew = prm["even"]
    qext, ckv, kr, dq, dk, dkb, dv, dvb = _even_proj(x2, prm["norm_mix"][0:1], ew, tabs, tm)
    past = (None,) * 4 if caches is None else tuple(c[0] for c in caches[:4])
    ckv_all = _pad_rows(with_past(past[0], ckv, F32), skp)
    kr_all = _pad_rows(with_past(past[1], kr, F32), skp)
    kext, vext = _kv_up(ckv_all.reshape(b * skp, -1), kr_all.reshape(b * skp, -1), ew, cfg["tm_kv"])
    kext = kext.reshape(b, skp, -1)
    vext = vext.reshape(b, skp, -1)
    tqs, n_sub = cfg["tqs"], cfg["n_sub"]
    assert n_sub == 1 or tqs == tk
    n_diag = (_round_up(q_off + tqs * n_sub, CHUNK) - 1) // tk - q_off // tk + 1
    tiles = dict(tqs=tqs, n_sub=n_sub, n_diag=n_diag, tkw=cfg["tkw"], tkn=tk, q_off=q_off)
    o_mla = _mla_attn(qext.reshape(b, sq, -1), kext, vext, hs=cfg["heads_per_step"], kv_buffers=cfg["kv_buffers"],
                      sk_real=sk_real, **tiles)
    dk_all = _pad_rows(with_past(past[2], dkb, BF16), skp)
    past_dv = past[3]
    if past_dv is not None:
        past_dv = jnp.concatenate([past_dv.astype(BF16), jnp.ones(past_dv.shape, BF16)], axis=-1)
    dv_all = _pad_rows(with_past(past_dv, dvb, BF16), skp)
    bkt, near_back = _diff_buckets(tqs, tk, q_off, sk_real, skp)
    assert bkt.shape[0] == near_back + n_diag - (n_sub - 1)
    lam_init = 0.8 - 0.6 * math.exp(-0.3 * 0)
    o_diff = _diff_attn(prm["t5"], dq.reshape(b, sq, -1), dk_all, dv_all, bkt, prm["lam_vecs"], prm["subln"],
                        near_back=near_back, far_bucket=T5_BUCKETS // 2 - 1, lam_init=lam_init, **tiles)
    x2 = _post(x2, o_mla.reshape(m, -1), o_diff.reshape(m, -1), ew["wo_mla"], ew["wo_diff"],
               prm["norm_ff"][0:1], prm["w_ff1"][0], prm["w_ff2"][0], prm["final_norm"],
               tm=tm, tf=tf, final_norm=False)
    new_even = (ckv.reshape(1, b, sq, MLA_KV_RANK), kr.reshape(1, b, sq, MLA_ROPE),
                dk.reshape(1, b, sq, DIFF_HEADS, 2 * DIFF_QK), dv.reshape(1, b, sq, DIFF_HEADS, DIFF_V))

    sq_, sk_, skb, sv_, svb, cq, ck, ckb, cv, cvb = _odd_proj(x2, prm["norm_mix"][1:2], prm["w_in_odd"], tm)
    past = (None,) * 4 if caches is None else tuple(c[0] for c in caches[4:])
    sk_all = _pad_rows(with_past(past[0], skb, BF16), skp)
    sv_all = _pad_rows(with_past(past[1], svb, BF16), skp)
    n_masked = (q_off + tq - 1) // tk - q_off // tk + 1
    pairs = dict(hp=cfg["heads_per_step"], kv_buffers=cfg["kv_buffers"])
    o_sb = _sb_attn(sq_.reshape(b, sq, -1), sk_all, sv_all, prm["tri"][tk], tq=tq, tk=tk, q_off=q_off,
                    n_masked=n_masked, **pairs)
    win_real = tq + CA_BAND
    win = _round_up(win_real, LANES)
    if caches is None:
        ck_all = _pad_rows(ckb.reshape(b, sq, -1), sq + CA_BAND + win - win_real, front=CA_BAND)
        cv_all = _pad_rows(cvb.reshape(b, sq, -1), sq + CA_BAND + win - win_real, front=CA_BAND)
    else:
        ck_all = _pad_rows(with_past(past[2], ckb, BF16), win)
        cv_all = _pad_rows(with_past(past[3], cvb, BF16), win)
    o_ca = _ca_attn(prm["ca_tab_t"], cq.reshape(b, sq, -1), ck_all, cv_all, tq=tq, win=win, win_real=win_real,
                    q_off=q_off, **pairs)
    x2 = _post(x2, o_sb.reshape(m, -1), o_ca.reshape(m, -1), prm["wo_sb"], prm["wo_ca"],
               prm["norm_ff"][1:2], prm["w_ff1"][1], prm["w_ff2"][1], prm["final_norm"],
               tm=tm, tf=tf, final_norm=True)

    heads = lambda a: a.reshape(b, sq, SB_HEADS, SB_DIM)
    if caches is None:
        nb = min(CA_BAND, sq)
        cak, cav = heads(ck)[:, sq - nb:], heads(cv)[:, sq - nb:]
    else:
        nb = past[2].shape[1]
        cak = jnp.concatenate([past[2], heads(ck)], axis=1)[:, sq:]
        cav = jnp.concatenate([past[3], heads(cv)], axis=1)[:, sq:]
        assert cak.shape[1] == nb
    new_odd = (heads(sk_)[None], heads(sv_)[None], cak[None], cav[None])
    return x2.reshape(b, sq, D_MODEL), new_even + new_odd


def _tri(tk):
    j = np.arange(tk)[:, None]
    s = np.arange(tk)[None, :]
    return jnp.asarray((j > s).astype(np.float32), BF16)


def kernel(x_prompt, x_sample, cache_mla_ckv, cache_mla_krope, cache_diff_k, cache_diff_v, cache_sb_k, cache_sb_v, cache_ca_k, cache_ca_v, norm_mix, norm_ff, w_in_even, mla_q_norm, mla_kv_norm, mla_w_uq, mla_w_ukv, diff_lambda_vecs, diff_subln, t5_bias, w_out_even, w_in_odd, ca_rel_bias, w_out_odd, w_ff1, w_ff2, final_norm):
    seq = x_prompt.shape[1]
    dec_seq = x_sample.shape[1]
    past_len = cache_mla_ckv.shape[2]
    assert cache_ca_k.shape[2] == CA_BAND and past_len % CHUNK == 0

    cfg_p = {"tq": 256, "tqs": 256, "n_sub": 4, "tk": 256, "tkw": (2048, 1024), "heads_per_step": 2, "kv_buffers": 1,
             "tm": min(512, seq), "tm_kv": min(512, seq), "tf": 512}
    rows_s = x_sample.shape[0] * dec_seq
    cfg_s = {"tq": dec_seq, "tqs": dec_seq, "n_sub": 1, "tk": 128, "tkw": (512,), "heads_per_step": 2,
             "kv_buffers": None, "tm": rows_s, "tm_kv": x_sample.shape[0] * 128 // 2, "tf": 512}
    n_sb = SB_HEADS * SB_DIM
    prm = {
        "norm_mix": norm_mix, "norm_ff": norm_ff, "final_norm": final_norm.reshape(1, -1),
        "even": _even_weights(w_in_even[0], mla_q_norm[0], mla_kv_norm[0], mla_w_uq[0], mla_w_ukv[0],
                              w_out_even[0]),
        "t5": t5_bias, "lam_vecs": diff_lambda_vecs[0], "subln": diff_subln[0].reshape(1, -1),
        "w_in_odd": w_in_odd[0].astype(BF16), "ca_tab_t": ca_rel_bias[0].T,
        "wo_sb": w_out_odd[0][:n_sb].astype(BF16), "wo_ca": w_out_odd[0][n_sb:].astype(BF16),
        "w_ff1": w_ff1.astype(BF16), "w_ff2": w_ff2.astype(BF16),
        "tri": {tk: _tri(tk) for tk in {cfg_p["tk"], cfg_s["tk"]}},
    }
    y_prompt, new_p = _trunk(x_prompt, 0, None, prm, cfg_p)
    caches = (cache_mla_ckv, cache_mla_krope, cache_diff_k, cache_diff_v,
              cache_sb_k, cache_sb_v, cache_ca_k, cache_ca_v)
    y_sample, new_s = _trunk(x_sample, past_len, caches, prm, cfg_s)
    return (y_prompt, y_sample) + tuple(new_p) + tuple(new_s)
```

```python
import functools
import math

import numpy as np
import jax
import jax.numpy as jnp
from jax import lax
from jax.experimental import pallas as pl
from jax.experimental.pallas import tpu as pltpu

F32 = jnp.float32
BF16 = jnp.bfloat16

D_MODEL = 1024
CHUNK = 64
CHUNK_SHIFT = 6
EPS = 1e-6
NEG = -1e30

MLA_HEADS = 8
MLA_Q_RANK = 256
MLA_KV_RANK = 128
MLA_NOPE = 64
MLA_ROPE = 32
MLA_V = 64
ROPE_BASE = 10000.0
DIFF_HEADS = 4
DIFF_QK = 64
DIFF_V = 2 * DIFF_QK
T5_BUCKETS = 32
T5_MAX_DIST = 128
SB_HEADS = 8
SB_DIM = 64
CA_HEADS = 8
CA_DIM = 64
CA_LEFT_CHUNKS = 8
CA_BAND = CA_LEFT_CHUNKS * CHUNK
CA_MAX_REL = 128

LANES = 128
VMEM_LIMIT = 48 * 1024 * 1024
LOG2E = math.log2(math.e)
SB_EXIT = -104.0


def _cparams(n_axes):
    return pltpu.CompilerParams(dimension_semantics=("arbitrary",) * n_axes,
                                vmem_limit_bytes=VMEM_LIMIT)


def _rms(x, g):
    return x * lax.rsqrt(jnp.mean(x * x, axis=-1, keepdims=True) + EPS) * g


def _dot(a, b):
    return jnp.dot(a, b, preferred_element_type=F32)


def _dot_nt(a, b):
    return lax.dot_general(a, b, (((1,), (1,)), ((), ())), preferred_element_type=F32)


def _keep_lanes(q, keep):
    return jnp.where(keep, q.astype(F32), 0.0).astype(BF16)


def _full(shape):
    n = len(shape)
    return pl.BlockSpec(shape, lambda *_: (0,) * n)


def _rows(tm, width):
    return pl.BlockSpec((tm, width), lambda i: (i, 0))


def _even_proj_kernel(x_ref, g_ref, wcq_ref, wckv_ref, wkr_ref, wkrs_ref, wdq_ref, wdk_ref, wdv_ref,
                      qn_ref, kvn_ref, wq_ref, wqs_ref, cosq_ref, sinq_ref, cosk_ref, sink_ref,
                      qext_ref, ckv_ref, kr_ref, dq_ref, dk_ref, dkb_ref, dv_ref, dvb_ref):
    hn = _rms(x_ref[...], g_ref[...]).astype(BF16)
    cq = _rms(_dot(hn, wcq_ref[...]), qn_ref[...]).astype(BF16)
    cosq = jnp.concatenate([cosq_ref[...]] * MLA_HEADS, axis=1)
    sinq = jnp.concatenate([sinq_ref[...]] * MLA_HEADS, axis=1)
    qext_ref[...] = (_dot(cq, wq_ref[...]) * cosq + _dot(cq, wqs_ref[...]) * sinq).astype(BF16)
    ckv_ref[...] = _rms(_dot(hn, wckv_ref[...]), kvn_ref[...])
    kr_ref[...] = _dot(hn, wkr_ref[...]) * cosk_ref[...] + _dot(hn, wkrs_ref[...]) * sink_ref[...]
    dq_ref[...] = (_dot(hn, wdq_ref[...]) * (DIFF_QK ** -0.5 * LOG2E)).astype(BF16)
    dk = _dot(hn, wdk_ref[...])
    dk_ref[...] = dk
    dkb_ref[...] = dk.astype(BF16)
    dv = _dot(hn, wdv_ref[...])
    dv_ref[...] = dv
    dvb = dv.astype(BF16)
    ones = jnp.ones((dvb.shape[0], DIFF_V), BF16)
    dvb_ref[...] = jnp.concatenate(
        [piece for hh in range(DIFF_HEADS) for piece in (dvb[:, hh * DIFF_V:(hh + 1) * DIFF_V], ones)], axis=1)


def _even_proj(x, g, w, tabs, tm):
    m = x.shape[0]
    ins = [x, g, w["wcq"], w["wckv"], w["wkr"], w["wkrs"], w["wdq"], w["wdk"], w["wdv"],
           w["qn"], w["kvn"], w["wq"], w["wqs"], tabs["cosq"], tabs["sinq"], tabs["cosk"], tabs["sink"]]
    row_in = {0: D_MODEL, 13: LANES, 14: LANES, 15: MLA_ROPE, 16: MLA_ROPE}
    in_specs = [_rows(tm, row_in[i]) if i in row_in else _full(a.shape) for i, a in enumerate(ins)]
    outs = [(MLA_HEADS * LANES, BF16), (MLA_KV_RANK, F32), (MLA_ROPE, F32),
            (DIFF_HEADS * DIFF_V, BF16), (DIFF_HEADS * DIFF_V, F32), (DIFF_HEADS * DIFF_V, BF16),
            (DIFF_HEADS * DIFF_V, F32), (DIFF_HEADS * 2 * DIFF_V, BF16)]
    return pl.pallas_call(
        _even_proj_kernel,
        grid=(m // tm,),
        in_specs=in_specs,
        out_specs=[_rows(tm, n) for n, _ in outs],
        out_shape=[jax.ShapeDtypeStruct((m, n), dt) for n, dt in outs],
        compiler_params=_cparams(1),
        name="even_proj",
    )(*ins)


def _odd_proj_kernel(x_ref, g_ref, w_ref, sq_ref, sk_ref, skb_ref, sv_ref, svb_ref,
                     cq_ref, ck_ref, ckb_ref, cv_ref, cvb_ref):
    hn = _rms(x_ref[...], g_ref[...]).astype(BF16)
    width = SB_HEADS * SB_DIM

    def seg(i):
        return _dot(hn, w_ref[:, i * width:(i + 1) * width])

    sq_ref[...] = (seg(0) * (SB_DIM ** -0.5)).astype(BF16)
    for i, (f_ref, b_ref) in ((1, (sk_ref, skb_ref)), (2, (sv_ref, svb_ref)),
                              (4, (ck_ref, ckb_ref)), (5, (cv_ref, cvb_ref))):
        y = seg(i)
        f_ref[...] = y
        b_ref[...] = y.astype(BF16)
    cq_ref[...] = (seg(3) * (CA_DIM ** -0.5)).astype(BF16)


def _odd_proj(x, g, w, tm):
    m = x.shape[0]
    width = SB_HEADS * SB_DIM
    dts = [BF16, F32, BF16, F32, BF16, BF16, F32, BF16, F32, BF16]
    return pl.pallas_call(
        _odd_proj_kernel,
        grid=(m // tm,),
        in_specs=[_rows(tm, D_MODEL), _full(g.shape), _full(w.shape)],
        out_specs=[_rows(tm, width) for _ in dts],
        out_shape=[jax.ShapeDtypeStruct((m, width), dt) for dt in dts],
        compiler_params=_cparams(1),
        name="odd_proj",
    )(x, g, w)


def _kv_up_kernel(ckv_ref, kr_ref, wk_ref, wv_ref, place_ref, ones_ref, kext_ref, vext_ref):
    c = ckv_ref[...].astype(BF16)
    r = kr_ref[...].astype(BF16)
    kext_ref[...] = (_dot(c, wk_ref[...]) + _dot(r, place_ref[...])).astype(BF16)
    vext_ref[...] = (_dot(c, wv_ref[...]) + ones_ref[...]).astype(BF16)


def _kv_up(ckv, kr, w, tm):
    m = ckv.shape[0]
    width = MLA_HEADS * LANES
    return pl.pallas_call(
        _kv_up_kernel,
        grid=(m // tm,),
        in_specs=[_rows(tm, MLA_KV_RANK), _rows(tm, MLA_ROPE), _full(w["wk"].shape), _full(w["wv"].shape),
                  _full(w["place"].shape), _full(w["ones"].shape)],
        out_specs=[_rows(tm, width)] * 2,
        out_shape=[jax.ShapeDtypeStruct((m, width), BF16)] * 2,
        compiler_params=_cparams(1),
        name="mla_kv_up",
    )(ckv, kr, w["wk"], w["wv"], w["place"], w["ones"])


def _attn_specs(tq, skp, width=LANES, v_width=None, kv_buffers=None):
    kw = {} if kv_buffers is None else {"pipeline_mode": pl.Buffered(kv_buffers)}
    q_spec = pl.BlockSpec((None, tq, width), lambda b, h, qi: (b, qi, h))
    k_spec = pl.BlockSpec((None, skp, width), lambda b, h, qi: (b, 0, h), **kw)
    v_spec = pl.BlockSpec((None, skp, v_width or width), lambda b, h, qi: (b, 0, h), **kw)
    return q_spec, k_spec, v_spec


def _softmax_block(s, m_ref):
    m_old = m_ref[...]
    m_new = jnp.maximum(m_old, jnp.max(s, axis=-1, keepdims=True))
    m_ref[...] = m_new
    return jnp.exp2(m_old - m_new), jnp.exp2(s - jnp.tile(m_new, (1, s.shape[1] // LANES)))


def _sweep_blocks(nb0, near_back, n_sub, n_diag, tkw, tkn, do_block):
    first = jnp.maximum(nb0 - near_back, 0)
    done = 0
    for width in tuple(tkw) + (tkn,):
        per = width // tkn
        count = (first - done) // per

        def plain(j, carry, width=width, per=per, done=done):
            do_block(pl.multiple_of((done + j * per) * tkn, width), width, None)
            return carry

        lax.fori_loop(0, count, plain, 0)
        done = done + count * per
    for c in range(-near_back, n_diag):
        kinds = []
        for r in range(n_sub):
            d = c - r
            kinds.append("skip" if d > 0 else None if d < -near_back else d + near_back)

        @pl.when(nb0 + c >= 0)
        def _special(c=c, kinds=kinds):
            do_block(pl.multiple_of((nb0 + c) * tkn, tkn), tkn, kinds)


def _mla_kernel(q_ref, k_ref, v_ref, o_ref, m_ref, acc_ref, *, hs, tqs, n_sub, n_diag, tkw, tkn, q_off, sk_real):
    qi = pl.program_id(2)
    qpos0 = q_off + qi * (tqs * n_sub)
    m_ref[...] = jnp.full(m_ref.shape, NEG, F32)
    acc_ref[...] = jnp.zeros(acc_ref.shape, F32)

    def block(start, width, kinds):
        for hh in range(hs):
            cols = slice(hh * LANES, (hh + 1) * LANES)
            k = k_ref[pl.ds(start, width), cols]
            v = v_ref[pl.ds(start, width), cols]
            for r in range(n_sub):
                kind = None if kinds is None else kinds[r]
                if kind == "skip":
                    continue
                s = _dot_nt(q_ref[r * tqs:(r + 1) * tqs, cols], k)
                if kind is not None:
                    qp = qpos0 + r * tqs + lax.broadcasted_iota(jnp.int32, (tqs, width), 0)
                    kp = start + lax.broadcasted_iota(jnp.int32, (tqs, width), 1)
                    ok = ((kp >> CHUNK_SHIFT) <= (qp >> CHUNK_SHIFT)) & (kp < sk_real)
                    s = jnp.where(ok, s, NEG)
                alpha, p = _softmax_block(s, m_ref.at[hh, r])
                acc_ref[hh, r] = acc_ref[hh, r] * alpha + _dot(p.astype(BF16), v)

    _sweep_blocks(qpos0 // tkn, 0, n_sub, n_diag, tkw, tkn, block)

    for hh in range(hs):
        for r in range(n_sub):
            acc = acc_ref[hh, r]
            lane = lax.broadcasted_iota(jnp.int32, acc.shape, 1)
            denom = jnp.sum(jnp.where(lane == MLA_V, acc, 0.0), axis=-1, keepdims=True)
            o_ref[r * tqs:(r + 1) * tqs, hh * LANES:(hh + 1) * LANES] = (
                jnp.where(lane < MLA_V, acc / denom, 0.0).astype(BF16))


def _mla_attn(qext, kext, vext, *, hs, kv_buffers, tqs, n_sub, n_diag, tkw, tkn, q_off, sk_real):
    b, sq, _ = qext.shape
    skp = kext.shape[1]
    tq = tqs * n_sub
    q_spec, k_spec, v_spec = _attn_specs(tq, skp, width=hs * LANES, kv_buffers=kv_buffers)
    kern = functools.partial(_mla_kernel, hs=hs, tqs=tqs, n_sub=n_sub, n_diag=n_diag, tkw=tkw, tkn=tkn,
                             q_off=q_off, sk_real=sk_real)
    return pl.pallas_call(
        kern,
        grid=(b, MLA_HEADS // hs, sq // tq),
        in_specs=[q_spec, k_spec, v_spec],
        out_specs=q_spec,
        out_shape=jax.ShapeDtypeStruct((b, sq, MLA_HEADS * LANES), BF16),
        scratch_shapes=[pltpu.VMEM((hs, n_sub, tqs, LANES), F32), pltpu.VMEM((hs, n_sub, tqs, LANES), F32)],
        compiler_params=_cparams(3),
        name="mla_attn",
    )(qext, kext, vext)


def _diff_kernel(t5_ref, q_ref, k_ref, v_ref, bkt_ref, lamv_ref, subln_ref, o_ref,
                 bias_ref, m_ref, acc_ref, *, hs, tqs, n_sub, n_diag, tkw, tkn, q_off, n_near, near_back, far_bucket,
                 lam_init):
    first = (pl.program_id(0) == 0) & (pl.program_id(1) == 0) & (pl.program_id(2) == 0)

    @pl.when(first)
    def _build_bias():
        for r in range(n_near):
            bkt = bkt_ref[r]
            vals = [jnp.full((tqs, tkn), NEG, F32) for _ in range(DIFF_HEADS)]
            for t in range(T5_BUCKETS):
                hit = bkt == t
                for hh in range(DIFF_HEADS):
                    vals[hh] = jnp.where(hit, (t5_ref[t, hh] - t5_ref[far_bucket, hh]) * LOG2E, vals[hh])
            for hh in range(DIFF_HEADS):
                bias_ref[r, hh] = vals[hh]

    group = pl.program_id(1)
    qi = pl.program_id(2)
    qpos0 = q_off + qi * (tqs * n_sub)
    lane = lax.broadcasted_iota(jnp.int32, (tqs * n_sub, LANES), 1)
    qm = []
    for hh in range(hs):
        q = q_ref[:, hh * LANES:(hh + 1) * LANES]
        qm.append([_keep_lanes(q, lane < DIFF_QK), _keep_lanes(q, lane >= DIFF_QK)])
    m_ref[...] = jnp.full(m_ref.shape, NEG, F32)
    acc_ref[...] = jnp.zeros(acc_ref.shape, F32)

    def block(start, width, kinds):
        for hh in range(hs):
            k = k_ref[pl.ds(start, width), hh * LANES:(hh + 1) * LANES]
            v = v_ref[pl.ds(start, width), hh * 2 * DIFF_V:(hh + 1) * 2 * DIFF_V]
            for r in range(n_sub):
                kind = None if kinds is None else kinds[r]
                if kind == "skip":
                    continue
                for mi in range(2):
                    s = _dot_nt(qm[hh][mi][r * tqs:(r + 1) * tqs, :], k)
                    if kind is not None:
                        s = s + bias_ref[kind, group * hs + hh]
                    alpha, p = _softmax_block(s, m_ref.at[hh, mi, r])
                    acc_ref[hh, mi, r] = (acc_ref[hh, mi, r] * jnp.tile(alpha, (1, 2))
                                          + _dot(p.astype(BF16), v))

    _sweep_blocks(qpos0 // tkn, near_back, n_sub, n_diag, tkw, tkn, block)

    lv = lamv_ref[...]
    lam = (jnp.exp(jnp.sum(lv[0:1] * lv[1:2], axis=-1, keepdims=True))
           - jnp.exp(jnp.sum(lv[2:3] * lv[3:4], axis=-1, keepdims=True)) + lam_init)
    for hh in range(hs):
        for r in range(n_sub):
            a0, a1 = acc_ref[hh, 0, r], acc_ref[hh, 1, r]
            o = a0[:, :DIFF_V] / a0[:, DIFF_V:] - lam * (a1[:, :DIFF_V] / a1[:, DIFF_V:])
            o_ref[r * tqs:(r + 1) * tqs, hh * LANES:(hh + 1) * LANES] = (
                _rms(o, subln_ref[...]) * (1.0 - lam_init)).astype(BF16)


def _diff_attn(t5, dq, dk, dv, bkt, lamv, subln, *, hs, tqs, n_sub, n_diag, tkw, tkn, q_off, near_back, far_bucket,
               lam_init):
    b, sq, _ = dq.shape
    skp = dk.shape[1]
    n_near = bkt.shape[0]
    tq = tqs * n_sub
    q_spec, k_spec, v_spec = _attn_specs(tq, skp, width=hs * LANES, v_width=hs * 2 * DIFF_V)
    kern = functools.partial(_diff_kernel, hs=hs, tqs=tqs, n_sub=n_sub, n_diag=n_diag, tkw=tkw, tkn=tkn,
                             q_off=q_off, n_near=n_near, near_back=near_back, far_bucket=far_bucket,
                             lam_init=lam_init)
    return pl.pallas_call(
        kern,
        grid=(b, DIFF_HEADS // hs, sq // tq),
        in_specs=[pl.BlockSpec(memory_space=pltpu.SMEM), q_spec, k_spec, v_spec,
                  _full(bkt.shape), _full(lamv.shape), _full(subln.shape)],
        out_specs=q_spec,
        out_shape=jax.ShapeDtypeStruct((b, sq, DIFF_HEADS * DIFF_V), BF16),
        scratch_shapes=[pltpu.VMEM((n_near, DIFF_HEADS, tqs, tkn), F32), pltpu.VMEM((hs, 2, n_sub, tqs, LANES), F32),
                        pltpu.VMEM((hs, 2, n_sub, tqs, 2 * DIFF_V), F32)],
        compiler_params=_cparams(3),
        name="diff_attn",
    )(t5, dq, dk, dv, bkt, lamv, subln)


def _sb_kernel(q_ref, k_ref, v_ref, tri_ref, o_ref, run_ref, acc_ref, *, hp, tq, tk, q_off, n_masked):
    qi = pl.program_id(2)
    qpos0 = q_off + qi * tq
    top = (qpos0 + tq - 1) // tk
    lane = lax.broadcasted_iota(jnp.int32, (tq, LANES), 1)
    tri = tri_ref[...]
    qhs = []
    for pp in range(hp):
        qpair = q_ref[:, pp * LANES:(pp + 1) * LANES]
        qhs.append([_keep_lanes(qpair, lane < SB_DIM), _keep_lanes(qpair, lane >= SB_DIM)])

    def block(kj, masked):
        ks = pl.multiple_of(kj * tk, tk)
        if masked:
            qp = qpos0 + lax.broadcasted_iota(jnp.int32, (tq, tk), 0)
            kp = kj * tk + lax.broadcasted_iota(jnp.int32, (tq, tk), 1)
            ok = kp < qp
        for pp in range(hp):
            cols = slice(pp * LANES, (pp + 1) * LANES)
            k = k_ref[pl.ds(ks, tk), cols]
            v = v_ref[pl.ds(ks, tk), cols]
            for hh in range(2):
                z = _dot_nt(qhs[pp][hh], k)
                sp = jnp.maximum(z, 0.0) + jnp.log(1.0 + jnp.exp2(jnp.abs(z) * -LOG2E))
                log_beta = z - sp
                if masked:
                    sp = jnp.where(ok, sp, 0.0)
                hi = sp.astype(BF16)
                lo = (sp - hi.astype(F32)).astype(BF16)
                later = _dot(hi, tri) + _dot(lo, tri)
                a = jnp.exp2(((log_beta - run_ref[pp, hh]) - later) * LOG2E)
                if masked:
                    a = jnp.where(ok, a, 0.0)
                acc_ref[pp, hh] += _dot(a.astype(BF16), v)
                run_ref[pp, hh] += jnp.sum(sp, axis=-1, keepdims=True)

    run_ref[...] = jnp.zeros(run_ref.shape, F32)
    acc_ref[...] = jnp.zeros(acc_ref.shape, F32)
    for r in range(n_masked):
        block(top - r, True)

    def cond(kj):
        return (kj >= 0) & (jnp.min(run_ref[...]) < -SB_EXIT)

    def body(kj):
        block(kj, False)
        return kj - 1

    lax.while_loop(cond, body, top - n_masked)
    for pp in range(hp):
        o_ref[:, pp * LANES:(pp + 1) * LANES] = jnp.where(lane < SB_DIM, acc_ref[pp, 0], acc_ref[pp, 1]).astype(BF16)


def _sb_attn(sq_, sk_, sv_, tri, *, hp, kv_buffers, tq, tk, q_off, n_masked):
    b, sq, _ = sq_.shape
    skp = sk_.shape[1]
    q_spec, kv_spec, _ = _attn_specs(tq, skp, width=hp * LANES, kv_buffers=kv_buffers)
    kern = functools.partial(_sb_kernel, hp=hp, tq=tq, tk=tk, q_off=q_off, n_masked=n_masked)
    return pl.pallas_call(
        kern,
        grid=(b, SB_HEADS // (2 * hp), sq // tq),
        in_specs=[q_spec, kv_spec, kv_spec, _full(tri.shape)],
        out_specs=q_spec,
        out_shape=jax.ShapeDtypeStruct((b, sq, SB_HEADS * SB_DIM), BF16),
        scratch_shapes=[pltpu.VMEM((hp, 2, tq, 1), F32), pltpu.VMEM((hp, 2, tq, LANES), F32)],
        compiler_params=_cparams(3),
        name="sb_attn",
    )(sq_, sk_, sv_, tri)


CA_ROW_GROUP = 16


def _ca_kernel(tab_ref, q_ref, k_ref, v_ref, o_ref, bias_ref, *, hp, tq, win, win_real, q_off):
    first = (pl.program_id(0) == 0) & (pl.program_id(1) == 0) & (pl.program_id(2) == 0)
    n_shift = -(-(tq + win) // LANES) * LANES
    n_rel = n_shift + LANES

    @pl.when(first)
    def _build_bias():
        x = lax.broadcasted_iota(jnp.int32, (CA_HEADS, n_rel), 1)
        idx = jnp.clip(x - (tq - 1) - CA_BAND, -CA_MAX_REL, CA_MAX_REL) + CA_MAX_REL
        f = jnp.zeros((CA_HEADS, n_rel), F32)
        for t in range(2 * CA_MAX_REL + 1):
            f = jnp.where(idx == t, tab_ref[:, t:t + 1], f)
        g = CA_ROW_GROUP
        i_loc = lax.broadcasted_iota(jnp.int32, (g, win), 0)
        j_loc = lax.broadcasted_iota(jnp.int32, (g, win), 1)
        for hh in range(CA_HEADS):
            fh = f[hh:hh + 1, :]
            shifted = jnp.concatenate([fh[:, g - 1 - bb:g - 1 - bb + n_shift] for bb in range(g)], axis=0)
            for a in range(tq // g):
                start = tq - g * a - g
                tile = shifted[:, start:start + win]
                i = i_loc + g * a
                kc = (j_loc >> CHUNK_SHIFT) - CA_LEFT_CHUNKS
                qc = i >> CHUNK_SHIFT
                ok = (kc <= qc) & (kc >= qc - CA_LEFT_CHUNKS) & (j_loc < win_real)
                bias_ref[hh, g * a:g * a + g, :] = jnp.where(ok, tile, NEG)

    group = pl.program_id(1)
    qi = pl.program_id(2)
    qpos0 = q_off + qi * tq
    ws = pl.multiple_of(qi * tq, tq)
    lane = lax.broadcasted_iota(jnp.int32, (tq, LANES), 1)
    kpos = qpos0 - CA_BAND + lax.broadcasted_iota(jnp.int32, (tq, win), 1)
    for pp in range(hp):
        cols = slice(pp * LANES, (pp + 1) * LANES)
        q = q_ref[:, cols]
        k = k_ref[pl.ds(ws, win), cols]
        v = v_ref[pl.ds(ws, win), cols]
        outs = []
        for hh in range(2):
            qh = _keep_lanes(q, (lane < CA_DIM) if hh == 0 else (lane >= CA_DIM))
            s = _dot_nt(qh, k) + bias_ref[2 * (group * hp + pp) + hh]
            s = jnp.where(kpos >= 0, s, NEG)
            m = jnp.max(s, axis=-1, keepdims=True)
            p = jnp.exp(s - m)
            denom = jnp.sum(p, axis=-1, keepdims=True)
            outs.append(_dot(p.astype(BF16), v) / denom)
        o_ref[:, cols] = jnp.where(lane < CA_DIM, outs[0], outs[1]).astype(BF16)


def _ca_attn(tab_t, cq, ck, cv, *, hp, kv_buffers, tq, win, win_real, q_off):
    b, sq, _ = cq.shape
    skp = ck.shape[1]
    q_spec, kv_spec, _ = _attn_specs(tq, skp, width=hp * LANES, kv_buffers=kv_buffers)
    kern = functools.partial(_ca_kernel, hp=hp, tq=tq, win=win, win_real=win_real, q_off=q_off)
    return pl.pallas_call(
        kern,
        grid=(b, CA_HEADS // (2 * hp), sq // tq),
        in_specs=[_full(tab_t.shape), q_spec, kv_spec, kv_spec],
        out_specs=q_spec,
        out_shape=jax.ShapeDtypeStruct((b, sq, CA_HEADS * CA_DIM), BF16),
        scratch_shapes=[pltpu.VMEM((CA_HEADS, tq, win), F32)],
        compiler_params=_cparams(3),
        name="ca_attn",
    )(tab_t, cq, ck, cv)


def _post_kernel(x_ref, oa_ref, ob_ref, wa_ref, wb_ref, g_ref, w1_ref, w2_ref, gf_ref, y_ref,
                 x1_ref, hn_ref, acc_ref, *, final_norm):
    j = pl.program_id(1)

    @pl.when(j == 0)
    def _mix():
        x1 = x_ref[...] + _dot(oa_ref[...], wa_ref[...]) + _dot(ob_ref[...], wb_ref[...])
        x1_ref[...] = x1
        hn_ref[...] = _rms(x1, g_ref[...]).astype(BF16)
        acc_ref[...] = jnp.zeros(acc_ref.shape, F32)

    a = jnp.maximum(_dot(hn_ref[...], w1_ref[...]), 0.0)
    acc_ref[...] += _dot((a * a).astype(BF16), w2_ref[...])

    @pl.when(j == pl.num_programs(1) - 1)
    def _finish():
        y = x1_ref[...] + acc_ref[...]
        if final_norm:
            y = _rms(y, gf_ref[...])
        y_ref[...] = y


def _post(x, oa, ob, wa, wb, g, w1, w2, gf, *, tm, tf, final_norm):
    m = x.shape[0]
    d_ff = w1.shape[1]
    rows = lambda width: pl.BlockSpec((tm, width), lambda i, j: (i, 0))
    return pl.pallas_call(
        functools.partial(_post_kernel, final_norm=final_norm),
        grid=(m // tm, d_ff // tf),
        in_specs=[rows(D_MODEL), rows(oa.shape[1]), rows(ob.shape[1]), _full(wa.shape), _full(wb.shape),
                  _full(g.shape), pl.BlockSpec((D_MODEL, tf), lambda i, j: (0, j)),
                  pl.BlockSpec((tf, D_MODEL), lambda i, j: (j, 0)), _full(gf.shape)],
        out_specs=rows(D_MODEL),
        out_shape=jax.ShapeDtypeStruct((m, D_MODEL), F32),
        scratch_shapes=[pltpu.VMEM((tm, D_MODEL), F32), pltpu.VMEM((tm, D_MODEL), BF16),
                        pltpu.VMEM((tm, D_MODEL), F32)],
        compiler_params=_cparams(2),
        name="post_mlp",
    )(x, oa, ob, wa, wb, g, w1, w2, gf)


def _t5_bucket(rel):
    nb = T5_BUCKETS // 2
    max_exact = nb // 2
    ret = jnp.where(rel > 0, nb, 0)
    n = jnp.abs(rel)
    nf = jnp.maximum(n, 1).astype(F32)
    large = max_exact + (jnp.log(nf / max_exact) / math.log(T5_MAX_DIST / max_exact) * (nb - max_exact)).astype(jnp.int32)
    large = jnp.minimum(large, nb - 1)
    return ret + jnp.where(n < max_exact, n, large)


def _rope_tables(pos):
    half = MLA_ROPE // 2
    inv = ROPE_BASE ** (-jnp.arange(half, dtype=F32) / half)
    ang = pos.astype(F32)[:, None] * inv[None, :]
    cos, sin = jnp.cos(ang), jnp.sin(ang)
    cosk = jnp.concatenate([cos, cos], axis=1)
    sink = jnp.concatenate([-sin, sin], axis=1)
    n = pos.shape[0]
    pad = jnp.zeros((n, LANES - MLA_NOPE - MLA_ROPE), F32)
    qscale = (MLA_NOPE + MLA_ROPE) ** -0.5 * LOG2E
    cosq = jnp.concatenate([jnp.ones((n, MLA_NOPE), F32), cosk, pad], axis=1) * qscale
    sinq = jnp.concatenate([jnp.zeros((n, MLA_NOPE), F32), sink, pad], axis=1) * qscale
    return {"cosq": cosq, "sinq": sinq, "cosk": cosk, "sink": sink}


def _swap_halves(w):
    half = w.shape[-1] // 2
    return jnp.concatenate([w[..., half:], w[..., :half]], axis=-1)


def _even_weights(w_in, q_norm, kv_norm, w_uq, w_ukv, w_out):
    sizes = [MLA_Q_RANK, MLA_KV_RANK, MLA_ROPE, DIFF_HEADS * 2 * DIFF_QK, DIFF_HEADS * 2 * DIFF_QK,
             DIFF_HEADS * DIFF_V]
    offs = np.cumsum([0] + sizes)
    wcq, wckv, wkr, wdq, wdk, wdv = (w_in[:, offs[i]:offs[i + 1]].astype(BF16) for i in range(6))
    uq = w_uq.reshape(MLA_Q_RANK, MLA_HEADS, MLA_NOPE + MLA_ROPE)
    zq = jnp.zeros((MLA_Q_RANK, MLA_HEADS, LANES - MLA_NOPE - MLA_ROPE), F32)
    wq = jnp.concatenate([uq, zq], axis=-1)
    wqs = jnp.concatenate([jnp.zeros_like(uq[..., :MLA_NOPE]), _swap_halves(uq[..., MLA_NOPE:]), zq], axis=-1)
    ukv = w_ukv.reshape(MLA_KV_RANK, MLA_HEADS, MLA_NOPE + MLA_V)
    zk = jnp.zeros((MLA_KV_RANK, MLA_HEADS, LANES - MLA_NOPE), F32)
    wk = jnp.concatenate([ukv[..., :MLA_NOPE], zk], axis=-1)
    wv = jnp.concatenate([ukv[..., MLA_NOPE:], jnp.zeros((MLA_KV_RANK, MLA_HEADS, LANES - MLA_V), F32)], axis=-1)
    place = np.zeros((MLA_ROPE, MLA_HEADS, LANES), np.float32)
    ones = np.zeros((1, MLA_HEADS, LANES), np.float32)
    for hh in range(MLA_HEADS):
        place[np.arange(MLA_ROPE), hh, MLA_NOPE + np.arange(MLA_ROPE)] = 1.0
        ones[0, hh, MLA_V] = 1.0
    flat = lambda a: a.reshape(a.shape[0], MLA_HEADS * LANES)
    wo_mla = w_out[:MLA_HEADS * MLA_V].reshape(MLA_HEADS, MLA_V, D_MODEL)
    wo_mla = jnp.concatenate([wo_mla, jnp.zeros((MLA_HEADS, LANES - MLA_V, D_MODEL), F32)], axis=1)
    return {
        "wcq": wcq, "wckv": wckv, "wkr": wkr, "wkrs": _swap_halves(wkr), "wdq": wdq, "wdk": wdk, "wdv": wdv,
        "qn": q_norm.reshape(1, -1), "kvn": kv_norm.reshape(1, -1),
        "wq": flat(wq).astype(BF16), "wqs": flat(wqs).astype(BF16),
        "wk": flat(wk).astype(BF16), "wv": flat(wv).astype(BF16),
        "place": jnp.asarray(flat(place), BF16), "ones": jnp.asarray(flat(ones), F32),
        "wo_mla": wo_mla.reshape(MLA_HEADS * LANES, D_MODEL).astype(BF16),
        "wo_diff": w_out[MLA_HEADS * MLA_V:].astype(BF16),
    }


def _pad_rows(a, total, front=0):
    back = total - front - a.shape[1]
    return jnp.pad(a, ((0, 0), (front, back), (0, 0)))


def _round_up(n, mult):
    return -(-n // mult) * mult


def _diff_buckets(tq, tk, q_off, sk_real, skp):
    near_back = -((q_off - (T5_MAX_DIST - 1)) // tk - q_off // tk)
    last = (_round_up(q_off + tq, CHUNK) - 1) // tk
    last = min(last, skp // tk - 1)
    n_near = last - (q_off // tk - near_back) + 1
    i = np.arange(tq)[:, None]
    mats = []
    for r in range(n_near):
        kp = (q_off // tk - near_back + r) * tk + np.arange(tk)[None, :]
        qp = q_off + i
        ok = ((kp >> CHUNK_SHIFT) <= (qp >> CHUNK_SHIFT)) & (kp < sk_real)
        bkt = _t5_bucket(jnp.asarray(kp - qp, jnp.int32))
        mats.append(jnp.where(jnp.asarray(ok), bkt, -1))
    return jnp.stack(mats).astype(jnp.int32), near_back


def _trunk(x, q_off, caches, prm, cfg):
    b, sq, _ = x.shape
    m = b * sq
    tq, tk, tm, tf = cfg["tq"], cfg["tk"], cfg["tm"], cfg["tf"]
    sk_real = q_off + sq
    skp = _round_up(sk_real, tk)
    pos = q_off + jnp.arange(sq, dtype=jnp.int32)
    tabs = {k: jnp.tile(v, (b, 1)) for k, v in _rope_tables(pos).items()}
    x2 = x.reshape(m, D_MODEL)

    def with_past(past, new, dtype):
        new = new.reshape(b, sq, -1)
        if past is None:
            return new.astype(dtype)
        return jnp.concatenate([past.reshape(b, past.shape[1], -1).astype(dtype), new.astype(dtype)], axis=1)

    ew = prm["even"]
    qext, ckv, kr, dq, dk, dkb, dv, dvb = _even_proj(x2, prm["norm_mix"][0:1], ew, tabs, tm)
    past = (None,) * 4 if caches is None else tuple(c[0] for c in caches[:4])
    ckv_all = _pad_rows(with_past(past[0], ckv, F32), skp)
    kr_all = _pad_rows(with_past(past[1], kr, F32), skp)
    kext, vext = _kv_up(ckv_all.reshape(b * skp, -1), kr_all.reshape(b * skp, -1), ew, cfg["tm_kv"])
    kext = kext.reshape(b, skp, -1)
    vext = vext.reshape(b, skp, -1)
    tqs, n_sub = cfg["tqs"], cfg["n_sub"]
    assert n_sub == 1 or tqs == tk
    n_diag = (_round_up(q_off + tqs * n_sub, CHUNK) - 1) // tk - q_off // tk + 1
    tiles = dict(tqs=tqs, n_sub=n_sub, n_diag=n_diag, tkw=cfg["tkw"], tkn=tk, q_off=q_off)
    o_mla = _mla_attn(qext.reshape(b, sq, -1), kext, vext, hs=cfg["mla_heads"], kv_buffers=cfg["kv_buffers"],
                      sk_real=sk_real, **tiles)
    dk_all = _pad_rows(with_past(past[2], dkb, BF16), skp)
    past_dv = past[3]
    if past_dv is not None:
        past_dv = jnp.concatenate([past_dv.astype(BF16), jnp.ones(past_dv.shape, BF16)], axis=-1)
    dv_all = _pad_rows(with_past(past_dv, dvb, BF16), skp)
    bkt, near_back = _diff_buckets(tqs, tk, q_off, sk_real, skp)
    assert bkt.shape[0] == near_back + n_diag - (n_sub - 1)
    lam_init = 0.8 - 0.6 * math.exp(-0.3 * 0)
    o_diff = _diff_attn(prm["t5"], dq.reshape(b, sq, -1), dk_all, dv_all, bkt, prm["lam_vecs"], prm["subln"],
                        hs=cfg["diff_heads"], near_back=near_back, far_bucket=T5_BUCKETS // 2 - 1,
                        lam_init=lam_init, **tiles)
    x2 = _post(x2, o_mla.reshape(m, -1), o_diff.reshape(m, -1), ew["wo_mla"], ew["wo_diff"],
               prm["norm_ff"][0:1], prm["w_ff1"][0], prm["w_ff2"][0], prm["final_norm"],
               tm=cfg["tm_post"], tf=tf, final_norm=False)
    new_even = (ckv.reshape(1, b, sq, MLA_KV_RANK), kr.reshape(1, b, sq, MLA_ROPE),
                dk.reshape(1, b, sq, DIFF_HEADS, 2 * DIFF_QK), dv.reshape(1, b, sq, DIFF_HEADS, DIFF_V))

    sq_, sk_, skb, sv_, svb, cq, ck, ckb, cv, cvb = _odd_proj(x2, prm["norm_mix"][1:2], prm["w_in_odd"], tm)
    past = (None,) * 4 if caches is None else tuple(c[0] for c in caches[4:])
    sk_all = _pad_rows(with_past(past[0], skb, BF16), skp)
    sv_all = _pad_rows(with_past(past[1], svb, BF16), skp)
    n_masked = (q_off + tq - 1) // tk - q_off // tk + 1
    pairs = dict(hp=cfg["head_pairs"], kv_buffers=cfg["kv_buffers"])
    o_sb = _sb_attn(sq_.reshape(b, sq, -1), sk_all, sv_all, prm["tri"][tk], tq=tq, tk=tk, q_off=q_off,
                    n_masked=n_masked, **pairs)
    win_real = tq + CA_BAND
    win = _round_up(win_real, LANES)
    if caches is None:
        ck_all = _pad_rows(ckb.reshape(b, sq, -1), sq + CA_BAND + win - win_real, front=CA_BAND)
        cv_all = _pad_rows(cvb.reshape(b, sq, -1), sq + CA_BAND + win - win_real, front=CA_BAND)
    else:
        ck_all = _pad_rows(with_past(past[2], ckb, BF16), win)
        cv_all = _pad_rows(with_past(past[3], cvb, BF16), win)
    o_ca = _ca_attn(prm["ca_tab_t"], cq.reshape(b, sq, -1), ck_all, cv_all, tq=tq, win=win, win_real=win_real,
                    q_off=q_off, **pairs)
    x2 = _post(x2, o_sb.reshape(m, -1), o_ca.reshape(m, -1), prm["wo_sb"], prm["wo_ca"],
               prm["norm_ff"][1:2], prm["w_ff1"][1], prm["w_ff2"][1], prm["final_norm"],
               tm=cfg["tm_post"], tf=tf, final_norm=True)

    heads = lambda a: a.reshape(b, sq, SB_HEADS, SB_DIM)
    if caches is None:
        nb = min(CA_BAND, sq)
        cak, cav = heads(ck)[:, sq - nb:], heads(cv)[:, sq - nb:]
    else:
        nb = past[2].shape[1]
        cak = jnp.concatenate([past[2], heads(ck)], axis=1)[:, sq:]
        cav = jnp.concatenate([past[3], heads(cv)], axis=1)[:, sq:]
        assert cak.shape[1] == nb
    new_odd = (heads(sk_)[None], heads(sv_)[None], cak[None], cav[None])
    return x2.reshape(b, sq, D_MODEL), new_even + new_odd


def _tri(tk):
    j = np.arange(tk)[:, None]
    s = np.arange(tk)[None, :]
    return jnp.asarray((j > s).astype(np.float32), BF16)


def kernel(x_prompt, x_sample, cache_mla_ckv, cache_mla_krope, cache_diff_k, cache_diff_v, cache_sb_k, cache_sb_v, cache_ca_k, cache_ca_v, norm_mix, norm_ff, w_in_even, mla_q_norm, mla_kv_norm, mla_w_uq, mla_w_ukv, diff_lambda_vecs, diff_subln, t5_bias, w_out_even, w_in_odd, ca_rel_bias, w_out_odd, w_ff1, w_ff2, final_norm):
    seq = x_prompt.shape[1]
    dec_seq = x_sample.shape[1]
    past_len = cache_mla_ckv.shape[2]
    assert cache_ca_k.shape[2] == CA_BAND and past_len % CHUNK == 0

    cfg_p = {"tq": 256, "tqs": 256, "n_sub": 4, "tk": 256, "tkw": (2048, 1024), "mla_heads": 2, "diff_heads": 1,
             "head_pairs": 2, "kv_buffers": 1,
             "tm": min(512, seq), "tm_kv": min(512, seq), "tm_post": min(1024, seq), "tf": 512}
    rows_s = x_sample.shape[0] * dec_seq
    cfg_s = {"tq": dec_seq, "tqs": dec_seq, "n_sub": 1, "tk": 128, "tkw": (512,), "mla_heads": MLA_HEADS,
             "diff_heads": DIFF_HEADS, "head_pairs": SB_HEADS // 2, "kv_buffers": None, "tm": rows_s, "tm_kv": x_sample.shape[0] * 128 // 2, "tm_post": rows_s,
             "tf": 512}
    n_sb = SB_HEADS * SB_DIM
    prm = {
        "norm_mix": norm_mix, "norm_ff": norm_ff, "final_norm": final_norm.reshape(1, -1),
        "even": _even_weights(w_in_even[0], mla_q_norm[0], mla_kv_norm[0], mla_w_uq[0], mla_w_ukv[0],
                              w_out_even[0]),
        "t5": t5_bias, "lam_vecs": diff_lambda_vecs[0], "subln": diff_subln[0].reshape(1, -1),
        "w_in_odd": w_in_odd[0].astype(BF16), "ca_tab_t": ca_rel_bias[0].T,
        "wo_sb": w_out_odd[0][:n_sb].astype(BF16), "wo_ca": w_out_odd[0][n_sb:].astype(BF16),
        "w_ff1": w_ff1.astype(BF16), "w_ff2": w_ff2.astype(BF16),
        "tri": {tk: _tri(tk) for tk in {cfg_p["tk"], cfg_s["tk"]}},
    }
    y_prompt, new_p = _trunk(x_prompt, 0, None, prm, cfg_p)
    caches = (cache_mla_ckv, cache_mla_krope, cache_diff_k, cache_diff_v,
              cache_sb_k, cache_sb_v, cache_ca_k, cache_ca_v)
    y_sample, new_s = _trunk(x_sample, past_len, caches, prm, cfg_s)
    return (y_prompt, y_sample) + tuple(new_p) + tuple(new_s)
```

```python
import functools
import math

import numpy as np
import jax
import jax.numpy as jnp
from jax import lax
from jax.experimental import pallas as pl
from jax.experimental.pallas import tpu as pltpu

F32 = jnp.float32
BF16 = jnp.bfloat16

D_MODEL = 1024
CHUNK = 64
CHUNK_SHIFT = 6
EPS = 1e-6
NEG = -1e30

MLA_HEADS = 8
MLA_Q_RANK = 256
MLA_KV_RANK = 128
MLA_NOPE = 64
MLA_ROPE = 32
MLA_V = 64
ROPE_BASE = 10000.0
DIFF_HEADS = 4
DIFF_QK = 64
DIFF_V = 2 * DIFF_QK
T5_BUCKETS = 32
T5_MAX_DIST = 128
SB_HEADS = 8
SB_DIM = 64
CA_HEADS = 8
CA_DIM = 64
CA_LEFT_CHUNKS = 8
CA_BAND = CA_LEFT_CHUNKS * CHUNK
CA_MAX_REL = 128

LANES = 128
VMEM_LIMIT = 48 * 1024 * 1024
LOG2E = math.log2(math.e)
SB_EXIT = -104.0


def _cparams(n_axes):
    return pltpu.CompilerParams(dimension_semantics=("arbitrary",) * n_axes,
                                vmem_limit_bytes=VMEM_LIMIT)


def _rms(x, g):
    return x * lax.rsqrt(jnp.mean(x * x, axis=-1, keepdims=True) + EPS) * g


def _dot(a, b):
    return jnp.dot(a, b, preferred_element_type=F32)


def _dot_nt(a, b):
    return lax.dot_general(a, b, (((1,), (1,)), ((), ())), preferred_element_type=F32)


def _keep_lanes(q, keep):
    return jnp.where(keep, q.astype(F32), 0.0).astype(BF16)


def _full(shape):
    n = len(shape)
    return pl.BlockSpec(shape, lambda *_: (0,) * n)


def _rows(tm, width):
    return pl.BlockSpec((tm, width), lambda i: (i, 0))


def _even_proj_kernel(x_ref, g_ref, wcq_ref, wckv_ref, wkr_ref, wkrs_ref, wdq_ref, wdk_ref, wdv_ref,
                      qn_ref, kvn_ref, wq_ref, wqs_ref, cosq_ref, sinq_ref, cosk_ref, sink_ref,
                      qext_ref, ckv_ref, kr_ref, dq_ref, dk_ref, dkb_ref, dv_ref, dvb_ref):
    hn = _rms(x_ref[...], g_ref[...]).astype(BF16)
    cq = _rms(_dot(hn, wcq_ref[...]), qn_ref[...]).astype(BF16)
    cosq = jnp.concatenate([cosq_ref[...]] * MLA_HEADS, axis=1)
    sinq = jnp.concatenate([sinq_ref[...]] * MLA_HEADS, axis=1)
    qext_ref[...] = (_dot(cq, wq_ref[...]) * cosq + _dot(cq, wqs_ref[...]) * sinq).astype(BF16)
    ckv_ref[...] = _rms(_dot(hn, wckv_ref[...]), kvn_ref[...])
    kr_ref[...] = _dot(hn, wkr_ref[...]) * cosk_ref[...] + _dot(hn, wkrs_ref[...]) * sink_ref[...]
    dq_ref[...] = (_dot(hn, wdq_ref[...]) * (DIFF_QK ** -0.5 * LOG2E)).astype(BF16)
    dk = _dot(hn, wdk_ref[...])
    dk_ref[...] = dk
    dkb_ref[...] = dk.astype(BF16)
    dv = _dot(hn, wdv_ref[...])
    dv_ref[...] = dv
    dvb = dv.astype(BF16)
    ones = jnp.ones((dvb.shape[0], DIFF_V), BF16)
    dvb_ref[...] = jnp.concatenate(
        [piece for hh in range(DIFF_HEADS) for piece in (dvb[:, hh * DIFF_V:(hh + 1) * DIFF_V], ones)], axis=1)


def _even_proj(x, g, w, tabs, tm):
    m = x.shape[0]
    ins = [x, g, w["wcq"], w["wckv"], w["wkr"], w["wkrs"], w["wdq"], w["wdk"], w["wdv"],
           w["qn"], w["kvn"], w["wq"], w["wqs"], tabs["cosq"], tabs["sinq"], tabs["cosk"], tabs["sink"]]
    row_in = {0: D_MODEL, 13: LANES, 14: LANES, 15: MLA_ROPE, 16: MLA_ROPE}
    in_specs = [_rows(tm, row_in[i]) if i in row_in else _full(a.shape) for i, a in enumerate(ins)]
    outs = [(MLA_HEADS * LANES, BF16), (MLA_KV_RANK, F32), (MLA_ROPE, F32),
            (DIFF_HEADS * DIFF_V, BF16), (DIFF_HEADS * DIFF_V, F32), (DIFF_HEADS * DIFF_V, BF16),
            (DIFF_HEADS * DIFF_V, F32), (DIFF_HEADS * 2 * DIFF_V, BF16)]
    return pl.pallas_call(
        _even_proj_kernel,
        grid=(m // tm,),
        in_specs=in_specs,
        out_specs=[_rows(tm, n) for n, _ in outs],
        out_shape=[jax.ShapeDtypeStruct((m, n), dt) for n, dt in outs],
        compiler_params=_cparams(1),
        name="even_proj",
    )(*ins)


def _odd_proj_kernel(x_ref, g_ref, w_ref, sq_ref, sk_ref, skb_ref, sv_ref, svb_ref,
                     cq_ref, ck_ref, ckb_ref, cv_ref, cvb_ref):
    hn = _rms(x_ref[...], g_ref[...]).astype(BF16)
    width = SB_HEADS * SB_DIM

    def seg(i):
        return _dot(hn, w_ref[:, i * width:(i + 1) * width])

    sq_ref[...] = (seg(0) * (SB_DIM ** -0.5)).astype(BF16)
    for i, (f_ref, b_ref) in ((1, (sk_ref, skb_ref)), (2, (sv_ref, svb_ref)),
                              (4, (ck_ref, ckb_ref)), (5, (cv_ref, cvb_ref))):
        y = seg(i)
        f_ref[...] = y
        b_ref[...] = y.astype(BF16)
    cq_ref[...] = (seg(3) * (CA_DIM ** -0.5)).astype(BF16)


def _odd_proj(x, g, w, tm):
    m = x.shape[0]
    width = SB_HEADS * SB_DIM
    dts = [BF16, F32, BF16, F32, BF16, BF16, F32, BF16, F32, BF16]
    return pl.pallas_call(
        _odd_proj_kernel,
        grid=(m // tm,),
        in_specs=[_rows(tm, D_MODEL), _full(g.shape), _full(w.shape)],
        out_specs=[_rows(tm, width) for _ in dts],
        out_shape=[jax.ShapeDtypeStruct((m, width), dt) for dt in dts],
        compiler_params=_cparams(1),
        name="odd_proj",
    )(x, g, w)


def _kv_up_kernel(ckv_ref, kr_ref, wk_ref, wv_ref, place_ref, ones_ref, kext_ref, vext_ref):
    c = ckv_ref[...].astype(BF16)
    r = kr_ref[...].astype(BF16)
    kext_ref[...] = (_dot(c, wk_ref[...]) + _dot(r, place_ref[...])).astype(BF16)
    vext_ref[...] = (_dot(c, wv_ref[...]) + ones_ref[...]).astype(BF16)


def _kv_up(ckv, kr, w, tm):
    m = ckv.shape[0]
    width = MLA_HEADS * LANES
    return pl.pallas_call(
        _kv_up_kernel,
        grid=(m // tm,),
        in_specs=[_rows(tm, MLA_KV_RANK), _rows(tm, MLA_ROPE), _full(w["wk"].shape), _full(w["wv"].shape),
                  _full(w["place"].shape), _full(w["ones"].shape)],
        out_specs=[_rows(tm, width)] * 2,
        out_shape=[jax.ShapeDtypeStruct((m, width), BF16)] * 2,
        compiler_params=_cparams(1),
        name="mla_kv_up",
    )(ckv, kr, w["wk"], w["wv"], w["place"], w["ones"])


def _attn_specs(tq, skp, width=LANES, v_width=None, kv_buffers=None):
    kw = {} if kv_buffers is None else {"pipeline_mode": pl.Buffered(kv_buffers)}
    q_spec = pl.BlockSpec((None, tq, width), lambda b, h, qi: (b, qi, h))
    k_spec = pl.BlockSpec((None, skp, width), lambda b, h, qi: (b, 0, h), **kw)
    v_spec = pl.BlockSpec((None, skp, v_width or width), lambda b, h, qi: (b, 0, h), **kw)
    return q_spec, k_spec, v_spec


def _softmax_block(s, m_ref):
    m_old = m_ref[...]
    m_new = jnp.maximum(m_old, jnp.max(s, axis=-1, keepdims=True))
    m_ref[...] = m_new
    return jnp.exp2(m_old - m_new), jnp.exp2(s - jnp.tile(m_new, (1, s.shape[1] // LANES)))


def _sweep_blocks(nb0, near_back, n_sub, n_diag, tkw, tkn, do_block):
    first = jnp.maximum(nb0 - near_back, 0)
    done = 0
    for width in tuple(tkw) + (tkn,):
        per = width // tkn
        count = (first - done) // per

        def plain(j, carry, width=width, per=per, done=done):
            do_block(pl.multiple_of((done + j * per) * tkn, width), width, None)
            return carry

        lax.fori_loop(0, count, plain, 0)
        done = done + count * per
    for c in range(-near_back, n_diag):
        kinds = []
        for r in range(n_sub):
            d = c - r
            kinds.append("skip" if d > 0 else None if d < -near_back else d + near_back)

        @pl.when(nb0 + c >= 0)
        def _special(c=c, kinds=kinds):
            do_block(pl.multiple_of((nb0 + c) * tkn, tkn), tkn, kinds)


def _mla_kernel(q_ref, k_ref, v_ref, o_ref, m_ref, acc_ref, *, hs, tqs, n_sub, n_diag, tkw, tkn, q_off, sk_real):
    qi = pl.program_id(2)
    qpos0 = q_off + qi * (tqs * n_sub)
    m_ref[...] = jnp.full(m_ref.shape, NEG, F32)
    acc_ref[...] = jnp.zeros(acc_ref.shape, F32)

    def block(start, width, kinds):
        for hh in range(hs):
            cols = slice(hh * LANES, (hh + 1) * LANES)
            k = k_ref[pl.ds(start, width), cols]
            v = v_ref[pl.ds(start, width), cols]
            for r in range(n_sub):
                kind = None if kinds is None else kinds[r]
                if kind == "skip":
                    continue
                s = _dot_nt(q_ref[r * tqs:(r + 1) * tqs, cols], k)
                if kind is not None:
                    qp = qpos0 + r * tqs + lax.broadcasted_iota(jnp.int32, (tqs, width), 0)
                    kp = start + lax.broadcasted_iota(jnp.int32, (tqs, width), 1)
                    ok = ((kp >> CHUNK_SHIFT) <= (qp >> CHUNK_SHIFT)) & (kp < sk_real)
                    s = jnp.where(ok, s, NEG)
                alpha, p = _softmax_block(s, m_ref.at[hh, r])
                acc_ref[hh, r] = acc_ref[hh, r] * alpha + _dot(p.astype(BF16), v)

    _sweep_blocks(qpos0 // tkn, 0, n_sub, n_diag, tkw, tkn, block)

    for hh in range(hs):
        for r in range(n_sub):
            acc = acc_ref[hh, r]
            lane = lax.broadcasted_iota(jnp.int32, acc.shape, 1)
            denom = jnp.sum(jnp.where(lane == MLA_V, acc, 0.0), axis=-1, keepdims=True)
            o_ref[r * tqs:(r + 1) * tqs, hh * LANES:(hh + 1) * LANES] = (
                jnp.where(lane < MLA_V, acc / denom, 0.0).astype(BF16))


def _mla_attn(qext, kext, vext, *, hs, kv_buffers, tqs, n_sub, n_diag, tkw, tkn, q_off, sk_real):
    b, sq, _ = qext.shape
    skp = kext.shape[1]
    tq = tqs * n_sub
    q_spec, k_spec, v_spec = _attn_specs(tq, skp, width=hs * LANES, kv_buffers=kv_buffers)
    kern = functools.partial(_mla_kernel, hs=hs, tqs=tqs, n_sub=n_sub, n_diag=n_diag, tkw=tkw, tkn=tkn,
                             q_off=q_off, sk_real=sk_real)
    return pl.pallas_call(
        kern,
        grid=(b, MLA_HEADS // hs, sq // tq),
        in_specs=[q_spec, k_spec, v_spec],
        out_specs=q_spec,
        out_shape=jax.ShapeDtypeStruct((b, sq, MLA_HEADS * LANES), BF16),
        scratch_shapes=[pltpu.VMEM((hs, n_sub, tqs, LANES), F32), pltpu.VMEM((hs, n_sub, tqs, LANES), F32)],
        compiler_params=_cparams(3),
        name="mla_attn",
    )(qext, kext, vext)


def _diff_kernel(t5_ref, q_ref, k_ref, v_ref, bkt_ref, lamv_ref, subln_ref, o_ref,
                 bias_ref, m_ref, acc_ref, *, hs, tqs, n_sub, n_diag, tkw, tkn, q_off, n_near, near_back, far_bucket,
                 lam_init):
    first = (pl.program_id(0) == 0) & (pl.program_id(1) == 0) & (pl.program_id(2) == 0)

    @pl.when(first)
    def _build_bias():
        for r in range(n_near):
            bkt = bkt_ref[r]
            vals = [jnp.full((tqs, tkn), NEG, F32) for _ in range(DIFF_HEADS)]
            for t in range(T5_BUCKETS):
                hit = bkt == t
                for hh in range(DIFF_HEADS):
                    vals[hh] = jnp.where(hit, (t5_ref[t, hh] - t5_ref[far_bucket, hh]) * LOG2E, vals[hh])
            for hh in range(DIFF_HEADS):
                bias_ref[r, hh] = vals[hh]

    group = pl.program_id(1)
    qi = pl.program_id(2)
    qpos0 = q_off + qi * (tqs * n_sub)
    lane = lax.broadcasted_iota(jnp.int32, (tqs * n_sub, LANES), 1)
    qm = []
    for hh in range(hs):
        q = q_ref[:, hh * LANES:(hh + 1) * LANES]
        qm.append([_keep_lanes(q, lane < DIFF_QK), _keep_lanes(q, lane >= DIFF_QK)])
    m_ref[...] = jnp.full(m_ref.shape, NEG, F32)
    acc_ref[...] = jnp.zeros(acc_ref.shape, F32)

    def block(start, width, kinds):
        for hh in range(hs):
            k = k_ref[pl.ds(start, width), hh * LANES:(hh + 1) * LANES]
            v = v_ref[pl.ds(start, width), hh * 2 * DIFF_V:(hh + 1) * 2 * DIFF_V]
            for r in range(n_sub):
                kind = None if kinds is None else kinds[r]
                if kind == "skip":
                    continue
                for mi in range(2):
                    s = _dot_nt(qm[hh][mi][r * tqs:(r + 1) * tqs, :], k)
                    if kind is not None:
                        s = s + bias_ref[kind, group * hs + hh]
                    alpha, p = _softmax_block(s, m_ref.at[hh, mi, r])
                    acc_ref[hh, mi, r] = (acc_ref[hh, mi, r] * jnp.tile(alpha, (1, 2))
                                          + _dot(p.astype(BF16), v))

    _sweep_blocks(qpos0 // tkn, near_back, n_sub, n_diag, tkw, tkn, block)

    lv = lamv_ref[...]
    lam = (jnp.exp(jnp.sum(lv[0:1] * lv[1:2], axis=-1, keepdims=True))
           - jnp.exp(jnp.sum(lv[2:3] * lv[3:4], axis=-1, keepdims=True)) + lam_init)
    for hh in range(hs):
        for r in range(n_sub):
            a0, a1 = acc_ref[hh, 0, r], acc_ref[hh, 1, r]
            o = a0[:, :DIFF_V] / a0[:, DIFF_V:] - lam * (a1[:, :DIFF_V] / a1[:, DIFF_V:])
            o_ref[r * tqs:(r + 1) * tqs, hh * LANES:(hh + 1) * LANES] = (
                _rms(o, subln_ref[...]) * (1.0 - lam_init)).astype(BF16)


def _diff_attn(t5, dq, dk, dv, bkt, lamv, subln, *, hs, tqs, n_sub, n_diag, tkw, tkn, q_off, near_back, far_bucket,
               lam_init):
    b, sq, _ = dq.shape
    skp = dk.shape[1]
    n_near = bkt.shape[0]
    tq = tqs * n_sub
    q_spec, k_spec, v_spec = _attn_specs(tq, skp, width=hs * LANES, v_width=hs * 2 * DIFF_V)
    kern = functools.partial(_diff_kernel, hs=hs, tqs=tqs, n_sub=n_sub, n_diag=n_diag, tkw=tkw, tkn=tkn,
                             q_off=q_off, n_near=n_near, near_back=near_back, far_bucket=far_bucket,
                             lam_init=lam_init)
    return pl.pallas_call(
        kern,
        grid=(b, DIFF_HEADS // hs, sq // tq),
        in_specs=[pl.BlockSpec(memory_space=pltpu.SMEM), q_spec, k_spec, v_spec,
                  _full(bkt.shape), _full(lamv.shape), _full(subln.shape)],
        out_specs=q_spec,
        out_shape=jax.ShapeDtypeStruct((b, sq, DIFF_HEADS * DIFF_V), BF16),
        scratch_shapes=[pltpu.VMEM((n_near, DIFF_HEADS, tqs, tkn), F32), pltpu.VMEM((hs, 2, n_sub, tqs, LANES), F32),
                        pltpu.VMEM((hs, 2, n_sub, tqs, 2 * DIFF_V), F32)],
        compiler_params=_cparams(3),
        name="diff_attn",
    )(t5, dq, dk, dv, bkt, lamv, subln)


def _sb_kernel(q_ref, k_ref, v_ref, tri_ref, o_ref, run_ref, acc_ref, *, hp, tq, tk, q_off, n_masked):
    qi = pl.program_id(2)
    qpos0 = q_off + qi * tq
    top = (qpos0 + tq - 1) // tk
    lane = lax.broadcasted_iota(jnp.int32, (tq, LANES), 1)
    tri = tri_ref[...]
    qhs = []
    for pp in range(hp):
        qpair = q_ref[:, pp * LANES:(pp + 1) * LANES]
        qhs.append([_keep_lanes(qpair, lane < SB_DIM), _keep_lanes(qpair, lane >= SB_DIM)])

    def block(kj, masked, valid=None):
        if valid is not None:
            kj = jnp.maximum(kj, 0)
        ks = pl.multiple_of(kj * tk, tk)
        if masked:
            qp = qpos0 + lax.broadcasted_iota(jnp.int32, (tq, tk), 0)
            kp = kj * tk + lax.broadcasted_iota(jnp.int32, (tq, tk), 1)
            ok = kp < qp
        for pp in range(hp):
            cols = slice(pp * LANES, (pp + 1) * LANES)
            k = k_ref[pl.ds(ks, tk), cols]
            v = v_ref[pl.ds(ks, tk), cols]
            for hh in range(2):
                z = _dot_nt(qhs[pp][hh], k)
                sp = jnp.maximum(z, 0.0) + jnp.log(1.0 + jnp.exp2(jnp.abs(z) * -LOG2E))
                log_beta = z - sp
                if masked:
                    sp = jnp.where(ok, sp, 0.0)
                if valid is not None:
                    sp = jnp.where(valid, sp, 0.0)
                later = _dot(sp.astype(BF16), tri)
                a = jnp.exp2(((log_beta - run_ref[pp, hh]) - later) * LOG2E)
                if masked:
                    a = jnp.where(ok, a, 0.0)
                if valid is not None:
                    a = jnp.where(valid, a, 0.0)
                acc_ref[pp, hh] += _dot(a.astype(BF16), v)
                run_ref[pp, hh] += jnp.sum(sp, axis=-1, keepdims=True)

    run_ref[...] = jnp.zeros(run_ref.shape, F32)
    acc_ref[...] = jnp.zeros(acc_ref.shape, F32)
    for r in range(n_masked):
        block(top - r, True)
    nxt = top - n_masked
    block(nxt, False, valid=nxt >= 0)

    def cond(kj):
        return (kj >= 0) & (jnp.min(run_ref[...]) < -SB_EXIT)

    def body(kj):
        block(kj, False)
        return kj - 1

    lax.while_loop(cond, body, nxt - 1)
    for pp in range(hp):
        o_ref[:, pp * LANES:(pp + 1) * LANES] = jnp.where(lane < SB_DIM, acc_ref[pp, 0], acc_ref[pp, 1]).astype(BF16)


def _sb_attn(sq_, sk_, sv_, tri, *, hp, kv_buffers, tq, tk, q_off, n_masked):
    b, sq, _ = sq_.shape
    skp = sk_.shape[1]
    q_spec, kv_spec, _ = _attn_specs(tq, skp, width=hp * LANES, kv_buffers=kv_buffers)
    kern = functools.partial(_sb_kernel, hp=hp, tq=tq, tk=tk, q_off=q_off, n_masked=n_masked)
    return pl.pallas_call(
        kern,
        grid=(b, SB_HEADS // (2 * hp), sq // tq),
        in_specs=[q_spec, kv_spec, kv_spec, _full(tri.shape)],
        out_specs=q_spec,
        out_shape=jax.ShapeDtypeStruct((b, sq, SB_HEADS * SB_DIM), BF16),
        scratch_shapes=[pltpu.VMEM((hp, 2, tq, 1), F32), pltpu.VMEM((hp, 2, tq, LANES), F32)],
        compiler_params=_cparams(3),
        name="sb_attn",
    )(sq_, sk_, sv_, tri)


CA_ROW_GROUP = 16


def _ca_kernel(tab_ref, q_ref, k_ref, v_ref, o_ref, bias_ref, *, hp, tq, win, win_real, q_off):
    first = (pl.program_id(0) == 0) & (pl.program_id(1) == 0) & (pl.program_id(2) == 0)
    n_shift = -(-(tq + win) // LANES) * LANES
    n_rel = n_shift + LANES

    @pl.when(first)
    def _build_bias():
        x = lax.broadcasted_iota(jnp.int32, (CA_HEADS, n_rel), 1)
        idx = jnp.clip(x - (tq - 1) - CA_BAND, -CA_MAX_REL, CA_MAX_REL) + CA_MAX_REL
        f = jnp.zeros((CA_HEADS, n_rel), F32)
        for t in range(2 * CA_MAX_REL + 1):
            f = jnp.where(idx == t, tab_ref[:, t:t + 1], f)
        g = CA_ROW_GROUP
        i_loc = lax.broadcasted_iota(jnp.int32, (g, win), 0)
        j_loc = lax.broadcasted_iota(jnp.int32, (g, win), 1)
        for hh in range(CA_HEADS):
            fh = f[hh:hh + 1, :]
            shifted = jnp.concatenate([fh[:, g - 1 - bb:g - 1 - bb + n_shift] for bb in range(g)], axis=0)
            for a in range(tq // g):
                start = tq - g * a - g
                tile = shifted[:, start:start + win]
                i = i_loc + g * a
                kc = (j_loc >> CHUNK_SHIFT) - CA_LEFT_CHUNKS
                qc = i >> CHUNK_SHIFT
                ok = (kc <= qc) & (kc >= qc - CA_LEFT_CHUNKS) & (j_loc < win_real)
                bias_ref[hh, g * a:g * a + g, :] = jnp.where(ok, tile, NEG)

    group = pl.program_id(1)
    qi = pl.program_id(2)
    qpos0 = q_off + qi * tq
    ws = pl.multiple_of(qi * tq, tq)
    lane = lax.broadcasted_iota(jnp.int32, (tq, LANES), 1)
    kpos = qpos0 - CA_BAND + lax.broadcasted_iota(jnp.int32, (tq, win), 1)
    for pp in range(hp):
        cols = slice(pp * LANES, (pp + 1) * LANES)
        q = q_ref[:, cols]
        k = k_ref[pl.ds(ws, win), cols]
        v = v_ref[pl.ds(ws, win), cols]
        outs = []
        for hh in range(2):
            qh = _keep_lanes(q, (lane < CA_DIM) if hh == 0 else (lane >= CA_DIM))
            s = _dot_nt(qh, k) + bias_ref[2 * (group * hp + pp) + hh]
            s = jnp.where(kpos >= 0, s, NEG)
            m = jnp.max(s, axis=-1, keepdims=True)
            p = jnp.exp(s - m)
            denom = jnp.sum(p, axis=-1, keepdims=True)
            outs.append(_dot(p.astype(BF16), v) / denom)
        o_ref[:, cols] = jnp.where(lane < CA_DIM, outs[0], outs[1]).astype(BF16)


def _ca_attn(tab_t, cq, ck, cv, *, hp, kv_buffers, tq, win, win_real, q_off):
    b, sq, _ = cq.shape
    skp = ck.shape[1]
    q_spec, kv_spec, _ = _attn_specs(tq, skp, width=hp * LANES, kv_buffers=kv_buffers)
    kern = functools.partial(_ca_kernel, hp=hp, tq=tq, win=win, win_real=win_real, q_off=q_off)
    return pl.pallas_call(
        kern,
        grid=(b, CA_HEADS // (2 * hp), sq // tq),
        in_specs=[_full(tab_t.shape), q_spec, kv_spec, kv_spec],
        out_specs=q_spec,
        out_shape=jax.ShapeDtypeStruct((b, sq, CA_HEADS * CA_DIM), BF16),
        scratch_shapes=[pltpu.VMEM((CA_HEADS, tq, win), F32)],
        compiler_params=_cparams(3),
        name="ca_attn",
    )(tab_t, cq, ck, cv)


def _post_kernel(x_ref, oa_ref, ob_ref, wa_ref, wb_ref, g_ref, w1_ref, w2_ref, gf_ref, y_ref,
                 x1_ref, hn_ref, acc_ref, *, final_norm):
    j = pl.program_id(1)

    @pl.when(j == 0)
    def _mix():
        x1 = x_ref[...] + _dot(oa_ref[...], wa_ref[...]) + _dot(ob_ref[...], wb_ref[...])
        x1_ref[...] = x1
        hn_ref[...] = _rms(x1, g_ref[...]).astype(BF16)
        acc_ref[...] = jnp.zeros(acc_ref.shape, F32)

    a = jnp.maximum(_dot(hn_ref[...], w1_ref[...]), 0.0)
    acc_ref[...] += _dot((a * a).astype(BF16), w2_ref[...])

    @pl.when(j == pl.num_programs(1) - 1)
    def _finish():
        y = x1_ref[...] + acc_ref[...]
        if final_norm:
            y = _rms(y, gf_ref[...])
        y_ref[...] = y


def _post(x, oa, ob, wa, wb, g, w1, w2, gf, *, tm, tf, final_norm):
    m = x.shape[0]
    d_ff = w1.shape[1]
    rows = lambda width: pl.BlockSpec((tm, width), lambda i, j: (i, 0))
    return pl.pallas_call(
        functools.partial(_post_kernel, final_norm=final_norm),
        grid=(m // tm, d_ff // tf),
        in_specs=[rows(D_MODEL), rows(oa.shape[1]), rows(ob.shape[1]), _full(wa.shape), _full(wb.shape),
                  _full(g.shape), pl.BlockSpec((D_MODEL, tf), lambda i, j: (0, j)),
                  pl.BlockSpec((tf, D_MODEL), lambda i, j: (j, 0)), _full(gf.shape)],
        out_specs=rows(D_MODEL),
        out_shape=jax.ShapeDtypeStruct((m, D_MODEL), F32),
        scratch_shapes=[pltpu.VMEM((tm, D_MODEL), F32), pltpu.VMEM((tm, D_MODEL), BF16),
                        pltpu.VMEM((tm, D_MODEL), F32)],
        compiler_params=_cparams(2),
        name="post_mlp",
    )(x, oa, ob, wa, wb, g, w1, w2, gf)


def _t5_bucket(rel):
    nb = T5_BUCKETS // 2
    max_exact = nb // 2
    ret = jnp.where(rel > 0, nb, 0)
    n = jnp.abs(rel)
    nf = jnp.maximum(n, 1).astype(F32)
    large = max_exact + (jnp.log(nf / max_exact) / math.log(T5_MAX_DIST / max_exact) * (nb - max_exact)).astype(jnp.int32)
    large = jnp.minimum(large, nb - 1)
    return ret + jnp.where(n < max_exact, n, large)


def _rope_tables(pos):
    half = MLA_ROPE // 2
    inv = ROPE_BASE ** (-jnp.arange(half, dtype=F32) / half)
    ang = pos.astype(F32)[:, None] * inv[None, :]
    cos, sin = jnp.cos(ang), jnp.sin(ang)
    cosk = jnp.concatenate([cos, cos], axis=1)
    sink = jnp.concatenate([-sin, sin], axis=1)
    n = pos.shape[0]
    pad = jnp.zeros((n, LANES - MLA_NOPE - MLA_ROPE), F32)
    qscale = (MLA_NOPE + MLA_ROPE) ** -0.5 * LOG2E
    cosq = jnp.concatenate([jnp.ones((n, MLA_NOPE), F32), cosk, pad], axis=1) * qscale
    sinq = jnp.concatenate([jnp.zeros((n, MLA_NOPE), F32), sink, pad], axis=1) * qscale
    return {"cosq": cosq, "sinq": sinq, "cosk": cosk, "sink": sink}


def _swap_halves(w):
    half = w.shape[-1] // 2
    return jnp.concatenate([w[..., half:], w[..., :half]], axis=-1)


def _even_weights(w_in, q_norm, kv_norm, w_uq, w_ukv, w_out):
    sizes = [MLA_Q_RANK, MLA_KV_RANK, MLA_ROPE, DIFF_HEADS * 2 * DIFF_QK, DIFF_HEADS * 2 * DIFF_QK,
             DIFF_HEADS * DIFF_V]
    offs = np.cumsum([0] + sizes)
    wcq, wckv, wkr, wdq, wdk, wdv = (w_in[:, offs[i]:offs[i + 1]].astype(BF16) for i in range(6))
    uq = w_uq.reshape(MLA_Q_RANK, MLA_HEADS, MLA_NOPE + MLA_ROPE)
    zq = jnp.zeros((MLA_Q_RANK, MLA_HEADS, LANES - MLA_NOPE - MLA_ROPE), F32)
    wq = jnp.concatenate([uq, zq], axis=-1)
    wqs = jnp.concatenate([jnp.zeros_like(uq[..., :MLA_NOPE]), _swap_halves(uq[..., MLA_NOPE:]), zq], axis=-1)
    ukv = w_ukv.reshape(MLA_KV_RANK, MLA_HEADS, MLA_NOPE + MLA_V)
    zk = jnp.zeros((MLA_KV_RANK, MLA_HEADS, LANES - MLA_NOPE), F32)
    wk = jnp.concatenate([ukv[..., :MLA_NOPE], zk], axis=-1)
    wv = jnp.concatenate([ukv[..., MLA_NOPE:], jnp.zeros((MLA_KV_RANK, MLA_HEADS, LANES - MLA_V), F32)], axis=-1)
    place = np.zeros((MLA_ROPE, MLA_HEADS, LANES), np.float32)
    ones = np.zeros((1, MLA_HEADS, LANES), np.float32)
    for hh in range(MLA_HEADS):
        place[np.arange(MLA_ROPE), hh, MLA_NOPE + np.arange(MLA_ROPE)] = 1.0
        ones[0, hh, MLA_V] = 1.0
    flat = lambda a: a.reshape(a.shape[0], MLA_HEADS * LANES)
    wo_mla = w_out[:MLA_HEADS * MLA_V].reshape(MLA_HEADS, MLA_V, D_MODEL)
    wo_mla = jnp.concatenate([wo_mla, jnp.zeros((MLA_HEADS, LANES - MLA_V, D_MODEL), F32)], axis=1)
    return {
        "wcq": wcq, "wckv": wckv, "wkr": wkr, "wkrs": _swap_halves(wkr), "wdq": wdq, "wdk": wdk, "wdv": wdv,
        "qn": q_norm.reshape(1, -1), "kvn": kv_norm.reshape(1, -1),
        "wq": flat(wq).astype(BF16), "wqs": flat(wqs).astype(BF16),
        "wk": flat(wk).astype(BF16), "wv": flat(wv).astype(BF16),
        "place": jnp.asarray(flat(place), BF16), "ones": jnp.asarray(flat(ones), F32),
        "wo_mla": wo_mla.reshape(MLA_HEADS * LANES, D_MODEL).astype(BF16),
        "wo_diff": w_out[MLA_HEADS * MLA_V:].astype(BF16),
    }


def _pad_rows(a, total, front=0):
    back = total - front - a.shape[1]
    return jnp.pad(a, ((0, 0), (front, back), (0, 0)))


def _round_up(n, mult):
    return -(-n // mult) * mult


def _diff_buckets(tq, tk, q_off, sk_real, skp):
    near_back = -((q_off - (T5_MAX_DIST - 1)) // tk - q_off // tk)
    last = (_round_up(q_off + tq, CHUNK) - 1) // tk
    last = min(last, skp // tk - 1)
    n_near = last - (q_off // tk - near_back) + 1
    i = np.arange(tq)[:, None]
    mats = []
    for r in range(n_near):
        kp = (q_off // tk - near_back + r) * tk + np.arange(tk)[None, :]
        qp = q_off + i
        ok = ((kp >> CHUNK_SHIFT) <= (qp >> CHUNK_SHIFT)) & (kp < sk_real)
        bkt = _t5_bucket(jnp.asarray(kp - qp, jnp.int32))
        mats.append(jnp.where(jnp.asarray(ok), bkt, -1))
    return jnp.stack(mats).astype(jnp.int32), near_back


def _trunk(x, q_off, caches, prm, cfg):
    b, sq, _ = x.shape
    m = b * sq
    tq, tk, tm, tf = cfg["tq"], cfg["tk"], cfg["tm"], cfg["tf"]
    sk_real = q_off + sq
    skp = _round_up(sk_real, tk)
    pos = q_off + jnp.arange(sq, dtype=jnp.int32)
    tabs = {k: jnp.tile(v, (b, 1)) for k, v in _rope_tables(pos).items()}
    x2 = x.reshape(m, D_MODEL)

    def with_past(past, new, dtype):
        new = new.reshape(b, sq, -1)
        if past is None:
            return new.astype(dtype)
        return jnp.concatenate([past.reshape(b, past.shape[1], -1).astype(dtype), new.astype(dtype)], axis=1)

    ew = prm["even"]
    qext, ckv, kr, dq, dk, dkb, dv, dvb = _even_proj(x2, prm["norm_mix"][0:1], ew, tabs, tm)
    past = (None,) * 4 if caches is None else tuple(c[0] for c in caches[:4])
    ckv_all = _pad_rows(with_past(past[0], ckv, F32), skp)
    kr_all = _pad_rows(with_past(past[1], kr, F32), skp)
    kext, vext = _kv_up(ckv_all.reshape(b * skp, -1), kr_all.reshape(b * skp, -1), ew, cfg["tm_kv"])
    kext = kext.reshape(b, skp, -1)
    vext = vext.reshape(b, skp, -1)
    tqs, n_sub = cfg["tqs"], cfg["n_sub"]
    assert n_sub == 1 or tqs == tk
    n_diag = (_round_up(q_off + tqs * n_sub, CHUNK) - 1) // tk - q_off // tk + 1
    tiles = dict(tqs=tqs, n_sub=n_sub, n_diag=n_diag, tkw=cfg["tkw"], tkn=tk, q_off=q_off)
    o_mla = _mla_attn(qext.reshape(b, sq, -1), kext, vext, hs=cfg["mla_heads"], kv_buffers=cfg["kv_buffers"],
                      sk_real=sk_real, **tiles)
    dk_all = _pad_rows(with_past(past[2], dkb, BF16), skp)
    past_dv = past[3]
    if past_dv is not None:
        past_dv = jnp.concatenate([past_dv.astype(BF16), jnp.ones(past_dv.shape, BF16)], axis=-1)
    dv_all = _pad_rows(with_past(past_dv, dvb, BF16), skp)
    bkt, near_back = _diff_buckets(tqs, tk, q_off, sk_real, skp)
    assert bkt.shape[0] == near_back + n_diag - (n_sub - 1)
    lam_init = 0.8 - 0.6 * math.exp(-0.3 * 0)
    o_diff = _diff_attn(prm["t5"], dq.reshape(b, sq, -1), dk_all, dv_all, bkt, prm["lam_vecs"], prm["subln"],
                        hs=cfg["diff_heads"], near_back=near_back, far_bucket=T5_BUCKETS // 2 - 1,
                        lam_init=lam_init, **tiles)
    x2 = _post(x2, o_mla.reshape(m, -1), o_diff.reshape(m, -1), ew["wo_mla"], ew["wo_diff"],
               prm["norm_ff"][0:1], prm["w_ff1"][0], prm["w_ff2"][0], prm["final_norm"],
               tm=cfg["tm_post"], tf=tf, final_norm=False)
    new_even = (ckv.reshape(1, b, sq, MLA_KV_RANK), kr.reshape(1, b, sq, MLA_ROPE),
                dk.reshape(1, b, sq, DIFF_HEADS, 2 * DIFF_QK), dv.reshape(1, b, sq, DIFF_HEADS, DIFF_V))

    sq_, sk_, skb, sv_, svb, cq, ck, ckb, cv, cvb = _odd_proj(x2, prm["norm_mix"][1:2], prm["w_in_odd"], tm)
    past = (None,) * 4 if caches is None else tuple(c[0] for c in caches[4:])
    sk_all = _pad_rows(with_past(past[0], skb, BF16), skp)
    sv_all = _pad_rows(with_past(past[1], svb, BF16), skp)
    n_masked = (q_off + tq - 1) // tk - q_off // tk + 1
    pairs = dict(hp=cfg["head_pairs"], kv_buffers=cfg["kv_buffers"])
    o_sb = _sb_attn(sq_.reshape(b, sq, -1), sk_all, sv_all, prm["tri"][tk], tq=tq, tk=tk, q_off=q_off,
                    n_masked=n_masked, **pairs)
    win_real = tq + CA_BAND
    win = _round_up(win_real, LANES)
    if caches is None:
        ck_all = _pad_rows(ckb.reshape(b, sq, -1), sq + CA_BAND + win - win_real, front=CA_BAND)
        cv_all = _pad_rows(cvb.reshape(b, sq, -1), sq + CA_BAND + win - win_real, front=CA_BAND)
    else:
        ck_all = _pad_rows(with_past(past[2], ckb, BF16), win)
        cv_all = _pad_rows(with_past(past[3], cvb, BF16), win)
    o_ca = _ca_attn(prm["ca_tab_t"], cq.reshape(b, sq, -1), ck_all, cv_all, tq=tq, win=win, win_real=win_real,
                    q_off=q_off, **pairs)
    x2 = _post(x2, o_sb.reshape(m, -1), o_ca.reshape(m, -1), prm["wo_sb"], prm["wo_ca"],
               prm["norm_ff"][1:2], prm["w_ff1"][1], prm["w_ff2"][1], prm["final_norm"],
               tm=cfg["tm_post"], tf=tf, final_norm=True)

    heads = lambda a: a.reshape(b, sq, SB_HEADS, SB_DIM)
    if caches is None:
        nb = min(CA_BAND, sq)
        cak, cav = heads(ck)[:, sq - nb:], heads(cv)[:, sq - nb:]
    else:
        nb = past[2].shape[1]
        cak = jnp.concatenate([past[2], heads(ck)], axis=1)[:, sq:]
        cav = jnp.concatenate([past[3], heads(cv)], axis=1)[:, sq:]
        assert cak.shape[1] == nb
    new_odd = (heads(sk_)[None], heads(sv_)[None], cak[None], cav[None])
    return x2.reshape(b, sq, D_MODEL), new_even + new_odd


def _tri(tk):
    j = np.arange(tk)[:, None]
    s = np.arange(tk)[None, :]
    return jnp.asarray((j > s).astype(np.float32), BF16)


def kernel(x_prompt, x_sample, cache_mla_ckv, cache_mla_krope, cache_diff_k, cache_diff_v, cache_sb_k, cache_sb_v, cache_ca_k, cache_ca_v, norm_mix, norm_ff, w_in_even, mla_q_norm, mla_kv_norm, mla_w_uq, mla_w_ukv, diff_lambda_vecs, diff_subln, t5_bias, w_out_even, w_in_odd, ca_rel_bias, w_out_odd, w_ff1, w_ff2, final_norm):
    seq = x_prompt.shape[1]
    dec_seq = x_sample.shape[1]
    past_len = cache_mla_ckv.shape[2]
    assert cache_ca_k.shape[2] == CA_BAND and past_len % CHUNK == 0

    cfg_p = {"tq": 256, "tqs": 256, "n_sub": 4, "tk": 256, "tkw": (2048, 1024), "mla_heads": 2, "diff_heads": 1,
             "head_pairs": 2, "kv_buffers": 1,
             "tm": min(512, seq), "tm_kv": min(512, seq), "tm_post": min(1024, seq), "tf": 512}
    rows_s = x_sample.shape[0] * dec_seq
    cfg_s = {"tq": dec_seq, "tqs": dec_seq, "n_sub": 1, "tk": 128, "tkw": (512,), "mla_heads": MLA_HEADS,
             "diff_heads": DIFF_HEADS, "head_pairs": SB_HEADS // 2, "kv_buffers": None, "tm": rows_s, "tm_kv": x_sample.shape[0] * 128 // 2, "tm_post": rows_s,
             "tf": 512}
    n_sb = SB_HEADS * SB_DIM
    prm = {
        "norm_mix": norm_mix, "norm_ff": norm_ff, "final_norm": final_norm.reshape(1, -1),
        "even": _even_weights(w_in_even[0], mla_q_norm[0], mla_kv_norm[0], mla_w_uq[0], mla_w_ukv[0],
                              w_out_even[0]),
        "t5": t5_bias, "lam_vecs": diff_lambda_vecs[0], "subln": diff_subln[0].reshape(1, -1),
        "w_in_odd": w_in_odd[0].astype(BF16), "ca_tab_t": ca_rel_bias[0].T,
        "wo_sb": w_out_odd[0][:n_sb].astype(BF16), "wo_ca": w_out_odd[0][n_sb:].astype(BF16),
        "w_ff1": w_ff1.astype(BF16), "w_ff2": w_ff2.astype(BF16),
        "tri": {tk: _tri(tk) for tk in {cfg_p["tk"], cfg_s["tk"]}},
    }
    y_prompt, new_p = _trunk(x_prompt, 0, None, prm, cfg_p)
    caches = (cache_mla_ckv, cache_mla_krope, cache_diff_k, cache_diff_v,
              cache_sb_k, cache_sb_v, cache_ca_k, cache_ca_v)
    y_sample, new_s = _trunk(x_sample, past_len, caches, prm, cfg_s)
    return (y_prompt, y_sample) + tuple(new_p) + tuple(new_s)
```

```python
import functools
import math

import numpy as np
import jax
import jax.numpy as jnp
from jax import lax
from jax.experimental import pallas as pl
from jax.experimental.pallas import tpu as pltpu

F32 = jnp.float32
BF16 = jnp.bfloat16

D_MODEL = 1024
CHUNK = 64
CHUNK_SHIFT = 6
EPS = 1e-6
NEG = -1e30

MLA_HEADS = 8
MLA_Q_RANK = 256
MLA_KV_RANK = 128
MLA_NOPE = 64
MLA_ROPE = 32
MLA_V = 64
ROPE_BASE = 10000.0
DIFF_HEADS = 4
DIFF_QK = 64
DIFF_V = 2 * DIFF_QK
T5_BUCKETS = 32
T5_MAX_DIST = 128
SB_HEADS = 8
SB_DIM = 64
CA_HEADS = 8
CA_DIM = 64
CA_LEFT_CHUNKS = 8
CA_BAND = CA_LEFT_CHUNKS * CHUNK
CA_MAX_REL = 128

LANES = 128
VMEM_LIMIT = 48 * 1024 * 1024
LOG2E = math.log2(math.e)
SB_EXIT = -104.0


def _cparams(n_axes):
    return pltpu.CompilerParams(dimension_semantics=("arbitrary",) * n_axes,
                                vmem_limit_bytes=VMEM_LIMIT)


def _rms(x, g):
    return x * lax.rsqrt(jnp.mean(x * x, axis=-1, keepdims=True) + EPS) * g


def _dot(a, b):
    return jnp.dot(a, b, preferred_element_type=F32)


def _dot_nt(a, b):
    return lax.dot_general(a, b, (((1,), (1,)), ((), ())), preferred_element_type=F32)


def _keep_lanes(q, keep):
    return jnp.where(keep, q.astype(F32), 0.0).astype(BF16)


def _full(shape):
    n = len(shape)
    return pl.BlockSpec(shape, lambda *_: (0,) * n)


def _rows(tm, width):
    return pl.BlockSpec((tm, width), lambda i: (i, 0))


def _even_proj_kernel(x_ref, g_ref, wcq_ref, wckv_ref, wkr_ref, wkrs_ref, wdq_ref, wdk_ref, wdv_ref,
                      qn_ref, kvn_ref, wq_ref, wqs_ref, cosq_ref, sinq_ref, cosk_ref, sink_ref,
                      qext_ref, ckv_ref, kr_ref, dq_ref, dk_ref, dkb_ref, dv_ref, dvb_ref):
    hn = _rms(x_ref[...], g_ref[...]).astype(BF16)
    cq = _rms(_dot(hn, wcq_ref[...]), qn_ref[...]).astype(BF16)
    cosq = jnp.concatenate([cosq_ref[...]] * MLA_HEADS, axis=1)
    sinq = jnp.concatenate([sinq_ref[...]] * MLA_HEADS, axis=1)
    qext_ref[...] = (_dot(cq, wq_ref[...]) * cosq + _dot(cq, wqs_ref[...]) * sinq).astype(BF16)
    ckv_ref[...] = _rms(_dot(hn, wckv_ref[...]), kvn_ref[...])
    kr_ref[...] = _dot(hn, wkr_ref[...]) * cosk_ref[...] + _dot(hn, wkrs_ref[...]) * sink_ref[...]
    dq_ref[...] = (_dot(hn, wdq_ref[...]) * (DIFF_QK ** -0.5 * LOG2E)).astype(BF16)
    dk = _dot(hn, wdk_ref[...])
    dk_ref[...] = dk
    dkb_ref[...] = dk.astype(BF16)
    dv = _dot(hn, wdv_ref[...])
    dv_ref[...] = dv
    dvb = dv.astype(BF16)
    ones = jnp.ones((dvb.shape[0], DIFF_V), BF16)
    dvb_ref[...] = jnp.concatenate(
        [piece for hh in range(DIFF_HEADS) for piece in (dvb[:, hh * DIFF_V:(hh + 1) * DIFF_V], ones)], axis=1)


def _even_proj(x, g, w, tabs, tm):
    m = x.shape[0]
    ins = [x, g, w["wcq"], w["wckv"], w["wkr"], w["wkrs"], w["wdq"], w["wdk"], w["wdv"],
           w["qn"], w["kvn"], w["wq"], w["wqs"], tabs["cosq"], tabs["sinq"], tabs["cosk"], tabs["sink"]]
    row_in = {0: D_MODEL, 13: LANES, 14: LANES, 15: MLA_ROPE, 16: MLA_ROPE}
    in_specs = [_rows(tm, row_in[i]) if i in row_in else _full(a.shape) for i, a in enumerate(ins)]
    outs = [(MLA_HEADS * LANES, BF16), (MLA_KV_RANK, F32), (MLA_ROPE, F32),
            (DIFF_HEADS * DIFF_V, BF16), (DIFF_HEADS * DIFF_V, F32), (DIFF_HEADS * DIFF_V, BF16),
            (DIFF_HEADS * DIFF_V, F32), (DIFF_HEADS * 2 * DIFF_V, BF16)]
    return pl.pallas_call(
        _even_proj_kernel,
        grid=(m // tm,),
        in_specs=in_specs,
        out_specs=[_rows(tm, n) for n, _ in outs],
        out_shape=[jax.ShapeDtypeStruct((m, n), dt) for n, dt in outs],
        compiler_params=_cparams(1),
        name="even_proj",
    )(*ins)


def _odd_proj_kernel(x_ref, g_ref, w_ref, sq_ref, sk_ref, skb_ref, sv_ref, svb_ref,
                     cq_ref, ck_ref, ckb_ref, cv_ref, cvb_ref):
    hn = _rms(x_ref[...], g_ref[...]).astype(BF16)
    width = SB_HEADS * SB_DIM

    def seg(i):
        return _dot(hn, w_ref[:, i * width:(i + 1) * width])

    sq_ref[...] = (seg(0) * (SB_DIM ** -0.5)).astype(BF16)
    for i, (f_ref, b_ref) in ((1, (sk_ref, skb_ref)), (2, (sv_ref, svb_ref)),
                              (4, (ck_ref, ckb_ref)), (5, (cv_ref, cvb_ref))):
        y = seg(i)
        f_ref[...] = y
        b_ref[...] = y.astype(BF16)
    cq_ref[...] = (seg(3) * (CA_DIM ** -0.5)).astype(BF16)


def _odd_proj(x, g, w, tm):
    m = x.shape[0]
    width = SB_HEADS * SB_DIM
    dts = [BF16, F32, BF16, F32, BF16, BF16, F32, BF16, F32, BF16]
    return pl.pallas_call(
        _odd_proj_kernel,
        grid=(m // tm,),
        in_specs=[_rows(tm, D_MODEL), _full(g.shape), _full(w.shape)],
        out_specs=[_rows(tm, width) for _ in dts],
        out_shape=[jax.ShapeDtypeStruct((m, width), dt) for dt in dts],
        compiler_params=_cparams(1),
        name="odd_proj",
    )(x, g, w)


def _kv_up_kernel(ckv_ref, kr_ref, wk_ref, wv_ref, place_ref, ones_ref, kext_ref, vext_ref):
    c = ckv_ref[...].astype(BF16)
    r = kr_ref[...].astype(BF16)
    kext_ref[...] = (_dot(c, wk_ref[...]) + _dot(r, place_ref[...])).astype(BF16)
    vext_ref[...] = (_dot(c, wv_ref[...]) + ones_ref[...]).astype(BF16)


def _kv_up(ckv, kr, w, tm):
    m = ckv.shape[0]
    width = MLA_HEADS * LANES
    return pl.pallas_call(
        _kv_up_kernel,
        grid=(m // tm,),
        in_specs=[_rows(tm, MLA_KV_RANK), _rows(tm, MLA_ROPE), _full(w["wk"].shape), _full(w["wv"].shape),
                  _full(w["place"].shape), _full(w["ones"].shape)],
        out_specs=[_rows(tm, width)] * 2,
        out_shape=[jax.ShapeDtypeStruct((m, width), BF16)] * 2,
        compiler_params=_cparams(1),
        name="mla_kv_up",
    )(ckv, kr, w["wk"], w["wv"], w["place"], w["ones"])


def _attn_specs(tq, skp, width=LANES, v_width=None, kv_buffers=None):
    kw = {} if kv_buffers is None else {"pipeline_mode": pl.Buffered(kv_buffers)}
    q_spec = pl.BlockSpec((None, tq, width), lambda b, h, qi: (b, qi, h))
    k_spec = pl.BlockSpec((None, skp, width), lambda b, h, qi: (b, 0, h), **kw)
    v_spec = pl.BlockSpec((None, skp, v_width or width), lambda b, h, qi: (b, 0, h), **kw)
    return q_spec, k_spec, v_spec


def _softmax_block(s, m_ref):
    m_old = m_ref[...]
    m_new = jnp.maximum(m_old, jnp.max(s, axis=-1, keepdims=True))
    m_ref[...] = m_new
    return jnp.exp2(m_old - m_new), jnp.exp2(s - jnp.tile(m_new, (1, s.shape[1] // LANES)))


def _sweep_blocks(nb0, near_back, n_sub, n_diag, tkw, tkn, do_block):
    first = jnp.maximum(nb0 - near_back, 0)
    done = 0
    for width in tuple(tkw) + (tkn,):
        per = width // tkn
        count = (first - done) // per

        def plain(j, carry, width=width, per=per, done=done):
            do_block(pl.multiple_of((done + j * per) * tkn, width), width, None)
            return carry

        lax.fori_loop(0, count, plain, 0)
        done = done + count * per
    for c in range(-near_back, n_diag):
        kinds = []
        for r in range(n_sub):
            d = c - r
            kinds.append("skip" if d > 0 else None if d < -near_back else d + near_back)

        def special(c=c, kinds=kinds):
            do_block(pl.multiple_of((nb0 + c) * tkn, tkn), tkn, kinds)

        if c < 0:
            pl.when(nb0 + c >= 0)(special)
        else:
            special()


def _mla_kernel(q_ref, k_ref, v_ref, o_ref, m_ref, acc_ref, *, hs, tqs, n_sub, n_diag, tkw, tkn, q_off, sk_real):
    qi = pl.program_id(2)
    qpos0 = q_off + qi * (tqs * n_sub)
    m_ref[...] = jnp.full(m_ref.shape, NEG, F32)
    acc_ref[...] = jnp.zeros(acc_ref.shape, F32)

    def block(start, width, kinds):
        for hh in range(hs):
            cols = slice(hh * LANES, (hh + 1) * LANES)
            k = k_ref[pl.ds(start, width), cols]
            v = v_ref[pl.ds(start, width), cols]
            for r in range(n_sub):
                kind = None if kinds is None else kinds[r]
                if kind == "skip":
                    continue
                s = _dot_nt(q_ref[r * tqs:(r + 1) * tqs, cols], k)
                if kind is not None:
                    qp = qpos0 + r * tqs + lax.broadcasted_iota(jnp.int32, (tqs, width), 0)
                    kp = start + lax.broadcasted_iota(jnp.int32, (tqs, width), 1)
                    ok = ((kp >> CHUNK_SHIFT) <= (qp >> CHUNK_SHIFT)) & (kp < sk_real)
                    s = jnp.where(ok, s, NEG)
                alpha, p = _softmax_block(s, m_ref.at[hh, r])
                acc_ref[hh, r] = acc_ref[hh, r] * alpha + _dot(p.astype(BF16), v)

    _sweep_blocks(qpos0 // tkn, 0, n_sub, n_diag, tkw, tkn, block)

    for hh in range(hs):
        for r in range(n_sub):
            acc = acc_ref[hh, r]
            lane = lax.broadcasted_iota(jnp.int32, acc.shape, 1)
            denom = jnp.sum(jnp.where(lane == MLA_V, acc, 0.0), axis=-1, keepdims=True)
            o_ref[r * tqs:(r + 1) * tqs, hh * LANES:(hh + 1) * LANES] = (
                jnp.where(lane < MLA_V, acc / denom, 0.0).astype(BF16))


def _mla_attn(qext, kext, vext, *, hs, kv_buffers, tqs, n_sub, n_diag, tkw, tkn, q_off, sk_real):
    b, sq, _ = qext.shape
    skp = kext.shape[1]
    tq = tqs * n_sub
    q_spec, k_spec, v_spec = _attn_specs(tq, skp, width=hs * LANES, kv_buffers=kv_buffers)
    kern = functools.partial(_mla_kernel, hs=hs, tqs=tqs, n_sub=n_sub, n_diag=n_diag, tkw=tkw, tkn=tkn,
                             q_off=q_off, sk_real=sk_real)
    return pl.pallas_call(
        kern,
        grid=(b, MLA_HEADS // hs, sq // tq),
        in_specs=[q_spec, k_spec, v_spec],
        out_specs=q_spec,
        out_shape=jax.ShapeDtypeStruct((b, sq, MLA_HEADS * LANES), BF16),
        scratch_shapes=[pltpu.VMEM((hs, n_sub, tqs, LANES), F32), pltpu.VMEM((hs, n_sub, tqs, LANES), F32)],
        compiler_params=_cparams(3),
        name="mla_attn",
    )(qext, kext, vext)


def _diff_kernel(t5_ref, q_ref, k_ref, v_ref, bkt_ref, lamv_ref, subln_ref, o_ref,
                 bias_ref, m_ref, acc_ref, *, hs, tqs, n_sub, n_diag, tkw, tkn, q_off, n_near, near_back, far_bucket,
                 lam_init):
    first = (pl.program_id(0) == 0) & (pl.program_id(1) == 0) & (pl.program_id(2) == 0)

    @pl.when(first)
    def _build_bias():
        for r in range(n_near):
            bkt = bkt_ref[r]
            vals = [jnp.full((tqs, tkn), NEG, F32) for _ in range(DIFF_HEADS)]
            for t in range(T5_BUCKETS):
                hit = bkt == t
                for hh in range(DIFF_HEADS):
                    vals[hh] = jnp.where(hit, (t5_ref[t, hh] - t5_ref[far_bucket, hh]) * LOG2E, vals[hh])
            for hh in range(DIFF_HEADS):
                bias_ref[r, hh] = vals[hh]

    group = pl.program_id(1)
    qi = pl.program_id(2)
    qpos0 = q_off + qi * (tqs * n_sub)
    lane = lax.broadcasted_iota(jnp.int32, (tqs * n_sub, LANES), 1)
    qm = []
    for hh in range(hs):
        q = q_ref[:, hh * LANES:(hh + 1) * LANES]
        qm.append([_keep_lanes(q, lane < DIFF_QK), _keep_lanes(q, lane >= DIFF_QK)])
    m_ref[...] = jnp.full(m_ref.shape, NEG, F32)
    acc_ref[...] = jnp.zeros(acc_ref.shape, F32)

    def block(start, width, kinds):
        for hh in range(hs):
            k = k_ref[pl.ds(start, width), hh * LANES:(hh + 1) * LANES]
            v = v_ref[pl.ds(start, width), hh * 2 * DIFF_V:(hh + 1) * 2 * DIFF_V]
            for r in range(n_sub):
                kind = None if kinds is None else kinds[r]
                if kind == "skip":
                    continue
                for mi in range(2):
                    s = _dot_nt(qm[hh][mi][r * tqs:(r + 1) * tqs, :], k)
                    if kind is not None:
                        s = s + bias_ref[kind, group * hs + hh]
                    alpha, p = _softmax_block(s, m_ref.at[hh, mi, r])
                    acc_ref[hh, mi, r] = (acc_ref[hh, mi, r] * jnp.tile(alpha, (1, 2))
                                          + _dot(p.astype(BF16), v))

    _sweep_blocks(qpos0 // tkn, near_back, n_sub, n_diag, tkw, tkn, block)

    lv = lamv_ref[...]
    lam = (jnp.exp(jnp.sum(lv[0:1] * lv[1:2], axis=-1, keepdims=True))
           - jnp.exp(jnp.sum(lv[2:3] * lv[3:4], axis=-1, keepdims=True)) + lam_init)
    for hh in range(hs):
        for r in range(n_sub):
            a0, a1 = acc_ref[hh, 0, r], acc_ref[hh, 1, r]
            o = a0[:, :DIFF_V] / a0[:, DIFF_V:] - lam * (a1[:, :DIFF_V] / a1[:, DIFF_V:])
            o_ref[r * tqs:(r + 1) * tqs, hh * LANES:(hh + 1) * LANES] = (
                _rms(o, subln_ref[...]) * (1.0 - lam_init)).astype(BF16)


def _diff_attn(t5, dq, dk, dv, bkt, lamv, subln, *, hs, tqs, n_sub, n_diag, tkw, tkn, q_off, near_back, far_bucket,
               lam_init):
    b, sq, _ = dq.shape
    skp = dk.shape[1]
    n_near = bkt.shape[0]
    tq = tqs * n_sub
    q_spec, k_spec, v_spec = _attn_specs(tq, skp, width=hs * LANES, v_width=hs * 2 * DIFF_V)
    kern = functools.partial(_diff_kernel, hs=hs, tqs=tqs, n_sub=n_sub, n_diag=n_diag, tkw=tkw, tkn=tkn,
                             q_off=q_off, n_near=n_near, near_back=near_back, far_bucket=far_bucket,
                             lam_init=lam_init)
    return pl.pallas_call(
        kern,
        grid=(b, DIFF_HEADS // hs, sq // tq),
        in_specs=[pl.BlockSpec(memory_space=pltpu.SMEM), q_spec, k_spec, v_spec,
                  _full(bkt.shape), _full(lamv.shape), _full(subln.shape)],
        out_specs=q_spec,
        out_shape=jax.ShapeDtypeStruct((b, sq, DIFF_HEADS * DIFF_V), BF16),
        scratch_shapes=[pltpu.VMEM((n_near, DIFF_HEADS, tqs, tkn), F32), pltpu.VMEM((hs, 2, n_sub, tqs, LANES), F32),
                        pltpu.VMEM((hs, 2, n_sub, tqs, 2 * DIFF_V), F32)],
        compiler_params=_cparams(3),
        name="diff_attn",
    )(t5, dq, dk, dv, bkt, lamv, subln)


def _sb_kernel(q_ref, k_ref, v_ref, tri_ref, o_ref, run_ref, acc_ref, *, hp, tq, tk, q_off, n_masked):
    qi = pl.program_id(2)
    qpos0 = q_off + qi * tq
    top = (qpos0 + tq - 1) // tk
    lane = lax.broadcasted_iota(jnp.int32, (tq, LANES), 1)
    tri = tri_ref[...]
    qhs = []
    for pp in range(hp):
        qpair = q_ref[:, pp * LANES:(pp + 1) * LANES]
        qhs.append([_keep_lanes(qpair, lane < SB_DIM), _keep_lanes(qpair, lane >= SB_DIM)])

    def block(kj, masked, valid=None):
        if valid is not None:
            kj = jnp.maximum(kj, 0)
        ks = pl.multiple_of(kj * tk, tk)
        if masked:
            qp = qpos0 + lax.broadcasted_iota(jnp.int32, (tq, tk), 0)
            kp = kj * tk + lax.broadcasted_iota(jnp.int32, (tq, tk), 1)
            ok = kp < qp
        for pp in range(hp):
            cols = slice(pp * LANES, (pp + 1) * LANES)
            k = k_ref[pl.ds(ks, tk), cols]
            v = v_ref[pl.ds(ks, tk), cols]
            for hh in range(2):
                z = _dot_nt(qhs[pp][hh], k)
                sp = jnp.maximum(z, 0.0) + jnp.log(1.0 + jnp.exp2(jnp.abs(z) * -LOG2E))
                log_beta = z - sp
                if masked:
                    sp = jnp.where(ok, sp, 0.0)
                if valid is not None:
                    sp = jnp.where(valid, sp, 0.0)
                later = _dot(sp.astype(BF16), tri)
                a = jnp.exp2(((log_beta - run_ref[pp, hh]) - later) * LOG2E)
                if masked:
                    a = jnp.where(ok, a, 0.0)
                if valid is not None:
                    a = jnp.where(valid, a, 0.0)
                acc_ref[pp, hh] += _dot(a.astype(BF16), v)
                run_ref[pp, hh] += jnp.sum(sp, axis=-1, keepdims=True)

    run_ref[...] = jnp.zeros(run_ref.shape, F32)
    acc_ref[...] = jnp.zeros(acc_ref.shape, F32)
    for r in range(n_masked):
        block(top - r, True)
    nxt = top - n_masked
    block(nxt, False, valid=nxt >= 0)

    def cond(kj):
        return (kj >= 0) & (jnp.min(run_ref[...]) < -SB_EXIT)

    def body(kj):
        block(kj, False)
        return kj - 1

    lax.while_loop(cond, body, nxt - 1)
    for pp in range(hp):
        o_ref[:, pp * LANES:(pp + 1) * LANES] = jnp.where(lane < SB_DIM, acc_ref[pp, 0], acc_ref[pp, 1]).astype(BF16)


def _sb_attn(sq_, sk_, sv_, tri, *, hp, kv_buffers, tq, tk, q_off, n_masked):
    b, sq, _ = sq_.shape
    skp = sk_.shape[1]
    q_spec, kv_spec, _ = _attn_specs(tq, skp, width=hp * LANES, kv_buffers=kv_buffers)
    kern = functools.partial(_sb_kernel, hp=hp, tq=tq, tk=tk, q_off=q_off, n_masked=n_masked)
    return pl.pallas_call(
        kern,
        grid=(b, SB_HEADS // (2 * hp), sq // tq),
        in_specs=[q_spec, kv_spec, kv_spec, _full(tri.shape)],
        out_specs=q_spec,
        out_shape=jax.ShapeDtypeStruct((b, sq, SB_HEADS * SB_DIM), BF16),
        scratch_shapes=[pltpu.VMEM((hp, 2, tq, 1), F32), pltpu.VMEM((hp, 2, tq, LANES), F32)],
        compiler_params=_cparams(3),
        name="sb_attn",
    )(sq_, sk_, sv_, tri)


CA_ROW_GROUP = 16


def _ca_kernel(tab_ref, q_ref, k_ref, v_ref, o_ref, bias_ref, *, hp, tq, win, win_real, q_off):
    first = (pl.program_id(0) == 0) & (pl.program_id(1) == 0) & (pl.program_id(2) == 0)
    n_shift = -(-(tq + win) // LANES) * LANES
    n_rel = n_shift + LANES

    @pl.when(first)
    def _build_bias():
        x = lax.broadcasted_iota(jnp.int32, (CA_HEADS, n_rel), 1)
        idx = jnp.clip(x - (tq - 1) - CA_BAND, -CA_MAX_REL, CA_MAX_REL) + CA_MAX_REL
        f = jnp.zeros((CA_HEADS, n_rel), F32)
        for t in range(2 * CA_MAX_REL + 1):
            f = jnp.where(idx == t, tab_ref[:, t:t + 1], f)
        g = CA_ROW_GROUP
        i_loc = lax.broadcasted_iota(jnp.int32, (g, win), 0)
        j_loc = lax.broadcasted_iota(jnp.int32, (g, win), 1)
        for hh in range(CA_HEADS):
            fh = f[hh:hh + 1, :]
            shifted = jnp.concatenate([fh[:, g - 1 - bb:g - 1 - bb + n_shift] for bb in range(g)], axis=0)
            for a in range(tq // g):
                start = tq - g * a - g
                tile = shifted[:, start:start + win]
                i = i_loc + g * a
                kc = (j_loc >> CHUNK_SHIFT) - CA_LEFT_CHUNKS
                qc = i >> CHUNK_SHIFT
                ok = (kc <= qc) & (kc >= qc - CA_LEFT_CHUNKS) & (j_loc < win_real)
                bias_ref[hh, g * a:g * a + g, :] = jnp.where(ok, tile, NEG)

    group = pl.program_id(1)
    qi = pl.program_id(2)
    qpos0 = q_off + qi * tq
    ws = pl.multiple_of(qi * tq, tq)
    lane = lax.broadcasted_iota(jnp.int32, (tq, LANES), 1)
    kpos = qpos0 - CA_BAND + lax.broadcasted_iota(jnp.int32, (tq, win), 1)
    for pp in range(hp):
        cols = slice(pp * LANES, (pp + 1) * LANES)
        q = q_ref[:, cols]
        k = k_ref[pl.ds(ws, win), cols]
        v = v_ref[pl.ds(ws, win), cols]
        outs = []
        for hh in range(2):
            qh = _keep_lanes(q, (lane < CA_DIM) if hh == 0 else (lane >= CA_DIM))
            s = _dot_nt(qh, k) + bias_ref[2 * (group * hp + pp) + hh]
            s = jnp.where(kpos >= 0, s, NEG)
            m = jnp.max(s, axis=-1, keepdims=True)
            p = jnp.exp(s - m)
            denom = jnp.sum(p, axis=-1, keepdims=True)
            outs.append(_dot(p.astype(BF16), v) / denom)
        o_ref[:, cols] = jnp.where(lane < CA_DIM, outs[0], outs[1]).astype(BF16)


def _ca_attn(tab_t, cq, ck, cv, *, hp, kv_buffers, tq, win, win_real, q_off):
    b, sq, _ = cq.shape
    skp = ck.shape[1]
    q_spec, kv_spec, _ = _attn_specs(tq, skp, width=hp * LANES, kv_buffers=kv_buffers)
    kern = functools.partial(_ca_kernel, hp=hp, tq=tq, win=win, win_real=win_real, q_off=q_off)
    return pl.pallas_call(
        kern,
        grid=(b, CA_HEADS // (2 * hp), sq // tq),
        in_specs=[_full(tab_t.shape), q_spec, kv_spec, kv_spec],
        out_specs=q_spec,
        out_shape=jax.ShapeDtypeStruct((b, sq, CA_HEADS * CA_DIM), BF16),
        scratch_shapes=[pltpu.VMEM((CA_HEADS, tq, win), F32)],
        compiler_params=_cparams(3),
        name="ca_attn",
    )(tab_t, cq, ck, cv)


def _post_kernel(x_ref, oa_ref, ob_ref, wa_ref, wb_ref, g_ref, w1_ref, w2_ref, gf_ref, y_ref,
                 x1_ref, hn_ref, acc_ref, *, final_norm):
    j = pl.program_id(1)

    @pl.when(j == 0)
    def _mix():
        x1 = x_ref[...] + _dot(oa_ref[...], wa_ref[...]) + _dot(ob_ref[...], wb_ref[...])
        x1_ref[...] = x1
        hn_ref[...] = _rms(x1, g_ref[...]).astype(BF16)
        acc_ref[...] = jnp.zeros(acc_ref.shape, F32)

    a = jnp.maximum(_dot(hn_ref[...], w1_ref[...]), 0.0)
    acc_ref[...] += _dot((a * a).astype(BF16), w2_ref[...])

    @pl.when(j == pl.num_programs(1) - 1)
    def _finish():
        y = x1_ref[...] + acc_ref[...]
        if final_norm:
            y = _rms(y, gf_ref[...])
        y_ref[...] = y


def _post(x, oa, ob, wa, wb, g, w1, w2, gf, *, tm, tf, final_norm):
    m = x.shape[0]
    d_ff = w1.shape[1]
    rows = lambda width: pl.BlockSpec((tm, width), lambda i, j: (i, 0))
    return pl.pallas_call(
        functools.partial(_post_kernel, final_norm=final_norm),
        grid=(m // tm, d_ff // tf),
        in_specs=[rows(D_MODEL), rows(oa.shape[1]), rows(ob.shape[1]), _full(wa.shape), _full(wb.shape),
                  _full(g.shape), pl.BlockSpec((D_MODEL, tf), lambda i, j: (0, j)),
                  pl.BlockSpec((tf, D_MODEL), lambda i, j: (j, 0)), _full(gf.shape)],
        out_specs=rows(D_MODEL),
        out_shape=jax.ShapeDtypeStruct((m, D_MODEL), F32),
        scratch_shapes=[pltpu.VMEM((tm, D_MODEL), F32), pltpu.VMEM((tm, D_MODEL), BF16),
                        pltpu.VMEM((tm, D_MODEL), F32)],
        compiler_params=_cparams(2),
        name="post_mlp",
    )(x, oa, ob, wa, wb, g, w1, w2, gf)


def _t5_bucket(rel):
    nb = T5_BUCKETS // 2
    max_exact = nb // 2
    ret = jnp.where(rel > 0, nb, 0)
    n = jnp.abs(rel)
    nf = jnp.maximum(n, 1).astype(F32)
    large = max_exact + (jnp.log(nf / max_exact) / math.log(T5_MAX_DIST / max_exact) * (nb - max_exact)).astype(jnp.int32)
    large = jnp.minimum(large, nb - 1)
    return ret + jnp.where(n < max_exact, n, large)


def _rope_tables(pos):
    half = MLA_ROPE // 2
    inv = ROPE_BASE ** (-jnp.arange(half, dtype=F32) / half)
    ang = pos.astype(F32)[:, None] * inv[None, :]
    cos, sin = jnp.cos(ang), jnp.sin(ang)
    cosk = jnp.concatenate([cos, cos], axis=1)
    sink = jnp.concatenate([-sin, sin], axis=1)
    n = pos.shape[0]
    pad = jnp.zeros((n, LANES - MLA_NOPE - MLA_ROPE), F32)
    qscale = (MLA_NOPE + MLA_ROPE) ** -0.5 * LOG2E
    cosq = jnp.concatenate([jnp.ones((n, MLA_NOPE), F32), cosk, pad], axis=1) * qscale
    sinq = jnp.concatenate([jnp.zeros((n, MLA_NOPE), F32), sink, pad], axis=1) * qscale
    return {"cosq": cosq, "sinq": sinq, "cosk": cosk, "sink": sink}


def _swap_halves(w):
    half = w.shape[-1] // 2
    return jnp.concatenate([w[..., half:], w[..., :half]], axis=-1)


def _even_weights(w_in, q_norm, kv_norm, w_uq, w_ukv, w_out):
    sizes = [MLA_Q_RANK, MLA_KV_RANK, MLA_ROPE, DIFF_HEADS * 2 * DIFF_QK, DIFF_HEADS * 2 * DIFF_QK,
             DIFF_HEADS * DIFF_V]
    offs = np.cumsum([0] + sizes)
    wcq, wckv, wkr, wdq, wdk, wdv = (w_in[:, offs[i]:offs[i + 1]].astype(BF16) for i in range(6))
    uq = w_uq.reshape(MLA_Q_RANK, MLA_HEADS, MLA_NOPE + MLA_ROPE)
    zq = jnp.zeros((MLA_Q_RANK, MLA_HEADS, LANES - MLA_NOPE - MLA_ROPE), F32)
    wq = jnp.concatenate([uq, zq], axis=-1)
    wqs = jnp.concatenate([jnp.zeros_like(uq[..., :MLA_NOPE]), _swap_halves(uq[..., MLA_NOPE:]), zq], axis=-1)
    ukv = w_ukv.reshape(MLA_KV_RANK, MLA_HEADS, MLA_NOPE + MLA_V)
    zk = jnp.zeros((MLA_KV_RANK, MLA_HEADS, LANES - MLA_NOPE), F32)
    wk = jnp.concatenate([ukv[..., :MLA_NOPE], zk], axis=-1)
    wv = jnp.concatenate([ukv[..., MLA_NOPE:], jnp.zeros((MLA_KV_RANK, MLA_HEADS, LANES - MLA_V), F32)], axis=-1)
    place = np.zeros((MLA_ROPE, MLA_HEADS, LANES), np.float32)
    ones = np.zeros((1, MLA_HEADS, LANES), np.float32)
    for hh in range(MLA_HEADS):
        place[np.arange(MLA_ROPE), hh, MLA_NOPE + np.arange(MLA_ROPE)] = 1.0
        ones[0, hh, MLA_V] = 1.0
    flat = lambda a: a.reshape(a.shape[0], MLA_HEADS * LANES)
    wo_mla = w_out[:MLA_HEADS * MLA_V].reshape(MLA_HEADS, MLA_V, D_MODEL)
    wo_mla = jnp.concatenate([wo_mla, jnp.zeros((MLA_HEADS, LANES - MLA_V, D_MODEL), F32)], axis=1)
    return {
        "wcq": wcq, "wckv": wckv, "wkr": wkr, "wkrs": _swap_halves(wkr), "wdq": wdq, "wdk": wdk, "wdv": wdv,
        "qn": q_norm.reshape(1, -1), "kvn": kv_norm.reshape(1, -1),
        "wq": flat(wq).astype(BF16), "wqs": flat(wqs).astype(BF16),
        "wk": flat(wk).astype(BF16), "wv": flat(wv).astype(BF16),
        "place": jnp.asarray(flat(place), BF16), "ones": jnp.asarray(flat(ones), F32),
        "wo_mla": wo_mla.reshape(MLA_HEADS * LANES, D_MODEL).astype(BF16),
        "wo_diff": w_out[MLA_HEADS * MLA_V:].astype(BF16),
    }


def _pad_rows(a, total, front=0):
    back = total - front - a.shape[1]
    return jnp.pad(a, ((0, 0), (front, back), (0, 0)))


def _round_up(n, mult):
    return -(-n // mult) * mult


def _diff_buckets(tq, tk, q_off, sk_real, skp):
    near_back = -((q_off - (T5_MAX_DIST - 1)) // tk - q_off // tk)
    last = (_round_up(q_off + tq, CHUNK) - 1) // tk
    last = min(last, skp // tk - 1)
    n_near = last - (q_off // tk - near_back) + 1
    i = np.arange(tq)[:, None]
    mats = []
    for r in range(n_near):
        kp = (q_off // tk - near_back + r) * tk + np.arange(tk)[None, :]
        qp = q_off + i
        ok = ((kp >> CHUNK_SHIFT) <= (qp >> CHUNK_SHIFT)) & (kp < sk_real)
        bkt = _t5_bucket(jnp.asarray(kp - qp, jnp.int32))
        mats.append(jnp.where(jnp.asarray(ok), bkt, -1))
    return jnp.stack(mats).astype(jnp.int32), near_back


def _trunk(x, q_off, caches, prm, cfg):
    b, sq, _ = x.shape
    m = b * sq
    tq, tk, tm, tf = cfg["tq"], cfg["tk"], cfg["tm"], cfg["tf"]
    sk_real = q_off + sq
    skp = _round_up(sk_real, tk)
    pos = q_off + jnp.arange(sq, dtype=jnp.int32)
    tabs = {k: jnp.tile(v, (b, 1)) for k, v in _rope_tables(pos).items()}
    x2 = x.reshape(m, D_MODEL)

    def with_past(past, new, dtype):
        new = new.reshape(b, sq, -1)
        if past is None:
            return new.astype(dtype)
        return jnp.concatenate([past.reshape(b, past.shape[1], -1).astype(dtype), new.astype(dtype)], axis=1)

    ew = prm["even"]
    qext, ckv, kr, dq, dk, dkb, dv, dvb = _even_proj(x2, prm["norm_mix"][0:1], ew, tabs, tm)
    past = (None,) * 4 if caches is None else tuple(c[0] for c in caches[:4])
    ckv_all = _pad_rows(with_past(past[0], ckv, F32), skp)
    kr_all = _pad_rows(with_past(past[1], kr, F32), skp)
    kext, vext = _kv_up(ckv_all.reshape(b * skp, -1), kr_all.reshape(b * skp, -1), ew, cfg["tm_kv"])
    kext = kext.reshape(b, skp, -1)
    vext = vext.reshape(b, skp, -1)
    tqs, n_sub = cfg["tqs"], cfg["n_sub"]
    assert n_sub == 1 or tqs == tk
    n_diag = (_round_up(q_off + tqs * n_sub, CHUNK) - 1) // tk - q_off // tk + 1
    tiles = dict(tqs=tqs, n_sub=n_sub, n_diag=n_diag, tkw=cfg["tkw"], tkn=tk, q_off=q_off)
    o_mla = _mla_attn(qext.reshape(b, sq, -1), kext, vext, hs=cfg["mla_heads"], kv_buffers=cfg["kv_buffers"],
                      sk_real=sk_real, **tiles)
    dk_all = _pad_rows(with_past(past[2], dkb, BF16), skp)
    past_dv = past[3]
    if past_dv is not None:
        past_dv = jnp.concatenate([past_dv.astype(BF16), jnp.ones(past_dv.shape, BF16)], axis=-1)
    dv_all = _pad_rows(with_past(past_dv, dvb, BF16), skp)
    bkt, near_back = _diff_buckets(tqs, tk, q_off, sk_real, skp)
    assert bkt.shape[0] == near_back + n_diag - (n_sub - 1)
    lam_init = 0.8 - 0.6 * math.exp(-0.3 * 0)
    o_diff = _diff_attn(prm["t5"], dq.reshape(b, sq, -1), dk_all, dv_all, bkt, prm["lam_vecs"], prm["subln"],
                        hs=cfg["diff_heads"], near_back=near_back, far_bucket=T5_BUCKETS // 2 - 1,
                        lam_init=lam_init, **tiles)
    x2 = _post(x2, o_mla.reshape(m, -1), o_diff.reshape(m, -1), ew["wo_mla"], ew["wo_diff"],
               prm["norm_ff"][0:1], prm["w_ff1"][0], prm["w_ff2"][0], prm["final_norm"],
               tm=cfg["tm_post"], tf=tf, final_norm=False)
    new_even = (ckv.reshape(1, b, sq, MLA_KV_RANK), kr.reshape(1, b, sq, MLA_ROPE),
                dk.reshape(1, b, sq, DIFF_HEADS, 2 * DIFF_QK), dv.reshape(1, b, sq, DIFF_HEADS, DIFF_V))

    sq_, sk_, skb, sv_, svb, cq, ck, ckb, cv, cvb = _odd_proj(x2, prm["norm_mix"][1:2], prm["w_in_odd"], tm)
    past = (None,) * 4 if caches is None else tuple(c[0] for c in caches[4:])
    sk_all = _pad_rows(with_past(past[0], skb, BF16), skp)
    sv_all = _pad_rows(with_past(past[1], svb, BF16), skp)
    n_masked = (q_off + tq - 1) // tk - q_off // tk + 1
    pairs = dict(hp=cfg["head_pairs"], kv_buffers=cfg["kv_buffers"])
    o_sb = _sb_attn(sq_.reshape(b, sq, -1), sk_all, sv_all, prm["tri"][tk], tq=tq, tk=tk, q_off=q_off,
                    n_masked=n_masked, **pairs)
    win_real = tq + CA_BAND
    win = _round_up(win_real, LANES)
    if caches is None:
        ck_all = _pad_rows(ckb.reshape(b, sq, -1), sq + CA_BAND + win - win_real, front=CA_BAND)
        cv_all = _pad_rows(cvb.reshape(b, sq, -1), sq + CA_BAND + win - win_real, front=CA_BAND)
    else:
        ck_all = _pad_rows(with_past(past[2], ckb, BF16), win)
        cv_all = _pad_rows(with_past(past[3], cvb, BF16), win)
    o_ca = _ca_attn(prm["ca_tab_t"], cq.reshape(b, sq, -1), ck_all, cv_all, tq=tq, win=win, win_real=win_real,
                    q_off=q_off, **pairs)
    x2 = _post(x2, o_sb.reshape(m, -1), o_ca.reshape(m, -1), prm["wo_sb"], prm["wo_ca"],
               prm["norm_ff"][1:2], prm["w_ff1"][1], prm["w_ff2"][1], prm["final_norm"],
               tm=cfg["tm_post"], tf=tf, final_norm=True)

    heads = lambda a: a.reshape(b, sq, SB_HEADS, SB_DIM)
    if caches is None:
        nb = min(CA_BAND, sq)
        cak, cav = heads(ck)[:, sq - nb:], heads(cv)[:, sq - nb:]
    else:
        nb = past[2].shape[1]
        cak = jnp.concatenate([past[2], heads(ck)], axis=1)[:, sq:]
        cav = jnp.concatenate([past[3], heads(cv)], axis=1)[:, sq:]
        assert cak.shape[1] == nb
    new_odd = (heads(sk_)[None], heads(sv_)[None], cak[None], cav[None])
    return x2.reshape(b, sq, D_MODEL), new_even + new_odd


def _tri(tk):
    j = np.arange(tk)[:, None]
    s = np.arange(tk)[None, :]
    return jnp.asarray((j > s).astype(np.float32), BF16)


def kernel(x_prompt, x_sample, cache_mla_ckv, cache_mla_krope, cache_diff_k, cache_diff_v, cache_sb_k, cache_sb_v, cache_ca_k, cache_ca_v, norm_mix, norm_ff, w_in_even, mla_q_norm, mla_kv_norm, mla_w_uq, mla_w_ukv, diff_lambda_vecs, diff_subln, t5_bias, w_out_even, w_in_odd, ca_rel_bias, w_out_odd, w_ff1, w_ff2, final_norm):
    seq = x_prompt.shape[1]
    dec_seq = x_sample.shape[1]
    past_len = cache_mla_ckv.shape[2]
    assert cache_ca_k.shape[2] == CA_BAND and past_len % CHUNK == 0

    cfg_p = {"tq": 256, "tqs": 256, "n_sub": 4, "tk": 256, "tkw": (2048, 1024), "mla_heads": 2, "diff_heads": 1,
             "head_pairs": 2, "kv_buffers": 1,
             "tm": min(512, seq), "tm_kv": min(512, seq), "tm_post": min(1024, seq), "tf": 1024}
    rows_s = x_sample.shape[0] * dec_seq
    cfg_s = {"tq": dec_seq, "tqs": dec_seq, "n_sub": 1, "tk": 128, "tkw": (512,), "mla_heads": MLA_HEADS,
             "diff_heads": DIFF_HEADS, "head_pairs": SB_HEADS // 2, "kv_buffers": None, "tm": rows_s, "tm_kv": x_sample.shape[0] * 128 // 2, "tm_post": rows_s,
             "tf": 512}
    n_sb = SB_HEADS * SB_DIM
    prm = {
        "norm_mix": norm_mix, "norm_ff": norm_ff, "final_norm": final_norm.reshape(1, -1),
        "even": _even_weights(w_in_even[0], mla_q_norm[0], mla_kv_norm[0], mla_w_uq[0], mla_w_ukv[0],
                              w_out_even[0]),
        "t5": t5_bias, "lam_vecs": diff_lambda_vecs[0], "subln": diff_subln[0].reshape(1, -1),
        "w_in_odd": w_in_odd[0].astype(BF16), "ca_tab_t": ca_rel_bias[0].T,
        "wo_sb": w_out_odd[0][:n_sb].astype(BF16), "wo_ca": w_out_odd[0][n_sb:].astype(BF16),
        "w_ff1": w_ff1.astype(BF16), "w_ff2": w_ff2.astype(BF16),
        "tri": {tk: _tri(tk) for tk in {cfg_p["tk"], cfg_s["tk"]}},
    }
    y_prompt, new_p = _trunk(x_prompt, 0, None, prm, cfg_p)
    caches = (cache_mla_ckv, cache_mla_krope, cache_diff_k, cache_diff_v,
              cache_sb_k, cache_sb_v, cache_ca_k, cache_ca_v)
    y_sample, new_s = _trunk(x_sample, past_len, caches, prm, cfg_s)
    return (y_prompt, y_sample) + tuple(new_p) + tuple(new_s)
```

```python
import functools
import math

import numpy as np
import jax
import jax.numpy as jnp
from jax import lax
from jax.experimental import pallas as pl
from jax.experimental.pallas import tpu as pltpu

F32 = jnp.float32
BF16 = jnp.bfloat16

D_MODEL = 1024
CHUNK = 64
CHUNK_SHIFT = 6
EPS = 1e-6
NEG = -1e30

MLA_HEADS = 8
MLA_Q_RANK = 256
MLA_KV_RANK = 128
MLA_NOPE = 64
MLA_ROPE = 32
MLA_V = 64
ROPE_BASE = 10000.0
DIFF_HEADS = 4
DIFF_QK = 64
DIFF_V = 2 * DIFF_QK
T5_BUCKETS = 32
T5_MAX_DIST = 128
SB_HEADS = 8
SB_DIM = 64
CA_HEADS = 8
CA_DIM = 64
CA_LEFT_CHUNKS = 8
CA_BAND = CA_LEFT_CHUNKS * CHUNK
CA_MAX_REL = 128

LANES = 128
VMEM_LIMIT = 48 * 1024 * 1024
LOG2E = math.log2(math.e)
SB_EXIT = -104.0


def _cparams(n_axes):
    return pltpu.CompilerParams(dimension_semantics=("arbitrary",) * n_axes,
                                vmem_limit_bytes=VMEM_LIMIT)


def _rms(x, g):
    return x * lax.rsqrt(jnp.mean(x * x, axis=-1, keepdims=True) + EPS) * g


def _dot(a, b):
    return jnp.dot(a, b, preferred_element_type=F32)


def _dot_nt(a, b):
    return lax.dot_general(a, b, (((1,), (1,)), ((), ())), preferred_element_type=F32)


def _keep_lanes(q, keep):
    return jnp.where(keep, q.astype(F32), 0.0).astype(BF16)


def _full(shape):
    n = len(shape)
    return pl.BlockSpec(shape, lambda *_: (0,) * n)


def _rows(tm, width):
    return pl.BlockSpec((tm, width), lambda i: (i, 0))


def _even_proj_kernel(x_ref, g_ref, wcq_ref, wckv_ref, wkr_ref, wkrs_ref, wdq_ref, wdk_ref, wdv_ref,
                      qn_ref, kvn_ref, wq_ref, wqs_ref, cosq_ref, sinq_ref, cosk_ref, sink_ref,
                      qext_ref, ckv_ref, kr_ref, dq_ref, dk_ref, dkb_ref, dv_ref, dvb_ref):
    hn = _rms(x_ref[...], g_ref[...]).astype(BF16)
    cq = _rms(_dot(hn, wcq_ref[...]), qn_ref[...]).astype(BF16)
    cosq = jnp.concatenate([cosq_ref[...]] * MLA_HEADS, axis=1)
    sinq = jnp.concatenate([sinq_ref[...]] * MLA_HEADS, axis=1)
    qext_ref[...] = (_dot(cq, wq_ref[...]) * cosq + _dot(cq, wqs_ref[...]) * sinq).astype(BF16)
    ckv_ref[...] = _rms(_dot(hn, wckv_ref[...]), kvn_ref[...])
    kr_ref[...] = _dot(hn, wkr_ref[...]) * cosk_ref[...] + _dot(hn, wkrs_ref[...]) * sink_ref[...]
    dq_ref[...] = (_dot(hn, wdq_ref[...]) * (DIFF_QK ** -0.5 * LOG2E)).astype(BF16)
    dk = _dot(hn, wdk_ref[...])
    dk_ref[...] = dk
    dkb_ref[...] = dk.astype(BF16)
    dv = _dot(hn, wdv_ref[...])
    dv_ref[...] = dv
    dvb = dv.astype(BF16)
    ones = jnp.ones((dvb.shape[0], DIFF_V), BF16)
    dvb_ref[...] = jnp.concatenate(
        [piece for hh in range(DIFF_HEADS) for piece in (dvb[:, hh * DIFF_V:(hh + 1) * DIFF_V], ones)], axis=1)


def _even_proj(x, g, w, tabs, tm):
    m = x.shape[0]
    ins = [x, g, w["wcq"], w["wckv"], w["wkr"], w["wkrs"], w["wdq"], w["wdk"], w["wdv"],
           w["qn"], w["kvn"], w["wq"], w["wqs"], tabs["cosq"], tabs["sinq"], tabs["cosk"], tabs["sink"]]
    row_in = {0: D_MODEL, 13: LANES, 14: LANES, 15: MLA_ROPE, 16: MLA_ROPE}
    in_specs = [_rows(tm, row_in[i]) if i in row_in else _full(a.shape) for i, a in enumerate(ins)]
    outs = [(MLA_HEADS * LANES, BF16), (MLA_KV_RANK, F32), (MLA_ROPE, F32),
            (DIFF_HEADS * DIFF_V, BF16), (DIFF_HEADS * DIFF_V, F32), (DIFF_HEADS * DIFF_V, BF16),
            (DIFF_HEADS * DIFF_V, F32), (DIFF_HEADS * 2 * DIFF_V, BF16)]
    return pl.pallas_call(
        _even_proj_kernel,
        grid=(m // tm,),
        in_specs=in_specs,
        out_specs=[_rows(tm, n) for n, _ in outs],
        out_shape=[jax.ShapeDtypeStruct((m, n), dt) for n, dt in outs],
        compiler_params=_cparams(1),
        name="even_proj",
    )(*ins)


def _odd_proj_kernel(x_ref, g_ref, w_ref, sq_ref, sk_ref, skb_ref, sv_ref, svb_ref,
                     cq_ref, ck_ref, ckb_ref, cv_ref, cvb_ref):
    hn = _rms(x_ref[...], g_ref[...]).astype(BF16)
    width = SB_HEADS * SB_DIM

    def seg(i):
        return _dot(hn, w_ref[:, i * width:(i + 1) * width])

    sq_ref[...] = (seg(0) * (SB_DIM ** -0.5)).astype(BF16)
    for i, (f_ref, b_ref) in ((1, (sk_ref, skb_ref)), (2, (sv_ref, svb_ref)),
                              (4, (ck_ref, ckb_ref)), (5, (cv_ref, cvb_ref))):
        y = seg(i)
        f_ref[...] = y
        b_ref[...] = y.astype(BF16)
    cq_ref[...] = (seg(3) * (CA_DIM ** -0.5)).astype(BF16)


def _odd_proj(x, g, w, tm):
    m = x.shape[0]
    width = SB_HEADS * SB_DIM
    dts = [BF16, F32, BF16, F32, BF16, BF16, F32, BF16, F32, BF16]
    return pl.pallas_call(
        _odd_proj_kernel,
        grid=(m // tm,),
        in_specs=[_rows(tm, D_MODEL), _full(g.shape), _full(w.shape)],
        out_specs=[_rows(tm, width) for _ in dts],
        out_shape=[jax.ShapeDtypeStruct((m, width), dt) for dt in dts],
        compiler_params=_cparams(1),
        name="odd_proj",
    )(x, g, w)


def _kv_up_kernel(ckv_ref, kr_ref, wk_ref, wv_ref, place_ref, ones_ref, kext_ref, vext_ref):
    c = ckv_ref[...].astype(BF16)
    r = kr_ref[...].astype(BF16)
    kext_ref[...] = (_dot(c, wk_ref[...]) + _dot(r, place_ref[...])).astype(BF16)
    vext_ref[...] = (_dot(c, wv_ref[...]) + ones_ref[...]).astype(BF16)


def _kv_up(ckv, kr, w, tm):
    m = ckv.shape[0]
    width = MLA_HEADS * LANES
    return pl.pallas_call(
        _kv_up_kernel,
        grid=(m // tm,),
        in_specs=[_rows(tm, MLA_KV_RANK), _rows(tm, MLA_ROPE), _full(w["wk"].shape), _full(w["wv"].shape),
                  _full(w["place"].shape), _full(w["ones"].shape)],
        out_specs=[_rows(tm, width)] * 2,
        out_shape=[jax.ShapeDtypeStruct((m, width), BF16)] * 2,
        compiler_params=_cparams(1),
        name="mla_kv_up",
    )(ckv, kr, w["wk"], w["wv"], w["place"], w["ones"])


def _attn_specs(tq, skp, width=LANES, v_width=None, kv_buffers=None):
    kw = {} if kv_buffers is None else {"pipeline_mode": pl.Buffered(kv_buffers)}
    q_spec = pl.BlockSpec((None, tq, width), lambda b, h, qi: (b, qi, h))
    k_spec = pl.BlockSpec((None, skp, width), lambda b, h, qi: (b, 0, h), **kw)
    v_spec = pl.BlockSpec((None, skp, v_width or width), lambda b, h, qi: (b, 0, h), **kw)
    return q_spec, k_spec, v_spec


def _softmax_block(s, m_ref):
    m_old = m_ref[...]
    m_new = jnp.maximum(m_old, jnp.max(s, axis=-1, keepdims=True))
    m_ref[...] = m_new
    return jnp.exp2(m_old - m_new), jnp.exp2(s - jnp.tile(m_new, (1, s.shape[1] // LANES)))


def _sweep_blocks(nb0, near_back, n_sub, n_diag, tkw, tkn, do_block):
    first = jnp.maximum(nb0 - near_back, 0)
    done = 0
    for width in tuple(tkw) + (tkn,):
        per = width // tkn
        count = (first - done) // per

        def plain(j, carry, width=width, per=per, done=done):
            do_block(pl.multiple_of((done + j * per) * tkn, width), width, None)
            return carry

        lax.fori_loop(0, count, plain, 0)
        done = done + count * per
    for c in range(-near_back, n_diag):
        kinds = []
        for r in range(n_sub):
            d = c - r
            kinds.append("skip" if d > 0 else None if d < -near_back else d + near_back)

        def special(c=c, kinds=kinds):
            do_block(pl.multiple_of((nb0 + c) * tkn, tkn), tkn, kinds)

        if c < 0:
            pl.when(nb0 + c >= 0)(special)
        else:
            special()


def _mla_kernel(q_ref, k_ref, v_ref, o_ref, m_ref, acc_ref, *, hs, tqs, n_sub, n_diag, tkw, tkn, q_off, sk_real):
    qi = pl.program_id(2)
    qpos0 = q_off + qi * (tqs * n_sub)
    m_ref[...] = jnp.full(m_ref.shape, NEG, F32)
    acc_ref[...] = jnp.zeros(acc_ref.shape, F32)

    def block(start, width, kinds):
        chains = [(hh, r) for hh in range(hs) for r in range(n_sub)
                  if kinds is None or kinds[r] != "skip"]
        cols = [slice(hh * LANES, (hh + 1) * LANES) for hh in range(hs)]
        ks = [k_ref[pl.ds(start, width), cols[hh]] for hh in range(hs)]
        vs = [v_ref[pl.ds(start, width), cols[hh]] for hh in range(hs)]
        scores = [_dot_nt(q_ref[r * tqs:(r + 1) * tqs, cols[hh]], ks[hh]) for hh, r in chains]
        probs = []
        for (hh, r), s in zip(chains, scores):
            if kinds is not None and kinds[r] is not None:
                qp = qpos0 + r * tqs + lax.broadcasted_iota(jnp.int32, (tqs, width), 0)
                kp = start + lax.broadcasted_iota(jnp.int32, (tqs, width), 1)
                ok = ((kp >> CHUNK_SHIFT) <= (qp >> CHUNK_SHIFT)) & (kp < sk_real)
                s = jnp.where(ok, s, NEG)
            alpha, p = _softmax_block(s, m_ref.at[hh, r])
            probs.append((alpha, p.astype(BF16)))
        for (hh, r), (alpha, p) in zip(chains, probs):
            acc_ref[hh, r] = acc_ref[hh, r] * alpha + _dot(p, vs[hh])

    _sweep_blocks(qpos0 // tkn, 0, n_sub, n_diag, tkw, tkn, block)

    for hh in range(hs):
        for r in range(n_sub):
            acc = acc_ref[hh, r]
            lane = lax.broadcasted_iota(jnp.int32, acc.shape, 1)
            denom = jnp.sum(jnp.where(lane == MLA_V, acc, 0.0), axis=-1, keepdims=True)
            o_ref[r * tqs:(r + 1) * tqs, hh * LANES:(hh + 1) * LANES] = (
                jnp.where(lane < MLA_V, acc / denom, 0.0).astype(BF16))


def _mla_attn(qext, kext, vext, *, hs, kv_buffers, tqs, n_sub, n_diag, tkw, tkn, q_off, sk_real):
    b, sq, _ = qext.shape
    skp = kext.shape[1]
    tq = tqs * n_sub
    q_spec, k_spec, v_spec = _attn_specs(tq, skp, width=hs * LANES, kv_buffers=kv_buffers)
    kern = functools.partial(_mla_kernel, hs=hs, tqs=tqs, n_sub=n_sub, n_diag=n_diag, tkw=tkw, tkn=tkn,
                             q_off=q_off, sk_real=sk_real)
    return pl.pallas_call(
        kern,
        grid=(b, MLA_HEADS // hs, sq // tq),
        in_specs=[q_spec, k_spec, v_spec],
        out_specs=q_spec,
        out_shape=jax.ShapeDtypeStruct((b, sq, MLA_HEADS * LANES), BF16),
        scratch_shapes=[pltpu.VMEM((hs, n_sub, tqs, LANES), F32), pltpu.VMEM((hs, n_sub, tqs, LANES), F32)],
        compiler_params=_cparams(3),
        name="mla_attn",
    )(qext, kext, vext)


def _diff_kernel(t5_ref, q_ref, k_ref, v_ref, bkt_ref, lamv_ref, subln_ref, o_ref,
                 bias_ref, m_ref, acc_ref, *, hs, tqs, n_sub, n_diag, tkw, tkn, q_off, n_near, near_back, far_bucket,
                 lam_init):
    first = (pl.program_id(0) == 0) & (pl.program_id(1) == 0) & (pl.program_id(2) == 0)

    @pl.when(first)
    def _build_bias():
        for r in range(n_near):
            bkt = bkt_ref[r]
            vals = [jnp.full((tqs, tkn), NEG, F32) for _ in range(DIFF_HEADS)]
            for t in range(T5_BUCKETS):
                hit = bkt == t
                for hh in range(DIFF_HEADS):
                    vals[hh] = jnp.where(hit, (t5_ref[t, hh] - t5_ref[far_bucket, hh]) * LOG2E, vals[hh])
            for hh in range(DIFF_HEADS):
                bias_ref[r, hh] = vals[hh]

    group = pl.program_id(1)
    qi = pl.program_id(2)
    qpos0 = q_off + qi * (tqs * n_sub)
    lane = lax.broadcasted_iota(jnp.int32, (tqs * n_sub, LANES), 1)
    qm = []
    for hh in range(hs):
        q = q_ref[:, hh * LANES:(hh + 1) * LANES]
        qm.append([_keep_lanes(q, lane < DIFF_QK), _keep_lanes(q, lane >= DIFF_QK)])
    m_ref[...] = jnp.full(m_ref.shape, NEG, F32)
    acc_ref[...] = jnp.zeros(acc_ref.shape, F32)

    def block(start, width, kinds):
        chains = [(hh, mi, r) for hh in range(hs) for r in range(n_sub) for mi in range(2)
                  if kinds is None or kinds[r] != "skip"]
        ks = [k_ref[pl.ds(start, width), hh * LANES:(hh + 1) * LANES] for hh in range(hs)]
        vs = [v_ref[pl.ds(start, width), hh * 2 * DIFF_V:(hh + 1) * 2 * DIFF_V] for hh in range(hs)]
        scores = [_dot_nt(qm[hh][mi][r * tqs:(r + 1) * tqs, :], ks[hh]) for hh, mi, r in chains]
        probs = []
        for (hh, mi, r), s in zip(chains, scores):
            if kinds is not None and kinds[r] is not None:
                s = s + bias_ref[kinds[r], group * hs + hh]
            alpha, p = _softmax_block(s, m_ref.at[hh, mi, r])
            probs.append((alpha, p.astype(BF16)))
        for (hh, mi, r), (alpha, p) in zip(chains, probs):
            acc_ref[hh, mi, r] = acc_ref[hh, mi, r] * jnp.tile(alpha, (1, 2)) + _dot(p, vs[hh])

    _sweep_blocks(qpos0 // tkn, near_back, n_sub, n_diag, tkw, tkn, block)

    lv = lamv_ref[...]
    lam = (jnp.exp(jnp.sum(lv[0:1] * lv[1:2], axis=-1, keepdims=True))
           - jnp.exp(jnp.sum(lv[2:3] * lv[3:4], axis=-1, keepdims=True)) + lam_init)
    for hh in range(hs):
        for r in range(n_sub):
            a0, a1 = acc_ref[hh, 0, r], acc_ref[hh, 1, r]
            o = a0[:, :DIFF_V] / a0[:, DIFF_V:] - lam * (a1[:, :DIFF_V] / a1[:, DIFF_V:])
            o_ref[r * tqs:(r + 1) * tqs, hh * LANES:(hh + 1) * LANES] = (
                _rms(o, subln_ref[...]) * (1.0 - lam_init)).astype(BF16)


def _diff_attn(t5, dq, dk, dv, bkt, lamv, subln, *, hs, kv_buffers, tqs, n_sub, n_diag, tkw, tkn, q_off, near_back,
               far_bucket,
               lam_init):
    b, sq, _ = dq.shape
    skp = dk.shape[1]
    n_near = bkt.shape[0]
    tq = tqs * n_sub
    q_spec, k_spec, v_spec = _attn_specs(tq, skp, width=hs * LANES, v_width=hs * 2 * DIFF_V, kv_buffers=kv_buffers)
    kern = functools.partial(_diff_kernel, hs=hs, tqs=tqs, n_sub=n_sub, n_diag=n_diag, tkw=tkw, tkn=tkn,
                             q_off=q_off, n_near=n_near, near_back=near_back, far_bucket=far_bucket,
                             lam_init=lam_init)
    return pl.pallas_call(
        kern,
        grid=(b, DIFF_HEADS // hs, sq // tq),
        in_specs=[pl.BlockSpec(memory_space=pltpu.SMEM), q_spec, k_spec, v_spec,
                  _full(bkt.shape), _full(lamv.shape), _full(subln.shape)],
        out_specs=q_spec,
        out_shape=jax.ShapeDtypeStruct((b, sq, DIFF_HEADS * DIFF_V), BF16),
        scratch_shapes=[pltpu.VMEM((n_near, DIFF_HEADS, tqs, tkn), F32), pltpu.VMEM((hs, 2, n_sub, tqs, LANES), F32),
                        pltpu.VMEM((hs, 2, n_sub, tqs, 2 * DIFF_V), F32)],
        compiler_params=_cparams(3),
        name="diff_attn",
    )(t5, dq, dk, dv, bkt, lamv, subln)


def _sb_kernel(q_ref, k_ref, v_ref, tri_ref, o_ref, run_ref, acc_ref, *, hp, tq, tk, q_off, n_masked):
    qi = pl.program_id(2)
    qpos0 = q_off + qi * tq
    top = (qpos0 + tq - 1) // tk
    lane = lax.broadcasted_iota(jnp.int32, (tq, LANES), 1)
    tri = tri_ref[...]
    qhs = []
    for pp in range(hp):
        qpair = q_ref[:, pp * LANES:(pp + 1) * LANES]
        qhs.append([_keep_lanes(qpair, lane < SB_DIM), _keep_lanes(qpair, lane >= SB_DIM)])

    def block(kj, masked, valid=None):
        if valid is not None:
            kj = jnp.maximum(kj, 0)
        ks = pl.multiple_of(kj * tk, tk)
        if masked:
            qp = qpos0 + lax.broadcasted_iota(jnp.int32, (tq, tk), 0)
            kp = kj * tk + lax.broadcasted_iota(jnp.int32, (tq, tk), 1)
            ok = kp < qp
        chains = [(pp, hh) for pp in range(hp) for hh in range(2)]
        ks_ = [k_ref[pl.ds(ks, tk), pp * LANES:(pp + 1) * LANES] for pp in range(hp)]
        vs_ = [v_ref[pl.ds(ks, tk), pp * LANES:(pp + 1) * LANES] for pp in range(hp)]
        zs = [_dot_nt(qhs[pp][hh], ks_[pp]) for pp, hh in chains]
        sps, log_betas = [], []
        for z in zs:
            sp = jnp.maximum(z, 0.0) + jnp.log(1.0 + jnp.exp2(jnp.abs(z) * -LOG2E))
            log_betas.append(z - sp)
            if masked:
                sp = jnp.where(ok, sp, 0.0)
            if valid is not None:
                sp = jnp.where(valid, sp, 0.0)
            sps.append(sp)
        laters = [_dot(sp.astype(BF16), tri) for sp in sps]
        probs = []
        for (pp, hh), log_beta, later in zip(chains, log_betas, laters):
            a = jnp.exp2(((log_beta - run_ref[pp, hh]) - later) * LOG2E)
            if masked:
                a = jnp.where(ok, a, 0.0)
            if valid is not None:
                a = jnp.where(valid, a, 0.0)
            probs.append(a.astype(BF16))
        for (pp, hh), a, sp in zip(chains, probs, sps):
            acc_ref[pp, hh] += _dot(a, vs_[pp])
            run_ref[pp, hh] += jnp.sum(sp, axis=-1, keepdims=True)

    run_ref[...] = jnp.zeros(run_ref.shape, F32)
    acc_ref[...] = jnp.zeros(acc_ref.shape, F32)
    for r in range(n_masked):
        block(top - r, True)
    nxt = top - n_masked
    block(nxt, False, valid=nxt >= 0)

    def cond(kj):
        return (kj >= 0) & (jnp.min(run_ref[...]) < -SB_EXIT)

    def body(kj):
        block(kj, False)
        return kj - 1

    lax.while_loop(cond, body, nxt - 1)
    for pp in range(hp):
        o_ref[:, pp * LANES:(pp + 1) * LANES] = jnp.where(lane < SB_DIM, acc_ref[pp, 0], acc_ref[pp, 1]).astype(BF16)


def _sb_attn(sq_, sk_, sv_, tri, *, hp, kv_buffers, tq, tk, q_off, n_masked):
    b, sq, _ = sq_.shape
    skp = sk_.shape[1]
    q_spec, kv_spec, _ = _attn_specs(tq, skp, width=hp * LANES, kv_buffers=kv_buffers)
    kern = functools.partial(_sb_kernel, hp=hp, tq=tq, tk=tk, q_off=q_off, n_masked=n_masked)
    return pl.pallas_call(
        kern,
        grid=(b, SB_HEADS // (2 * hp), sq // tq),
        in_specs=[q_spec, kv_spec, kv_spec, _full(tri.shape)],
        out_specs=q_spec,
        out_shape=jax.ShapeDtypeStruct((b, sq, SB_HEADS * SB_DIM), BF16),
        scratch_shapes=[pltpu.VMEM((hp, 2, tq, 1), F32), pltpu.VMEM((hp, 2, tq, LANES), F32)],
        compiler_params=_cparams(3),
        name="sb_attn",
    )(sq_, sk_, sv_, tri)


CA_ROW_GROUP = 16


def _ca_kernel(tab_ref, q_ref, k_ref, v_ref, o_ref, bias_ref, *, hp, tq, win, win_real, q_off):
    first = (pl.program_id(0) == 0) & (pl.program_id(1) == 0) & (pl.program_id(2) == 0)
    n_shift = -(-(tq + win) // LANES) * LANES
    n_rel = n_shift + LANES

    @pl.when(first)
    def _build_bias():
        x = lax.broadcasted_iota(jnp.int32, (CA_HEADS, n_rel), 1)
        idx = jnp.clip(x - (tq - 1) - CA_BAND, -CA_MAX_REL, CA_MAX_REL) + CA_MAX_REL
        f = jnp.zeros((CA_HEADS, n_rel), F32)
        for t in range(2 * CA_MAX_REL + 1):
            f = jnp.where(idx == t, tab_ref[:, t:t + 1], f)
        g = CA_ROW_GROUP
        i_loc = lax.broadcasted_iota(jnp.int32, (g, win), 0)
        j_loc = lax.broadcasted_iota(jnp.int32, (g, win), 1)
        for hh in range(CA_HEADS):
            fh = f[hh:hh + 1, :]
            shifted = jnp.concatenate([fh[:, g - 1 - bb:g - 1 - bb + n_shift] for bb in range(g)], axis=0)
            for a in range(tq // g):
                start = tq - g * a - g
                tile = shifted[:, start:start + win]
                i = i_loc + g * a
                kc = (j_loc >> CHUNK_SHIFT) - CA_LEFT_CHUNKS
                qc = i >> CHUNK_SHIFT
                ok = (kc <= qc) & (kc >= qc - CA_LEFT_CHUNKS) & (j_loc < win_real)
                bias_ref[hh, g * a:g * a + g, :] = jnp.where(ok, tile, NEG)

    group = pl.program_id(1)
    qi = pl.program_id(2)
    qpos0 = q_off + qi * tq
    ws = pl.multiple_of(qi * tq, tq)
    lane = lax.broadcasted_iota(jnp.int32, (tq, LANES), 1)
    kpos = qpos0 - CA_BAND + lax.broadcasted_iota(jnp.int32, (tq, win), 1)
    for pp in range(hp):
        cols = slice(pp * LANES, (pp + 1) * LANES)
        q = q_ref[:, cols]
        k = k_ref[pl.ds(ws, win), cols]
        v = v_ref[pl.ds(ws, win), cols]
        outs = []
        for hh in range(2):
            qh = _keep_lanes(q, (lane < CA_DIM) if hh == 0 else (lane >= CA_DIM))
            s = _dot_nt(qh, k) + bias_ref[2 * (group * hp + pp) + hh]
            s = jnp.where(kpos >= 0, s, NEG)
            m = jnp.max(s, axis=-1, keepdims=True)
            p = jnp.exp(s - m)
            denom = jnp.sum(p, axis=-1, keepdims=True)
            outs.append(_dot(p.astype(BF16), v) / denom)
        o_ref[:, cols] = jnp.where(lane < CA_DIM, outs[0], outs[1]).astype(BF16)


def _ca_attn(tab_t, cq, ck, cv, *, hp, kv_buffers, tq, win, win_real, q_off):
    b, sq, _ = cq.shape
    skp = ck.shape[1]
    q_spec, kv_spec, _ = _attn_specs(tq, skp, width=hp * LANES, kv_buffers=kv_buffers)
    kern = functools.partial(_ca_kernel, hp=hp, tq=tq, win=win, win_real=win_real, q_off=q_off)
    return pl.pallas_call(
        kern,
        grid=(b, CA_HEADS // (2 * hp), sq // tq),
        in_specs=[_full(tab_t.shape), q_spec, kv_spec, kv_spec],
        out_specs=q_spec,
        out_shape=jax.ShapeDtypeStruct((b, sq, CA_HEADS * CA_DIM), BF16),
        scratch_shapes=[pltpu.VMEM((CA_HEADS, tq, win), F32)],
        compiler_params=_cparams(3),
        name="ca_attn",
    )(tab_t, cq, ck, cv)


def _post_kernel(x_ref, oa_ref, ob_ref, wa_ref, wb_ref, g_ref, w1_ref, w2_ref, gf_ref, y_ref,
                 x1_ref, hn_ref, acc_ref, *, final_norm):
    j = pl.program_id(1)

    @pl.when(j == 0)
    def _mix():
        x1 = x_ref[...] + _dot(oa_ref[...], wa_ref[...]) + _dot(ob_ref[...], wb_ref[...])
        x1_ref[...] = x1
        hn_ref[...] = _rms(x1, g_ref[...]).astype(BF16)
        acc_ref[...] = jnp.zeros(acc_ref.shape, F32)

    a = jnp.maximum(_dot(hn_ref[...], w1_ref[...]), 0.0)
    acc_ref[...] += _dot((a * a).astype(BF16), w2_ref[...])

    @pl.when(j == pl.num_programs(1) - 1)
    def _finish():
        y = x1_ref[...] + acc_ref[...]
        if final_norm:
            y = _rms(y, gf_ref[...])
        y_ref[...] = y


def _post(x, oa, ob, wa, wb, g, w1, w2, gf, *, tm, tf, final_norm):
    m = x.shape[0]
    d_ff = w1.shape[1]
    rows = lambda width: pl.BlockSpec((tm, width), lambda i, j: (i, 0))
    return pl.pallas_call(
        functools.partial(_post_kernel, final_norm=final_norm),
        grid=(m // tm, d_ff // tf),
        in_specs=[rows(D_MODEL), rows(oa.shape[1]), rows(ob.shape[1]), _full(wa.shape), _full(wb.shape),
                  _full(g.shape), pl.BlockSpec((D_MODEL, tf), lambda i, j: (0, j)),
                  pl.BlockSpec((tf, D_MODEL), lambda i, j: (j, 0)), _full(gf.shape)],
        out_specs=rows(D_MODEL),
        out_shape=jax.ShapeDtypeStruct((m, D_MODEL), F32),
        scratch_shapes=[pltpu.VMEM((tm, D_MODEL), F32), pltpu.VMEM((tm, D_MODEL), BF16),
                        pltpu.VMEM((tm, D_MODEL), F32)],
        compiler_params=_cparams(2),
        name="post_mlp",
    )(x, oa, ob, wa, wb, g, w1, w2, gf)


def _t5_bucket(rel):
    nb = T5_BUCKETS // 2
    max_exact = nb // 2
    ret = jnp.where(rel > 0, nb, 0)
    n = jnp.abs(rel)
    nf = jnp.maximum(n, 1).astype(F32)
    large = max_exact + (jnp.log(nf / max_exact) / math.log(T5_MAX_DIST / max_exact) * (nb - max_exact)).astype(jnp.int32)
    large = jnp.minimum(large, nb - 1)
    return ret + jnp.where(n < max_exact, n, large)


def _rope_tables(pos):
    half = MLA_ROPE // 2
    inv = ROPE_BASE ** (-jnp.arange(half, dtype=F32) / half)
    ang = pos.astype(F32)[:, None] * inv[None, :]
    cos, sin = jnp.cos(ang), jnp.sin(ang)
    cosk = jnp.concatenate([cos, cos], axis=1)
    sink = jnp.concatenate([-sin, sin], axis=1)
    n = pos.shape[0]
    pad = jnp.zeros((n, LANES - MLA_NOPE - MLA_ROPE), F32)
    qscale = (MLA_NOPE + MLA_ROPE) ** -0.5 * LOG2E
    cosq = jnp.concatenate([jnp.ones((n, MLA_NOPE), F32), cosk, pad], axis=1) * qscale
    sinq = jnp.concatenate([jnp.zeros((n, MLA_NOPE), F32), sink, pad], axis=1) * qscale
    return {"cosq": cosq, "sinq": sinq, "cosk": cosk, "sink": sink}


def _swap_halves(w):
    half = w.shape[-1] // 2
    return jnp.concatenate([w[..., half:], w[..., :half]], axis=-1)


def _even_weights(w_in, q_norm, kv_norm, w_uq, w_ukv, w_out):
    sizes = [MLA_Q_RANK, MLA_KV_RANK, MLA_ROPE, DIFF_HEADS * 2 * DIFF_QK, DIFF_HEADS * 2 * DIFF_QK,
             DIFF_HEADS * DIFF_V]
    offs = np.cumsum([0] + sizes)
    wcq, wckv, wkr, wdq, wdk, wdv = (w_in[:, offs[i]:offs[i + 1]].astype(BF16) for i in range(6))
    uq = w_uq.reshape(MLA_Q_RANK, MLA_HEADS, MLA_NOPE + MLA_ROPE)
    zq = jnp.zeros((MLA_Q_RANK, MLA_HEADS, LANES - MLA_NOPE - MLA_ROPE), F32)
    wq = jnp.concatenate([uq, zq], axis=-1)
    wqs = jnp.concatenate([jnp.zeros_like(uq[..., :MLA_NOPE]), _swap_halves(uq[..., MLA_NOPE:]), zq], axis=-1)
    ukv = w_ukv.reshape(MLA_KV_RANK, MLA_HEADS, MLA_NOPE + MLA_V)
    zk = jnp.zeros((MLA_KV_RANK, MLA_HEADS, LANES - MLA_NOPE), F32)
    wk = jnp.concatenate([ukv[..., :MLA_NOPE], zk], axis=-1)
    wv = jnp.concatenate([ukv[..., MLA_NOPE:], jnp.zeros((MLA_KV_RANK, MLA_HEADS, LANES - MLA_V), F32)], axis=-1)
    place = np.zeros((MLA_ROPE, MLA_HEADS, LANES), np.float32)
    ones = np.zeros((1, MLA_HEADS, LANES), np.float32)
    for hh in range(MLA_HEADS):
        place[np.arange(MLA_ROPE), hh, MLA_NOPE + np.arange(MLA_ROPE)] = 1.0
        ones[0, hh, MLA_V] = 1.0
    flat = lambda a: a.reshape(a.shape[0], MLA_HEADS * LANES)
    wo_mla = w_out[:MLA_HEADS * MLA_V].reshape(MLA_HEADS, MLA_V, D_MODEL)
    wo_mla = jnp.concatenate([wo_mla, jnp.zeros((MLA_HEADS, LANES - MLA_V, D_MODEL), F32)], axis=1)
    return {
        "wcq": wcq, "wckv": wckv, "wkr": wkr, "wkrs": _swap_halves(wkr), "wdq": wdq, "wdk": wdk, "wdv": wdv,
        "qn": q_norm.reshape(1, -1), "kvn": kv_norm.reshape(1, -1),
        "wq": flat(wq).astype(BF16), "wqs": flat(wqs).astype(BF16),
        "wk": flat(wk).astype(BF16), "wv": flat(wv).astype(BF16),
        "place": jnp.asarray(flat(place), BF16), "ones": jnp.asarray(flat(ones), F32),
        "wo_mla": wo_mla.reshape(MLA_HEADS * LANES, D_MODEL).astype(BF16),
        "wo_diff": w_out[MLA_HEADS * MLA_V:].astype(BF16),
    }


def _pad_rows(a, total, front=0):
    back = total - front - a.shape[1]
    return jnp.pad(a, ((0, 0), (front, back), (0, 0)))


def _round_up(n, mult):
    return -(-n // mult) * mult


def _diff_buckets(tq, tk, q_off, sk_real, skp):
    near_back = -((q_off - (T5_MAX_DIST - 1)) // tk - q_off // tk)
    last = (_round_up(q_off + tq, CHUNK) - 1) // tk
    last = min(last, skp // tk - 1)
    n_near = last - (q_off // tk - near_back) + 1
    i = np.arange(tq)[:, None]
    mats = []
    for r in range(n_near):
        kp = (q_off // tk - near_back + r) * tk + np.arange(tk)[None, :]
        qp = q_off + i
        ok = ((kp >> CHUNK_SHIFT) <= (qp >> CHUNK_SHIFT)) & (kp < sk_real)
        bkt = _t5_bucket(jnp.asarray(kp - qp, jnp.int32))
        mats.append(jnp.where(jnp.asarray(ok), bkt, -1))
    return jnp.stack(mats).astype(jnp.int32), near_back


def _trunk(x, q_off, caches, prm, cfg):
    b, sq, _ = x.shape
    m = b * sq
    tq, tk, tm, tf = cfg["tq"], cfg["tk"], cfg["tm"], cfg["tf"]
    sk_real = q_off + sq
    skp = _round_up(sk_real, tk)
    pos = q_off + jnp.arange(sq, dtype=jnp.int32)
    tabs = {k: jnp.tile(v, (b, 1)) for k, v in _rope_tables(pos).items()}
    x2 = x.reshape(m, D_MODEL)

    def with_past(past, new, dtype):
        new = new.reshape(b, sq, -1)
        if past is None:
            return new.astype(dtype)
        return jnp.concatenate([past.reshape(b, past.shape[1], -1).astype(dtype), new.astype(dtype)], axis=1)

    ew = prm["even"]
    qext, ckv, kr, dq, dk, dkb, dv, dvb = _even_proj(x2, prm["norm_mix"][0:1], ew, tabs, tm)
    past = (None,) * 4 if caches is None else tuple(c[0] for c in caches[:4])
    ckv_all = _pad_rows(with_past(past[0], ckv, F32), skp)
    kr_all = _pad_rows(with_past(past[1], kr, F32), skp)
    kext, vext = _kv_up(ckv_all.reshape(b * skp, -1), kr_all.reshape(b * skp, -1), ew, cfg["tm_kv"])
    kext = kext.reshape(b, skp, -1)
    vext = vext.reshape(b, skp, -1)
    tqs, n_sub = cfg["tqs"], cfg["n_sub"]
    assert n_sub == 1 or tqs == tk
    n_diag = (_round_up(q_off + tqs * n_sub, CHUNK) - 1) // tk - q_off // tk + 1
    tiles = dict(tqs=tqs, n_sub=n_sub, n_diag=n_diag, tkw=cfg["tkw"], tkn=tk, q_off=q_off)
    o_mla = _mla_attn(qext.reshape(b, sq, -1), kext, vext, hs=cfg["mla_heads"], kv_buffers=cfg["kv_buffers"],
                      sk_real=sk_real, **tiles)
    dk_all = _pad_rows(with_past(past[2], dkb, BF16), skp)
    past_dv = past[3]
    if past_dv is not None:
        past_dv = jnp.concatenate([past_dv.astype(BF16), jnp.ones(past_dv.shape, BF16)], axis=-1)
    dv_all = _pad_rows(with_past(past_dv, dvb, BF16), skp)
    bkt, near_back = _diff_buckets(tqs, tk, q_off, sk_real, skp)
    assert bkt.shape[0] == near_back + n_diag - (n_sub - 1)
    lam_init = 0.8 - 0.6 * math.exp(-0.3 * 0)
    o_diff = _diff_attn(prm["t5"], dq.reshape(b, sq, -1), dk_all, dv_all, bkt, prm["lam_vecs"], prm["subln"],
                        hs=cfg["diff_heads"], kv_buffers=cfg["kv_buffers"], near_back=near_back,
                        far_bucket=T5_BUCKETS // 2 - 1,
                        lam_init=lam_init, **tiles)
    x2 = _post(x2, o_mla.reshape(m, -1), o_diff.reshape(m, -1), ew["wo_mla"], ew["wo_diff"],
               prm["norm_ff"][0:1], prm["w_ff1"][0], prm["w_ff2"][0], prm["final_norm"],
               tm=cfg["tm_post"], tf=tf, final_norm=False)
    new_even = (ckv.reshape(1, b, sq, MLA_KV_RANK), kr.reshape(1, b, sq, MLA_ROPE),
                dk.reshape(1, b, sq, DIFF_HEADS, 2 * DIFF_QK), dv.reshape(1, b, sq, DIFF_HEADS, DIFF_V))

    sq_, sk_, skb, sv_, svb, cq, ck, ckb, cv, cvb = _odd_proj(x2, prm["norm_mix"][1:2], prm["w_in_odd"], tm)
    past = (None,) * 4 if caches is None else tuple(c[0] for c in caches[4:])
    sk_all = _pad_rows(with_past(past[0], skb, BF16), skp)
    sv_all = _pad_rows(with_past(past[1], svb, BF16), skp)
    n_masked = (q_off + tq - 1) // tk - q_off // tk + 1
    pairs = dict(hp=cfg["head_pairs"], kv_buffers=cfg["kv_buffers"])
    o_sb = _sb_attn(sq_.reshape(b, sq, -1), sk_all, sv_all, prm["tri"][tk], tq=tq, tk=tk, q_off=q_off,
                    n_masked=n_masked, **pairs)
    win_real = tq + CA_BAND
    win = _round_up(win_real, LANES)
    if caches is None:
        ck_all = _pad_rows(ckb.reshape(b, sq, -1), sq + CA_BAND + win - win_real, front=CA_BAND)
        cv_all = _pad_rows(cvb.reshape(b, sq, -1), sq + CA_BAND + win - win_real, front=CA_BAND)
    else:
        ck_all = _pad_rows(with_past(past[2], ckb, BF16), win)
        cv_all = _pad_rows(with_past(past[3], cvb, BF16), win)
    o_ca = _ca_attn(prm["ca_tab_t"], cq.reshape(b, sq, -1), ck_all, cv_all, tq=tq, win=win, win_real=win_real,
                    q_off=q_off, **pairs)
    x2 = _post(x2, o_sb.reshape(m, -1), o_ca.reshape(m, -1), prm["wo_sb"], prm["wo_ca"],
               prm["norm_ff"][1:2], prm["w_ff1"][1], prm["w_ff2"][1], prm["final_norm"],
               tm=cfg["tm_post"], tf=tf, final_norm=True)

    heads = lambda a: a.reshape(b, sq, SB_HEADS, SB_DIM)
    if caches is None:
        nb = min(CA_BAND, sq)
        cak, cav = heads(ck)[:, sq - nb:], heads(cv)[:, sq - nb:]
    else:
        nb = past[2].shape[1]
        cak = jnp.concatenate([past[2], heads(ck)], axis=1)[:, sq:]
        cav = jnp.concatenate([past[3], heads(cv)], axis=1)[:, sq:]
        assert cak.shape[1] == nb
    new_odd = (heads(sk_)[None], heads(sv_)[None], cak[None], cav[None])
    return x2.reshape(b, sq, D_MODEL), new_even + new_odd


def _tri(tk):
    j = np.arange(tk)[:, None]
    s = np.arange(tk)[None, :]
    return jnp.asarray((j > s).astype(np.float32), BF16)


def kernel(x_prompt, x_sample, cache_mla_ckv, cache_mla_krope, cache_diff_k, cache_diff_v, cache_sb_k, cache_sb_v, cache_ca_k, cache_ca_v, norm_mix, norm_ff, w_in_even, mla_q_norm, mla_kv_norm, mla_w_uq, mla_w_ukv, diff_lambda_vecs, diff_subln, t5_bias, w_out_even, w_in_odd, ca_rel_bias, w_out_odd, w_ff1, w_ff2, final_norm):
    seq = x_prompt.shape[1]
    dec_seq = x_sample.shape[1]
    past_len = cache_mla_ckv.shape[2]
    assert cache_ca_k.shape[2] == CA_BAND and past_len % CHUNK == 0

    cfg_p = {"tq": 256, "tqs": 256, "n_sub": 4, "tk": 256, "tkw": (2048, 1024), "mla_heads": 2, "diff_heads": 1,
             "head_pairs": 2, "kv_buffers": 1,
             "tm": min(512, seq), "tm_kv": min(512, seq), "tm_post": min(1024, seq), "tf": 1024}
    rows_s = x_sample.shape[0] * dec_seq
    cfg_s = {"tq": dec_seq, "tqs": dec_seq, "n_sub": 1, "tk": 128, "tkw": (512,), "mla_heads": MLA_HEADS,
             "diff_heads": DIFF_HEADS, "head_pairs": SB_HEADS // 2, "kv_buffers": None, "tm": rows_s, "tm_kv": x_sample.shape[0] * 128 // 2, "tm_post": rows_s,
             "tf": 512}
    n_sb = SB_HEADS * SB_DIM
    prm = {
        "norm_mix": norm_mix, "norm_ff": norm_ff, "final_norm": final_norm.reshape(1, -1),
        "even": _even_weights(w_in_even[0], mla_q_norm[0], mla_kv_norm[0], mla_w_uq[0], mla_w_ukv[0],
                              w_out_even[0]),
        "t5": t5_bias, "lam_vecs": diff_lambda_vecs[0], "subln": diff_subln[0].reshape(1, -1),
        "w_in_odd": w_in_odd[0].astype(BF16), "ca_tab_t": ca_rel_bias[0].T,
        "wo_sb": w_out_odd[0][:n_sb].astype(BF16), "wo_ca": w_out_odd[0][n_sb:].astype(BF16),
        "w_ff1": w_ff1.astype(BF16), "w_ff2": w_ff2.astype(BF16),
        "tri": {tk: _tri(tk) for tk in {cfg_p["tk"], cfg_s["tk"]}},
    }
    y_prompt, new_p = _trunk(x_prompt, 0, None, prm, cfg_p)
    caches = (cache_mla_ckv, cache_mla_krope, cache_diff_k, cache_diff_v,
              cache_sb_k, cache_sb_v, cache_ca_k, cache_ca_v)
    y_sample, new_s = _trunk(x_sample, past_len, caches, prm, cfg_s)
    return (y_prompt, y_sample) + tuple(new_p) + tuple(new_s)
```

```python
import functools
import math

import numpy as np
import jax
import jax.numpy as jnp
from jax import lax
from jax.experimental import pallas as pl
from jax.experimental.pallas import tpu as pltpu

F32 = jnp.float32
BF16 = jnp.bfloat16

D_MODEL = 1024
CHUNK = 64
CHUNK_SHIFT = 6
EPS = 1e-6
NEG = -1e30

MLA_HEADS = 8
MLA_Q_RANK = 256
MLA_KV_RANK = 128
MLA_NOPE = 64
MLA_ROPE = 32
MLA_V = 64
ROPE_BASE = 10000.0
DIFF_HEADS = 4
DIFF_QK = 64
DIFF_V = 2 * DIFF_QK
T5_BUCKETS = 32
T5_MAX_DIST = 128
SB_HEADS = 8
SB_DIM = 64
CA_HEADS = 8
CA_DIM = 64
CA_LEFT_CHUNKS = 8
CA_BAND = CA_LEFT_CHUNKS * CHUNK
CA_MAX_REL = 128

LANES = 128
VMEM_LIMIT = 48 * 1024 * 1024
LOG2E = math.log2(math.e)
SB_EXIT = -104.0


def _cparams(n_axes):
    return pltpu.CompilerParams(dimension_semantics=("arbitrary",) * n_axes,
                                vmem_limit_bytes=VMEM_LIMIT)


def _rms(x, g):
    return x * lax.rsqrt(jnp.mean(x * x, axis=-1, keepdims=True) + EPS) * g


def _dot(a, b):
    return jnp.dot(a, b, preferred_element_type=F32)


def _dot_nt(a, b):
    return lax.dot_general(a, b, (((1,), (1,)), ((), ())), preferred_element_type=F32)


def _keep_lanes(q, keep):
    return jnp.where(keep, q.astype(F32), 0.0).astype(BF16)


def _full(shape):
    n = len(shape)
    return pl.BlockSpec(shape, lambda *_: (0,) * n)


def _rows(tm, width):
    return pl.BlockSpec((tm, width), lambda i: (i, 0))


def _even_proj_kernel(x_ref, g_ref, wcq_ref, wckv_ref, wkr_ref, wkrs_ref, wdq_ref, wdk_ref, wdv_ref,
                      qn_ref, kvn_ref, wq_ref, wqs_ref, cosq_ref, sinq_ref, cosk_ref, sink_ref,
                      qext_ref, ckv_ref, kr_ref, dq_ref, dk_ref, dkb_ref, dv_ref, dvb_ref):
    hn = _rms(x_ref[...], g_ref[...]).astype(BF16)
    cq = _rms(_dot(hn, wcq_ref[...]), qn_ref[...]).astype(BF16)
    cosq = jnp.concatenate([cosq_ref[...]] * MLA_HEADS, axis=1)
    sinq = jnp.concatenate([sinq_ref[...]] * MLA_HEADS, axis=1)
    qext_ref[...] = (_dot(cq, wq_ref[...]) * cosq + _dot(cq, wqs_ref[...]) * sinq).astype(BF16)
    ckv_ref[...] = _rms(_dot(hn, wckv_ref[...]), kvn_ref[...])
    kr_ref[...] = _dot(hn, wkr_ref[...]) * cosk_ref[...] + _dot(hn, wkrs_ref[...]) * sink_ref[...]
    dq_ref[...] = (_dot(hn, wdq_ref[...]) * (DIFF_QK ** -0.5 * LOG2E)).astype(BF16)
    dk = _dot(hn, wdk_ref[...])
    dk_ref[...] = dk
    dkb_ref[...] = dk.astype(BF16)
    dv = _dot(hn, wdv_ref[...])
    dv_ref[...] = dv
    dvb = dv.astype(BF16)
    ones = jnp.ones((dvb.shape[0], DIFF_V), BF16)
    dvb_ref[...] = jnp.concatenate(
        [piece for hh in range(DIFF_HEADS) for piece in (dvb[:, hh * DIFF_V:(hh + 1) * DIFF_V], ones)], axis=1)


def _even_proj(x, g, w, tabs, tm):
    m = x.shape[0]
    ins = [x, g, w["wcq"], w["wckv"], w["wkr"], w["wkrs"], w["wdq"], w["wdk"], w["wdv"],
           w["qn"], w["kvn"], w["wq"], w["wqs"], tabs["cosq"], tabs["sinq"], tabs["cosk"], tabs["sink"]]
    row_in = {0: D_MODEL, 13: LANES, 14: LANES, 15: MLA_ROPE, 16: MLA_ROPE}
    in_specs = [_rows(tm, row_in[i]) if i in row_in else _full(a.shape) for i, a in enumerate(ins)]
    outs = [(MLA_HEADS * LANES, BF16), (MLA_KV_RANK, F32), (MLA_ROPE, F32),
            (DIFF_HEADS * DIFF_V, BF16), (DIFF_HEADS * DIFF_V, F32), (DIFF_HEADS * DIFF_V, BF16),
            (DIFF_HEADS * DIFF_V, F32), (DIFF_HEADS * 2 * DIFF_V, BF16)]
    return pl.pallas_call(
        _even_proj_kernel,
        grid=(m // tm,),
        in_specs=in_specs,
        out_specs=[_rows(tm, n) for n, _ in outs],
        out_shape=[jax.ShapeDtypeStruct((m, n), dt) for n, dt in outs],
        compiler_params=_cparams(1),
        name="even_proj",
    )(*ins)


def _odd_proj_kernel(x_ref, g_ref, w_ref, sq_ref, sk_ref, skb_ref, sv_ref, svb_ref,
                     cq_ref, ck_ref, ckb_ref, cv_ref, cvb_ref):
    hn = _rms(x_ref[...], g_ref[...]).astype(BF16)
    width = SB_HEADS * SB_DIM

    def seg(i):
        return _dot(hn, w_ref[:, i * width:(i + 1) * width])

    sq_ref[...] = (seg(0) * (SB_DIM ** -0.5)).astype(BF16)
    for i, (f_ref, b_ref) in ((1, (sk_ref, skb_ref)), (2, (sv_ref, svb_ref)),
                              (4, (ck_ref, ckb_ref)), (5, (cv_ref, cvb_ref))):
        y = seg(i)
        f_ref[...] = y
        b_ref[...] = y.astype(BF16)
    cq_ref[...] = (seg(3) * (CA_DIM ** -0.5)).astype(BF16)


def _odd_proj(x, g, w, tm):
    m = x.shape[0]
    width = SB_HEADS * SB_DIM
    dts = [BF16, F32, BF16, F32, BF16, BF16, F32, BF16, F32, BF16]
    return pl.pallas_call(
        _odd_proj_kernel,
        grid=(m // tm,),
        in_specs=[_rows(tm, D_MODEL), _full(g.shape), _full(w.shape)],
        out_specs=[_rows(tm, width) for _ in dts],
        out_shape=[jax.ShapeDtypeStruct((m, width), dt) for dt in dts],
        compiler_params=_cparams(1),
        name="odd_proj",
    )(x, g, w)


def _kv_up_kernel(ckv_ref, kr_ref, wk_ref, wv_ref, place_ref, ones_ref, kext_ref, vext_ref):
    c = ckv_ref[...].astype(BF16)
    r = kr_ref[...].astype(BF16)
    kext_ref[...] = (_dot(c, wk_ref[...]) + _dot(r, place_ref[...])).astype(BF16)
    vext_ref[...] = (_dot(c, wv_ref[...]) + ones_ref[...]).astype(BF16)


def _kv_up(ckv, kr, w, tm):
    m = ckv.shape[0]
    width = MLA_HEADS * LANES
    return pl.pallas_call(
        _kv_up_kernel,
        grid=(m // tm,),
        in_specs=[_rows(tm, MLA_KV_RANK), _rows(tm, MLA_ROPE), _full(w["wk"].shape), _full(w["wv"].shape),
                  _full(w["place"].shape), _full(w["ones"].shape)],
        out_specs=[_rows(tm, width)] * 2,
        out_shape=[jax.ShapeDtypeStruct((m, width), BF16)] * 2,
        compiler_params=_cparams(1),
        name="mla_kv_up",
    )(ckv, kr, w["wk"], w["wv"], w["place"], w["ones"])


def _attn_specs(tq, skp, width=LANES, v_width=None, kv_buffers=None):
    kw = {} if kv_buffers is None else {"pipeline_mode": pl.Buffered(kv_buffers)}
    q_spec = pl.BlockSpec((None, tq, width), lambda b, h, qi: (b, qi, h))
    k_spec = pl.BlockSpec((None, skp, width), lambda b, h, qi: (b, 0, h), **kw)
    v_spec = pl.BlockSpec((None, skp, v_width or width), lambda b, h, qi: (b, 0, h), **kw)
    return q_spec, k_spec, v_spec


def _softmax_block(s, m_ref):
    m_old = m_ref[...]
    m_new = jnp.maximum(m_old, jnp.max(s, axis=-1, keepdims=True))
    m_ref[...] = m_new
    return jnp.exp2(m_old - m_new), jnp.exp2(s - jnp.tile(m_new, (1, s.shape[1] // LANES)))


def _sweep_blocks(nb0, near_back, n_sub, n_diag, tkw, tkn, do_block):
    first = jnp.maximum(nb0 - near_back, 0)
    done = 0
    for width in tuple(tkw) + (tkn,):
        per = width // tkn
        count = (first - done) // per

        def plain(j, carry, width=width, per=per, done=done):
            do_block(pl.multiple_of((done + j * per) * tkn, width), width, None)
            return carry

        lax.fori_loop(0, count, plain, 0)
        done = done + count * per
    for c in range(-near_back, n_diag):
        kinds = []
        for r in range(n_sub):
            d = c - r
            kinds.append("skip" if d > 0 else None if d < -near_back else d + near_back)

        def special(c=c, kinds=kinds):
            do_block(pl.multiple_of((nb0 + c) * tkn, tkn), tkn, kinds)

        if c < 0:
            pl.when(nb0 + c >= 0)(special)
        else:
            special()


def _mla_kernel(q_ref, k_ref, v_ref, o_ref, m_ref, acc_ref, *, hs, tqs, n_sub, n_diag, tkw, tkn, q_off, sk_real):
    qi = pl.program_id(2)
    qpos0 = q_off + qi * (tqs * n_sub)
    m_ref[...] = jnp.full(m_ref.shape, NEG, F32)
    acc_ref[...] = jnp.zeros(acc_ref.shape, F32)

    def block(start, width, kinds):
        chains = [(hh, r) for hh in range(hs) for r in range(n_sub)
                  if kinds is None or kinds[r] != "skip"]
        cols = [slice(hh * LANES, (hh + 1) * LANES) for hh in range(hs)]
        ks = [k_ref[pl.ds(start, width), cols[hh]] for hh in range(hs)]
        vs = [v_ref[pl.ds(start, width), cols[hh]] for hh in range(hs)]
        scores = [_dot_nt(q_ref[r * tqs:(r + 1) * tqs, cols[hh]], ks[hh]) for hh, r in chains]
        probs = []
        for (hh, r), s in zip(chains, scores):
            if kinds is not None and kinds[r] is not None:
                qp = qpos0 + r * tqs + lax.broadcasted_iota(jnp.int32, (tqs, width), 0)
                kp = start + lax.broadcasted_iota(jnp.int32, (tqs, width), 1)
                ok = ((kp >> CHUNK_SHIFT) <= (qp >> CHUNK_SHIFT)) & (kp < sk_real)
                s = jnp.where(ok, s, NEG)
            alpha, p = _softmax_block(s, m_ref.at[hh, r])
            probs.append((alpha, p.astype(BF16)))
        for (hh, r), (alpha, p) in zip(chains, probs):
            acc_ref[hh, r] = acc_ref[hh, r] * alpha + _dot(p, vs[hh])

    _sweep_blocks(qpos0 // tkn, 0, n_sub, n_diag, tkw, tkn, block)

    for hh in range(hs):
        for r in range(n_sub):
            acc = acc_ref[hh, r]
            lane = lax.broadcasted_iota(jnp.int32, acc.shape, 1)
            denom = jnp.sum(jnp.where(lane == MLA_V, acc, 0.0), axis=-1, keepdims=True)
            o_ref[r * tqs:(r + 1) * tqs, hh * LANES:(hh + 1) * LANES] = (
                jnp.where(lane < MLA_V, acc / denom, 0.0).astype(BF16))


def _mla_attn(qext, kext, vext, *, hs, kv_buffers, tqs, n_sub, n_diag, tkw, tkn, q_off, sk_real):
    b, sq, _ = qext.shape
    skp = kext.shape[1]
    tq = tqs * n_sub
    q_spec, k_spec, v_spec = _attn_specs(tq, skp, width=hs * LANES, kv_buffers=kv_buffers)
    kern = functools.partial(_mla_kernel, hs=hs, tqs=tqs, n_sub=n_sub, n_diag=n_diag, tkw=tkw, tkn=tkn,
                             q_off=q_off, sk_real=sk_real)
    return pl.pallas_call(
        kern,
        grid=(b, MLA_HEADS // hs, sq // tq),
        in_specs=[q_spec, k_spec, v_spec],
        out_specs=q_spec,
        out_shape=jax.ShapeDtypeStruct((b, sq, MLA_HEADS * LANES), BF16),
        scratch_shapes=[pltpu.VMEM((hs, n_sub, tqs, LANES), F32), pltpu.VMEM((hs, n_sub, tqs, LANES), F32)],
        compiler_params=_cparams(3),
        name="mla_attn",
    )(qext, kext, vext)


def _diff_kernel(t5_ref, q_ref, k_ref, v_ref, bkt_ref, lamv_ref, subln_ref, o_ref,
                 bias_ref, m_ref, acc_ref, *, hs, tqs, n_sub, n_diag, tkw, tkn, q_off, n_near, near_back, far_bucket,
                 lam_init):
    first = (pl.program_id(0) == 0) & (pl.program_id(1) == 0) & (pl.program_id(2) == 0)

    @pl.when(first)
    def _build_bias():
        for r in range(n_near):
            bkt = bkt_ref[r]
            vals = [jnp.full((tqs, tkn), NEG, F32) for _ in range(DIFF_HEADS)]
            for t in range(T5_BUCKETS):
                hit = bkt == t
                for hh in range(DIFF_HEADS):
                    vals[hh] = jnp.where(hit, (t5_ref[t, hh] - t5_ref[far_bucket, hh]) * LOG2E, vals[hh])
            for hh in range(DIFF_HEADS):
                bias_ref[r, hh] = vals[hh]

    group = pl.program_id(1)
    qi = pl.program_id(2)
    qpos0 = q_off + qi * (tqs * n_sub)
    lane = lax.broadcasted_iota(jnp.int32, (tqs * n_sub, LANES), 1)
    qm = []
    for hh in range(hs):
        q = q_ref[:, hh * LANES:(hh + 1) * LANES]
        qm.append([_keep_lanes(q, lane < DIFF_QK), _keep_lanes(q, lane >= DIFF_QK)])
    m_ref[...] = jnp.full(m_ref.shape, NEG, F32)
    acc_ref[...] = jnp.zeros(acc_ref.shape, F32)

    def block(start, width, kinds):
        chains = [(hh, mi, r) for hh in range(hs) for r in range(n_sub) for mi in range(2)
                  if kinds is None or kinds[r] != "skip"]
        ks = [k_ref[pl.ds(start, width), hh * LANES:(hh + 1) * LANES] for hh in range(hs)]
        vs = [v_ref[pl.ds(start, width), hh * 2 * DIFF_V:(hh + 1) * 2 * DIFF_V] for hh in range(hs)]
        scores = [_dot_nt(qm[hh][mi][r * tqs:(r + 1) * tqs, :], ks[hh]) for hh, mi, r in chains]
        probs = []
        for (hh, mi, r), s in zip(chains, scores):
            if kinds is not None and kinds[r] is not None:
                s = s + bias_ref[kinds[r], group * hs + hh]
            alpha, p = _softmax_block(s, m_ref.at[hh, mi, r])
            probs.append((alpha, p.astype(BF16)))
        for (hh, mi, r), (alpha, p) in zip(chains, probs):
            acc_ref[hh, mi, r] = acc_ref[hh, mi, r] * jnp.tile(alpha, (1, 2)) + _dot(p, vs[hh])

    _sweep_blocks(qpos0 // tkn, near_back, n_sub, n_diag, tkw, tkn, block)

    lv = lamv_ref[...]
    lam = (jnp.exp(jnp.sum(lv[0:1] * lv[1:2], axis=-1, keepdims=True))
           - jnp.exp(jnp.sum(lv[2:3] * lv[3:4], axis=-1, keepdims=True)) + lam_init)
    for hh in range(hs):
        for r in range(n_sub):
            a0, a1 = acc_ref[hh, 0, r], acc_ref[hh, 1, r]
            o = a0[:, :DIFF_V] / a0[:, DIFF_V:] - lam * (a1[:, :DIFF_V] / a1[:, DIFF_V:])
            o_ref[r * tqs:(r + 1) * tqs, hh * LANES:(hh + 1) * LANES] = (
                _rms(o, subln_ref[...]) * (1.0 - lam_init)).astype(BF16)


def _diff_attn(t5, dq, dk, dv, bkt, lamv, subln, *, hs, kv_buffers, tqs, n_sub, n_diag, tkw, tkn, q_off, near_back,
               far_bucket,
               lam_init):
    b, sq, _ = dq.shape
    skp = dk.shape[1]
    n_near = bkt.shape[0]
    tq = tqs * n_sub
    q_spec, k_spec, v_spec = _attn_specs(tq, skp, width=hs * LANES, v_width=hs * 2 * DIFF_V, kv_buffers=kv_buffers)
    kern = functools.partial(_diff_kernel, hs=hs, tqs=tqs, n_sub=n_sub, n_diag=n_diag, tkw=tkw, tkn=tkn,
                             q_off=q_off, n_near=n_near, near_back=near_back, far_bucket=far_bucket,
                             lam_init=lam_init)
    return pl.pallas_call(
        kern,
        grid=(b, DIFF_HEADS // hs, sq // tq),
        in_specs=[pl.BlockSpec(memory_space=pltpu.SMEM), q_spec, k_spec, v_spec,
                  _full(bkt.shape), _full(lamv.shape), _full(subln.shape)],
        out_specs=q_spec,
        out_shape=jax.ShapeDtypeStruct((b, sq, DIFF_HEADS * DIFF_V), BF16),
        scratch_shapes=[pltpu.VMEM((n_near, DIFF_HEADS, tqs, tkn), F32), pltpu.VMEM((hs, 2, n_sub, tqs, LANES), F32),
                        pltpu.VMEM((hs, 2, n_sub, tqs, 2 * DIFF_V), F32)],
        compiler_params=_cparams(3),
        name="diff_attn",
    )(t5, dq, dk, dv, bkt, lamv, subln)


def _sb_kernel(q_ref, k_ref, v_ref, tri_ref, o_ref, run_ref, acc_ref, *, hp, tq, tk, q_off, n_masked):
    qi = pl.program_id(2)
    qpos0 = q_off + qi * tq
    top = (qpos0 + tq - 1) // tk
    lane = lax.broadcasted_iota(jnp.int32, (tq, LANES), 1)
    tri = tri_ref[...]
    qhs = []
    for pp in range(hp):
        qpair = q_ref[:, pp * LANES:(pp + 1) * LANES]
        qhs.append([_keep_lanes(qpair, lane < SB_DIM), _keep_lanes(qpair, lane >= SB_DIM)])

    def block(kj, masked, valid=None):
        if valid is not None:
            kj = jnp.maximum(kj, 0)
        ks = pl.multiple_of(kj * tk, tk)
        if masked:
            qp = qpos0 + lax.broadcasted_iota(jnp.int32, (tq, tk), 0)
            kp = kj * tk + lax.broadcasted_iota(jnp.int32, (tq, tk), 1)
            ok = kp < qp
        chains = [(pp, hh) for pp in range(hp) for hh in range(2)]
        ks_ = [k_ref[pl.ds(ks, tk), pp * LANES:(pp + 1) * LANES] for pp in range(hp)]
        vs_ = [v_ref[pl.ds(ks, tk), pp * LANES:(pp + 1) * LANES] for pp in range(hp)]
        zs = [_dot_nt(qhs[pp][hh], ks_[pp]) for pp, hh in chains]
        sps, log_betas = [], []
        for z in zs:
            sp = jnp.maximum(z, 0.0) + jnp.log(1.0 + jnp.exp2(jnp.abs(z) * -LOG2E))
            log_betas.append(z - sp)
            if masked:
                sp = jnp.where(ok, sp, 0.0)
            if valid is not None:
                sp = jnp.where(valid, sp, 0.0)
            sps.append(sp)
        laters = [_dot(sp.astype(BF16), tri) for sp in sps]
        probs = []
        for (pp, hh), log_beta, later in zip(chains, log_betas, laters):
            a = jnp.exp2(((log_beta - run_ref[pp, hh]) - later) * LOG2E)
            if masked:
                a = jnp.where(ok, a, 0.0)
            if valid is not None:
                a = jnp.where(valid, a, 0.0)
            probs.append(a.astype(BF16))
        for (pp, hh), a, sp in zip(chains, probs, sps):
            acc_ref[pp, hh] += _dot(a, vs_[pp])
            run_ref[pp, hh] += jnp.sum(sp, axis=-1, keepdims=True)

    run_ref[...] = jnp.zeros(run_ref.shape, F32)
    acc_ref[...] = jnp.zeros(acc_ref.shape, F32)
    for r in range(n_masked):
        block(top - r, True)
    nxt = top - n_masked
    block(nxt, False, valid=nxt >= 0)

    def cond(kj):
        return (kj >= 0) & (jnp.min(run_ref[...]) < -SB_EXIT)

    def body(kj):
        block(kj, False)
        return kj - 1

    lax.while_loop(cond, body, nxt - 1)
    for pp in range(hp):
        o_ref[:, pp * LANES:(pp + 1) * LANES] = jnp.where(lane < SB_DIM, acc_ref[pp, 0], acc_ref[pp, 1]).astype(BF16)


def _sb_attn(sq_, sk_, sv_, tri, *, hp, kv_buffers, tq, tk, q_off, n_masked):
    b, sq, _ = sq_.shape
    skp = sk_.shape[1]
    q_spec, kv_spec, _ = _attn_specs(tq, skp, width=hp * LANES, kv_buffers=kv_buffers)
    kern = functools.partial(_sb_kernel, hp=hp, tq=tq, tk=tk, q_off=q_off, n_masked=n_masked)
    return pl.pallas_call(
        kern,
        grid=(b, SB_HEADS // (2 * hp), sq // tq),
        in_specs=[q_spec, kv_spec, kv_spec, _full(tri.shape)],
        out_specs=q_spec,
        out_shape=jax.ShapeDtypeStruct((b, sq, SB_HEADS * SB_DIM), BF16),
        scratch_shapes=[pltpu.VMEM((hp, 2, tq, 1), F32), pltpu.VMEM((hp, 2, tq, LANES), F32)],
        compiler_params=_cparams(3),
        name="sb_attn",
    )(sq_, sk_, sv_, tri)


CA_ROW_GROUP = 16


def _ca_kernel(tab_ref, q_ref, k_ref, v_ref, o_ref, bias_ref, *, hp, tq, win, win_real, q_off):
    first = (pl.program_id(0) == 0) & (pl.program_id(1) == 0) & (pl.program_id(2) == 0)
    n_shift = -(-(tq + win) // LANES) * LANES
    n_rel = n_shift + LANES

    @pl.when(first)
    def _build_bias():
        x = lax.broadcasted_iota(jnp.int32, (CA_HEADS, n_rel), 1)
        idx = jnp.clip(x - (tq - 1) - CA_BAND, -CA_MAX_REL, CA_MAX_REL) + CA_MAX_REL
        f = jnp.zeros((CA_HEADS, n_rel), F32)
        for t in range(2 * CA_MAX_REL + 1):
            f = jnp.where(idx == t, tab_ref[:, t:t + 1], f)
        g = CA_ROW_GROUP
        i_loc = lax.broadcasted_iota(jnp.int32, (g, win), 0)
        j_loc = lax.broadcasted_iota(jnp.int32, (g, win), 1)
        for hh in range(CA_HEADS):
            fh = f[hh:hh + 1, :]
            shifted = jnp.concatenate([fh[:, g - 1 - bb:g - 1 - bb + n_shift] for bb in range(g)], axis=0)
            for a in range(tq // g):
                start = tq - g * a - g
                tile = shifted[:, start:start + win]
                i = i_loc + g * a
                kc = (j_loc >> CHUNK_SHIFT) - CA_LEFT_CHUNKS
                qc = i >> CHUNK_SHIFT
                ok = (kc <= qc) & (kc >= qc - CA_LEFT_CHUNKS) & (j_loc < win_real)
                bias_ref[hh, g * a:g * a + g, :] = jnp.where(ok, tile, NEG)

    group = pl.program_id(1)
    qi = pl.program_id(2)
    qpos0 = q_off + qi * tq
    ws = pl.multiple_of(qi * tq, tq)
    lane = lax.broadcasted_iota(jnp.int32, (tq, LANES), 1)
    kpos = qpos0 - CA_BAND + lax.broadcasted_iota(jnp.int32, (tq, win), 1)
    chains = [(pp, hh) for pp in range(hp) for hh in range(2)]
    cols = [slice(pp * LANES, (pp + 1) * LANES) for pp in range(hp)]
    ks = [k_ref[pl.ds(ws, win), cols[pp]] for pp in range(hp)]
    vs = [v_ref[pl.ds(ws, win), cols[pp]] for pp in range(hp)]
    scores = []
    for pp, hh in chains:
        qh = _keep_lanes(q_ref[:, cols[pp]], (lane < CA_DIM) if hh == 0 else (lane >= CA_DIM))
        scores.append(_dot_nt(qh, ks[pp]))
    probs = []
    for (pp, hh), s in zip(chains, scores):
        s = jnp.where(kpos >= 0, s + bias_ref[2 * (group * hp + pp) + hh], NEG)
        p = jnp.exp(s - jnp.max(s, axis=-1, keepdims=True))
        probs.append((p.astype(BF16), jnp.sum(p, axis=-1, keepdims=True)))
    outs = [_dot(p, vs[pp]) / denom for (pp, hh), (p, denom) in zip(chains, probs)]
    for pp in range(hp):
        o_ref[:, cols[pp]] = jnp.where(lane < CA_DIM, outs[2 * pp], outs[2 * pp + 1]).astype(BF16)


def _ca_attn(tab_t, cq, ck, cv, *, hp, kv_buffers, tq, win, win_real, q_off):
    b, sq, _ = cq.shape
    skp = ck.shape[1]
    q_spec, kv_spec, _ = _attn_specs(tq, skp, width=hp * LANES, kv_buffers=kv_buffers)
    kern = functools.partial(_ca_kernel, hp=hp, tq=tq, win=win, win_real=win_real, q_off=q_off)
    return pl.pallas_call(
        kern,
        grid=(b, CA_HEADS // (2 * hp), sq // tq),
        in_specs=[_full(tab_t.shape), q_spec, kv_spec, kv_spec],
        out_specs=q_spec,
        out_shape=jax.ShapeDtypeStruct((b, sq, CA_HEADS * CA_DIM), BF16),
        scratch_shapes=[pltpu.VMEM((CA_HEADS, tq, win), F32)],
        compiler_params=_cparams(3),
        name="ca_attn",
    )(tab_t, cq, ck, cv)


def _post_kernel(x_ref, oa_ref, ob_ref, wa_ref, wb_ref, g_ref, w1_ref, w2_ref, gf_ref, y_ref,
                 x1_ref, hn_ref, acc_ref, *, final_norm):
    j = pl.program_id(1)

    @pl.when(j == 0)
    def _mix():
        x1 = x_ref[...] + _dot(oa_ref[...], wa_ref[...]) + _dot(ob_ref[...], wb_ref[...])
        x1_ref[...] = x1
        hn_ref[...] = _rms(x1, g_ref[...]).astype(BF16)
        acc_ref[...] = jnp.zeros(acc_ref.shape, F32)

    a = jnp.maximum(_dot(hn_ref[...], w1_ref[...]), 0.0)
    acc_ref[...] += _dot((a * a).astype(BF16), w2_ref[...])

    @pl.when(j == pl.num_programs(1) - 1)
    def _finish():
        y = x1_ref[...] + acc_ref[...]
        if final_norm:
            y = _rms(y, gf_ref[...])
        y_ref[...] = y


def _post(x, oa, ob, wa, wb, g, w1, w2, gf, *, tm, tf, final_norm):
    m = x.shape[0]
    d_ff = w1.shape[1]
    rows = lambda width: pl.BlockSpec((tm, width), lambda i, j: (i, 0))
    return pl.pallas_call(
        functools.partial(_post_kernel, final_norm=final_norm),
        grid=(m // tm, d_ff // tf),
        in_specs=[rows(D_MODEL), rows(oa.shape[1]), rows(ob.shape[1]), _full(wa.shape), _full(wb.shape),
                  _full(g.shape), pl.BlockSpec((D_MODEL, tf), lambda i, j: (0, j)),
                  pl.BlockSpec((tf, D_MODEL), lambda i, j: (j, 0)), _full(gf.shape)],
        out_specs=rows(D_MODEL),
        out_shape=jax.ShapeDtypeStruct((m, D_MODEL), F32),
        scratch_shapes=[pltpu.VMEM((tm, D_MODEL), F32), pltpu.VMEM((tm, D_MODEL), BF16),
                        pltpu.VMEM((tm, D_MODEL), F32)],
        compiler_params=_cparams(2),
        name="post_mlp",
    )(x, oa, ob, wa, wb, g, w1, w2, gf)


def _t5_bucket(rel):
    nb = T5_BUCKETS // 2
    max_exact = nb // 2
    ret = jnp.where(rel > 0, nb, 0)
    n = jnp.abs(rel)
    nf = jnp.maximum(n, 1).astype(F32)
    large = max_exact + (jnp.log(nf / max_exact) / math.log(T5_MAX_DIST / max_exact) * (nb - max_exact)).astype(jnp.int32)
    large = jnp.minimum(large, nb - 1)
    return ret + jnp.where(n < max_exact, n, large)


def _rope_tables(pos):
    half = MLA_ROPE // 2
    inv = ROPE_BASE ** (-jnp.arange(half, dtype=F32) / half)
    ang = pos.astype(F32)[:, None] * inv[None, :]
    cos, sin = jnp.cos(ang), jnp.sin(ang)
    cosk = jnp.concatenate([cos, cos], axis=1)
    sink = jnp.concatenate([-sin, sin], axis=1)
    n = pos.shape[0]
    pad = jnp.zeros((n, LANES - MLA_NOPE - MLA_ROPE), F32)
    qscale = (MLA_NOPE + MLA_ROPE) ** -0.5 * LOG2E
    cosq = jnp.concatenate([jnp.ones((n, MLA_NOPE), F32), cosk, pad], axis=1) * qscale
    sinq = jnp.concatenate([jnp.zeros((n, MLA_NOPE), F32), sink, pad], axis=1) * qscale
    return {"cosq": cosq, "sinq": sinq, "cosk": cosk, "sink": sink}


def _swap_halves(w):
    half = w.shape[-1] // 2
    return jnp.concatenate([w[..., half:], w[..., :half]], axis=-1)


def _even_weights(w_in, q_norm, kv_norm, w_uq, w_ukv, w_out):
    sizes = [MLA_Q_RANK, MLA_KV_RANK, MLA_ROPE, DIFF_HEADS * 2 * DIFF_QK, DIFF_HEADS * 2 * DIFF_QK,
             DIFF_HEADS * DIFF_V]
    offs = np.cumsum([0] + sizes)
    wcq, wckv, wkr, wdq, wdk, wdv = (w_in[:, offs[i]:offs[i + 1]].astype(BF16) for i in range(6))
    uq = w_uq.reshape(MLA_Q_RANK, MLA_HEADS, MLA_NOPE + MLA_ROPE)
    zq = jnp.zeros((MLA_Q_RANK, MLA_HEADS, LANES - MLA_NOPE - MLA_ROPE), F32)
    wq = jnp.concatenate([uq, zq], axis=-1)
    wqs = jnp.concatenate([jnp.zeros_like(uq[..., :MLA_NOPE]), _swap_halves(uq[..., MLA_NOPE:]), zq], axis=-1)
    ukv = w_ukv.reshape(MLA_KV_RANK, MLA_HEADS, MLA_NOPE + MLA_V)
    zk = jnp.zeros((MLA_KV_RANK, MLA_HEADS, LANES - MLA_NOPE), F32)
    wk = jnp.concatenate([ukv[..., :MLA_NOPE], zk], axis=-1)
    wv = jnp.concatenate([ukv[..., MLA_NOPE:], jnp.zeros((MLA_KV_RANK, MLA_HEADS, LANES - MLA_V), F32)], axis=-1)
    place = np.zeros((MLA_ROPE, MLA_HEADS, LANES), np.float32)
    ones = np.zeros((1, MLA_HEADS, LANES), np.float32)
    for hh in range(MLA_HEADS):
        place[np.arange(MLA_ROPE), hh, MLA_NOPE + np.arange(MLA_ROPE)] = 1.0
        ones[0, hh, MLA_V] = 1.0
    flat = lambda a: a.reshape(a.shape[0], MLA_HEADS * LANES)
    wo_mla = w_out[:MLA_HEADS * MLA_V].reshape(MLA_HEADS, MLA_V, D_MODEL)
    wo_mla = jnp.concatenate([wo_mla, jnp.zeros((MLA_HEADS, LANES - MLA_V, D_MODEL), F32)], axis=1)
    return {
        "wcq": wcq, "wckv": wckv, "wkr": wkr, "wkrs": _swap_halves(wkr), "wdq": wdq, "wdk": wdk, "wdv": wdv,
        "qn": q_norm.reshape(1, -1), "kvn": kv_norm.reshape(1, -1),
        "wq": flat(wq).astype(BF16), "wqs": flat(wqs).astype(BF16),
        "wk": flat(wk).astype(BF16), "wv": flat(wv).astype(BF16),
        "place": jnp.asarray(flat(place), BF16), "ones": jnp.asarray(flat(ones), F32),
        "wo_mla": wo_mla.reshape(MLA_HEADS * LANES, D_MODEL).astype(BF16),
        "wo_diff": w_out[MLA_HEADS * MLA_V:].astype(BF16),
    }


PACK_ROWS = 256


def _pack_kernel(past_ref, new_ref, out_ref, *, ones):
    n_past, heads, dim = past_ref.shape
    n_new = new_ref.shape[0]
    def rows(c, carry):
        r0 = pl.multiple_of(c * PACK_ROWS, PACK_ROWS)
        pieces = []
        for hh in range(heads):
            pieces.append(past_ref[pl.ds(r0, PACK_ROWS), hh, :].astype(BF16))
            if ones:
                pieces.append(jnp.ones((PACK_ROWS, dim), BF16))
        out_ref[pl.ds(r0, PACK_ROWS), :] = jnp.concatenate(pieces, axis=1)
        return carry

    lax.fori_loop(0, n_past // PACK_ROWS, rows, 0)
    out_ref[n_past:n_past + n_new, :] = new_ref[...]
    n_pad = out_ref.shape[0] - n_past - n_new
    if n_pad:
        out_ref[n_past + n_new:, :] = jnp.zeros((n_pad, out_ref.shape[1]), BF16)


def _pack_cache(past, new, total, ones=False):
    b, n_past, heads, dim = past.shape
    n_new, width = new.shape[1], new.shape[2]
    return pl.pallas_call(
        functools.partial(_pack_kernel, ones=ones),
        grid=(b,),
        in_specs=[pl.BlockSpec((None, n_past, heads, dim), lambda bi: (bi, 0, 0, 0)),
                  pl.BlockSpec((None, n_new, width), lambda bi: (bi, 0, 0))],
        out_specs=pl.BlockSpec((None, total, width), lambda bi: (bi, 0, 0)),
        out_shape=jax.ShapeDtypeStruct((b, total, width), BF16),
        compiler_params=_cparams(1),
        name="pack_cache",
    )(past, new)


def _pad_rows(a, total, front=0):
    back = total - front - a.shape[1]
    return jnp.pad(a, ((0, 0), (front, back), (0, 0)))


def _round_up(n, mult):
    return -(-n // mult) * mult


def _diff_buckets(tq, tk, q_off, sk_real, skp):
    near_back = -((q_off - (T5_MAX_DIST - 1)) // tk - q_off // tk)
    last = (_round_up(q_off + tq, CHUNK) - 1) // tk
    last = min(last, skp // tk - 1)
    n_near = last - (q_off // tk - near_back) + 1
    i = np.arange(tq)[:, None]
    mats = []
    for r in range(n_near):
        kp = (q_off // tk - near_back + r) * tk + np.arange(tk)[None, :]
        qp = q_off + i
        ok = ((kp >> CHUNK_SHIFT) <= (qp >> CHUNK_SHIFT)) & (kp < sk_real)
        bkt = _t5_bucket(jnp.asarray(kp - qp, jnp.int32))
        mats.append(jnp.where(jnp.asarray(ok), bkt, -1))
    return jnp.stack(mats).astype(jnp.int32), near_back


def _trunk(x, q_off, caches, prm, cfg):
    b, sq, _ = x.shape
    m = b * sq
    tq, tk, tm, tf = cfg["tq"], cfg["tk"], cfg["tm"], cfg["tf"]
    sk_real = q_off + sq
    skp = _round_up(sk_real, tk)
    pos = q_off + jnp.arange(sq, dtype=jnp.int32)
    tabs = {k: jnp.tile(v, (b, 1)) for k, v in _rope_tables(pos).items()}
    x2 = x.reshape(m, D_MODEL)

    def with_past(past, new, dtype):
        new = new.reshape(b, sq, -1)
        if past is None:
            return new.astype(dtype)
        return jnp.concatenate([past.reshape(b, past.shape[1], -1).astype(dtype), new.astype(dtype)], axis=1)

    ew = prm["even"]
    qext, ckv, kr, dq, dk, dkb, dv, dvb = _even_proj(x2, prm["norm_mix"][0:1], ew, tabs, tm)
    past = (None,) * 4 if caches is None else tuple(c[0] for c in caches[:4])
    ckv_all = _pad_rows(with_past(past[0], ckv, F32), skp)
    kr_all = _pad_rows(with_past(past[1], kr, F32), skp)
    kext, vext = _kv_up(ckv_all.reshape(b * skp, -1), kr_all.reshape(b * skp, -1), ew, cfg["tm_kv"])
    kext = kext.reshape(b, skp, -1)
    vext = vext.reshape(b, skp, -1)
    tqs, n_sub = cfg["tqs"], cfg["n_sub"]
    assert n_sub == 1 or tqs == tk
    n_diag = (_round_up(q_off + tqs * n_sub, CHUNK) - 1) // tk - q_off // tk + 1
    tiles = dict(tqs=tqs, n_sub=n_sub, n_diag=n_diag, tkw=cfg["tkw"], tkn=tk, q_off=q_off)
    o_mla = _mla_attn(qext.reshape(b, sq, -1), kext, vext, hs=cfg["mla_heads"], kv_buffers=cfg["kv_buffers"],
                      sk_real=sk_real, **tiles)
    def keys_values(past, new, total, ones=False):
        new = new.reshape(b, sq, -1)
        return _pad_rows(new, total) if past is None else _pack_cache(past, new, total, ones)

    dk_all = keys_values(past[2], dkb, skp)
    dv_all = keys_values(past[3], dvb, skp, ones=True)
    bkt, near_back = _diff_buckets(tqs, tk, q_off, sk_real, skp)
    assert bkt.shape[0] == near_back + n_diag - (n_sub - 1)
    lam_init = 0.8 - 0.6 * math.exp(-0.3 * 0)
    o_diff = _diff_attn(prm["t5"], dq.reshape(b, sq, -1), dk_all, dv_all, bkt, prm["lam_vecs"], prm["subln"],
                        hs=cfg["diff_heads"], kv_buffers=cfg["kv_buffers"], near_back=near_back,
                        far_bucket=T5_BUCKETS // 2 - 1,
                        lam_init=lam_init, **tiles)
    x2 = _post(x2, o_mla.reshape(m, -1), o_diff.reshape(m, -1), ew["wo_mla"], ew["wo_diff"],
               prm["norm_ff"][0:1], prm["w_ff1"][0], prm["w_ff2"][0], prm["final_norm"],
               tm=cfg["tm_post"], tf=tf, final_norm=False)
    new_even = (ckv.reshape(1, b, sq, MLA_KV_RANK), kr.reshape(1, b, sq, MLA_ROPE),
                dk.reshape(1, b, sq, DIFF_HEADS, 2 * DIFF_QK), dv.reshape(1, b, sq, DIFF_HEADS, DIFF_V))

    sq_, sk_, skb, sv_, svb, cq, ck, ckb, cv, cvb = _odd_proj(x2, prm["norm_mix"][1:2], prm["w_in_odd"], tm)
    past = (None,) * 4 if caches is None else tuple(c[0] for c in caches[4:])
    sk_all = keys_values(past[0], skb, skp)
    sv_all = keys_values(past[1], svb, skp)
    n_masked = (q_off + tq - 1) // tk - q_off // tk + 1
    pairs = dict(hp=cfg["head_pairs"], kv_buffers=cfg["kv_buffers"])
    o_sb = _sb_attn(sq_.reshape(b, sq, -1), sk_all, sv_all, prm["tri"][tk], tq=tq, tk=tk, q_off=q_off,
                    n_masked=n_masked, **pairs)
    win_real = tq + CA_BAND
    win = _round_up(win_real, LANES)
    if caches is None:
        ck_all = _pad_rows(ckb.reshape(b, sq, -1), sq + CA_BAND + win - win_real, front=CA_BAND)
        cv_all = _pad_rows(cvb.reshape(b, sq, -1), sq + CA_BAND + win - win_real, front=CA_BAND)
    else:
        ck_all = keys_values(past[2], ckb, win)
        cv_all = keys_values(past[3], cvb, win)
    o_ca = _ca_attn(prm["ca_tab_t"], cq.reshape(b, sq, -1), ck_all, cv_all, tq=tq, win=win, win_real=win_real,
                    q_off=q_off, **pairs)
    x2 = _post(x2, o_sb.reshape(m, -1), o_ca.reshape(m, -1), prm["wo_sb"], prm["wo_ca"],
               prm["norm_ff"][1:2], prm["w_ff1"][1], prm["w_ff2"][1], prm["final_norm"],
               tm=cfg["tm_post"], tf=tf, final_norm=True)

    heads = lambda a: a.reshape(b, sq, SB_HEADS, SB_DIM)
    if caches is None:
        nb = min(CA_BAND, sq)
        cak, cav = heads(ck)[:, sq - nb:], heads(cv)[:, sq - nb:]
    else:
        nb = past[2].shape[1]
        cak = jnp.concatenate([past[2], heads(ck)], axis=1)[:, sq:]
        cav = jnp.concatenate([past[3], heads(cv)], axis=1)[:, sq:]
        assert cak.shape[1] == nb
    new_odd = (heads(sk_)[None], heads(sv_)[None], cak[None], cav[None])
    return x2.reshape(b, sq, D_MODEL), new_even + new_odd


def _tri(tk):
    j = np.arange(tk)[:, None]
    s = np.arange(tk)[None, :]
    return jnp.asarray((j > s).astype(np.float32), BF16)


def kernel(x_prompt, x_sample, cache_mla_ckv, cache_mla_krope, cache_diff_k, cache_diff_v, cache_sb_k, cache_sb_v, cache_ca_k, cache_ca_v, norm_mix, norm_ff, w_in_even, mla_q_norm, mla_kv_norm, mla_w_uq, mla_w_ukv, diff_lambda_vecs, diff_subln, t5_bias, w_out_even, w_in_odd, ca_rel_bias, w_out_odd, w_ff1, w_ff2, final_norm):
    seq = x_prompt.shape[1]
    dec_seq = x_sample.shape[1]
    past_len = cache_mla_ckv.shape[2]
    assert cache_ca_k.shape[2] == CA_BAND and past_len % CHUNK == 0

    cfg_p = {"tq": 256, "tqs": 256, "n_sub": 4, "tk": 256, "tkw": (2048, 1024), "mla_heads": 2, "diff_heads": 1,
             "head_pairs": 2, "kv_buffers": 1,
             "tm": min(512, seq), "tm_kv": min(512, seq), "tm_post": min(1024, seq), "tf": 1024}
    rows_s = x_sample.shape[0] * dec_seq
    cfg_s = {"tq": dec_seq, "tqs": dec_seq, "n_sub": 1, "tk": 128, "tkw": (512,), "mla_heads": MLA_HEADS,
             "diff_heads": DIFF_HEADS, "head_pairs": SB_HEADS // 2, "kv_buffers": None, "tm": rows_s, "tm_kv": x_sample.shape[0] * 128 // 2, "tm_post": rows_s,
             "tf": 512}
    n_sb = SB_HEADS * SB_DIM
    prm = {
        "norm_mix": norm_mix, "norm_ff": norm_ff, "final_norm": final_norm.reshape(1, -1),
        "even": _even_weights(w_in_even[0], mla_q_norm[0], mla_kv_norm[0], mla_w_uq[0], mla_w_ukv[0],
                              w_out_even[0]),
        "t5": t5_bias, "lam_vecs": diff_lambda_vecs[0], "subln": diff_subln[0].reshape(1, -1),
        "w_in_odd": w_in_odd[0].astype(BF16), "ca_tab_t": ca_rel_bias[0].T,
        "wo_sb": w_out_odd[0][:n_sb].astype(BF16), "wo_ca": w_out_odd[0][n_sb:].astype(BF16),
        "w_ff1": w_ff1.astype(BF16), "w_ff2": w_ff2.astype(BF16),
        "tri": {tk: _tri(tk) for tk in {cfg_p["tk"], cfg_s["tk"]}},
    }
    y_prompt, new_p = _trunk(x_prompt, 0, None, prm, cfg_p)
    caches = (cache_mla_ckv, cache_mla_krope, cache_diff_k, cache_diff_v,
              cache_sb_k, cache_sb_v, cache_ca_k, cache_ca_v)
    y_sample, new_s = _trunk(x_sample, past_len, caches, prm, cfg_s)
    return (y_prompt, y_sample) + tuple(new_p) + tuple(new_s)
```

```python
import functools
import math

import numpy as np
import jax
import jax.numpy as jnp
from jax import lax
from jax.experimental import pallas as pl
from jax.experimental.pallas import tpu as pltpu

F32 = jnp.float32
BF16 = jnp.bfloat16

D_MODEL = 1024
CHUNK = 64
CHUNK_SHIFT = 6
EPS = 1e-6
NEG = -1e30

MLA_HEADS = 8
MLA_Q_RANK = 256
MLA_KV_RANK = 128
MLA_NOPE = 64
MLA_ROPE = 32
MLA_V = 64
ROPE_BASE = 10000.0
DIFF_HEADS = 4
DIFF_QK = 64
DIFF_V = 2 * DIFF_QK
T5_BUCKETS = 32
T5_MAX_DIST = 128
SB_HEADS = 8
SB_DIM = 64
CA_HEADS = 8
CA_DIM = 64
CA_LEFT_CHUNKS = 8
CA_BAND = CA_LEFT_CHUNKS * CHUNK
CA_MAX_REL = 128

LANES = 128
VMEM_LIMIT = 48 * 1024 * 1024
LOG2E = math.log2(math.e)
SB_EXIT = -104.0


def _cparams(n_axes):
    return pltpu.CompilerParams(dimension_semantics=("arbitrary",) * n_axes,
                                vmem_limit_bytes=VMEM_LIMIT)


def _rms(x, g):
    return x * lax.rsqrt(jnp.mean(x * x, axis=-1, keepdims=True) + EPS) * g


def _dot(a, b):
    return jnp.dot(a, b, preferred_element_type=F32)


def _dot_nt(a, b):
    return lax.dot_general(a, b, (((1,), (1,)), ((), ())), preferred_element_type=F32)


def _keep_lanes(q, keep):
    return jnp.where(keep, q.astype(F32), 0.0).astype(BF16)


def _full(shape):
    n = len(shape)
    return pl.BlockSpec(shape, lambda *_: (0,) * n)


def _rows(tm, width):
    return pl.BlockSpec((tm, width), lambda i: (i, 0))


def _even_proj_kernel(x_ref, g_ref, wcq_ref, wckv_ref, wkr_ref, wkrs_ref, wdq_ref, wdk_ref, wdv_ref,
                      qn_ref, kvn_ref, wq_ref, wqs_ref, cosq_ref, sinq_ref, cosk_ref, sink_ref,
                      qext_ref, ckv_ref, kr_ref, dq_ref, dk_ref, dkb_ref, dv_ref, dvb_ref):
    hn = _rms(x_ref[...], g_ref[...]).astype(BF16)
    cq = _rms(_dot(hn, wcq_ref[...]), qn_ref[...]).astype(BF16)
    cosq = jnp.concatenate([cosq_ref[...]] * MLA_HEADS, axis=1)
    sinq = jnp.concatenate([sinq_ref[...]] * MLA_HEADS, axis=1)
    qext_ref[...] = (_dot(cq, wq_ref[...]) * cosq + _dot(cq, wqs_ref[...]) * sinq).astype(BF16)
    ckv_ref[...] = _rms(_dot(hn, wckv_ref[...]), kvn_ref[...])
    kr_ref[...] = _dot(hn, wkr_ref[...]) * cosk_ref[...] + _dot(hn, wkrs_ref[...]) * sink_ref[...]
    dq_ref[...] = (_dot(hn, wdq_ref[...]) * (DIFF_QK ** -0.5 * LOG2E)).astype(BF16)
    dk = _dot(hn, wdk_ref[...])
    dk_ref[...] = dk
    dkb_ref[...] = dk.astype(BF16)
    dv = _dot(hn, wdv_ref[...])
    dv_ref[...] = dv
    dvb = dv.astype(BF16)
    ones = jnp.ones((dvb.shape[0], DIFF_V), BF16)
    dvb_ref[...] = jnp.concatenate(
        [piece for hh in range(DIFF_HEADS) for piece in (dvb[:, hh * DIFF_V:(hh + 1) * DIFF_V], ones)], axis=1)


def _even_proj(x, g, w, tabs, tm):
    m = x.shape[0]
    ins = [x, g, w["wcq"], w["wckv"], w["wkr"], w["wkrs"], w["wdq"], w["wdk"], w["wdv"],
           w["qn"], w["kvn"], w["wq"], w["wqs"], tabs["cosq"], tabs["sinq"], tabs["cosk"], tabs["sink"]]
    row_in = {0: D_MODEL, 13: LANES, 14: LANES, 15: MLA_ROPE, 16: MLA_ROPE}
    in_specs = [_rows(tm, row_in[i]) if i in row_in else _full(a.shape) for i, a in enumerate(ins)]
    outs = [(MLA_HEADS * LANES, BF16), (MLA_KV_RANK, F32), (MLA_ROPE, F32),
            (DIFF_HEADS * DIFF_V, BF16), (DIFF_HEADS * DIFF_V, F32), (DIFF_HEADS * DIFF_V, BF16),
            (DIFF_HEADS * DIFF_V, F32), (DIFF_HEADS * 2 * DIFF_V, BF16)]
    return pl.pallas_call(
        _even_proj_kernel,
        grid=(m // tm,),
        in_specs=in_specs,
        out_specs=[_rows(tm, n) for n, _ in outs],
        out_shape=[jax.ShapeDtypeStruct((m, n), dt) for n, dt in outs],
        compiler_params=_cparams(1),
        name="even_proj",
    )(*ins)


def _odd_proj_kernel(x_ref, g_ref, w_ref, sq_ref, sk_ref, skb_ref, sv_ref, svb_ref,
                     cq_ref, ck_ref, ckb_ref, cv_ref, cvb_ref):
    hn = _rms(x_ref[...], g_ref[...]).astype(BF16)
    width = SB_HEADS * SB_DIM

    def seg(i):
        return _dot(hn, w_ref[:, i * width:(i + 1) * width])

    sq_ref[...] = (seg(0) * (SB_DIM ** -0.5)).astype(BF16)
    for i, (f_ref, b_ref) in ((1, (sk_ref, skb_ref)), (2, (sv_ref, svb_ref)),
                              (4, (ck_ref, ckb_ref)), (5, (cv_ref, cvb_ref))):
        y = seg(i)
        f_ref[...] = y
        b_ref[...] = y.astype(BF16)
    cq_ref[...] = (seg(3) * (CA_DIM ** -0.5)).astype(BF16)


def _odd_proj(x, g, w, tm):
    m = x.shape[0]
    width = SB_HEADS * SB_DIM
    dts = [BF16, F32, BF16, F32, BF16, BF16, F32, BF16, F32, BF16]
    return pl.pallas_call(
        _odd_proj_kernel,
        grid=(m // tm,),
        in_specs=[_rows(tm, D_MODEL), _full(g.shape), _full(w.shape)],
        out_specs=[_rows(tm, width) for _ in dts],
        out_shape=[jax.ShapeDtypeStruct((m, width), dt) for dt in dts],
        compiler_params=_cparams(1),
        name="odd_proj",
    )(x, g, w)


def _kv_up_kernel(ckv_ref, kr_ref, wk_ref, wv_ref, place_ref, ones_ref, kext_ref, vext_ref):
    c = ckv_ref[...].astype(BF16)
    r = kr_ref[...].astype(BF16)
    kext_ref[...] = (_dot(c, wk_ref[...]) + _dot(r, place_ref[...])).astype(BF16)
    vext_ref[...] = (_dot(c, wv_ref[...]) + ones_ref[...]).astype(BF16)


def _kv_up(ckv, kr, w, tm):
    m = ckv.shape[0]
    width = MLA_HEADS * LANES
    return pl.pallas_call(
        _kv_up_kernel,
        grid=(m // tm,),
        in_specs=[_rows(tm, MLA_KV_RANK), _rows(tm, MLA_ROPE), _full(w["wk"].shape), _full(w["wv"].shape),
                  _full(w["place"].shape), _full(w["ones"].shape)],
        out_specs=[_rows(tm, width)] * 2,
        out_shape=[jax.ShapeDtypeStruct((m, width), BF16)] * 2,
        compiler_params=_cparams(1),
        name="mla_kv_up",
    )(ckv, kr, w["wk"], w["wv"], w["place"], w["ones"])


def _attn_specs(tq, skp, width=LANES, v_width=None, kv_buffers=None):
    kw = {} if kv_buffers is None else {"pipeline_mode": pl.Buffered(kv_buffers)}
    q_spec = pl.BlockSpec((None, tq, width), lambda b, h, qi: (b, qi, h))
    k_spec = pl.BlockSpec((None, skp, width), lambda b, h, qi: (b, 0, h), **kw)
    v_spec = pl.BlockSpec((None, skp, v_width or width), lambda b, h, qi: (b, 0, h), **kw)
    return q_spec, k_spec, v_spec


def _softmax_block(s, m_ref):
    m_old = m_ref[...]
    m_new = jnp.maximum(m_old, jnp.max(s, axis=-1, keepdims=True))
    m_ref[...] = m_new
    return jnp.exp2(m_old - m_new), jnp.exp2(s - jnp.tile(m_new, (1, s.shape[1] // LANES)))


def _sweep_blocks(nb0, n_ahead, first_tile_possible, tkw, tkn, do_plain, do_tail):
    def run(first, n_lead):
        done = 0
        for width in tuple(tkw) + (tkn,):
            per = width // tkn
            count = (first - done) // per

            def plain(j, carry, width=width, per=per, done=done):
                do_plain(pl.multiple_of((done + j * per) * tkn, width), width)
                return carry

            lax.fori_loop(0, count, plain, 0)
            done = done + count * per
        do_tail(first, n_lead)

    if n_ahead == 0:
        run(nb0, 0)
        return
    pl.when(nb0 >= n_ahead)(lambda: run(nb0 - n_ahead, n_ahead))
    if first_tile_possible:
        pl.when(nb0 < n_ahead)(lambda: run(0, 0))


def _tail_kinds(n_blocks, near_back):
    return [None if i - (n_blocks - 1) < -near_back else i - (n_blocks - 1) + near_back
            for i in range(n_blocks)]


def _mla_kernel(q_ref, k_ref, v_ref, o_ref, m_ref, acc_ref, *, hs, tqs, n_sub, n_diag, tkw, tkn, q_off, sk_real):
    qi = pl.program_id(2)
    qpos0 = q_off + qi * (tqs * n_sub)
    m_ref[...] = jnp.full(m_ref.shape, NEG, F32)
    acc_ref[...] = jnp.zeros(acc_ref.shape, F32)

    chains = [(hh, r) for hh in range(hs) for r in range(n_sub)]
    cols = [slice(hh * LANES, (hh + 1) * LANES) for hh in range(hs)]

    def step(start, widths, mask_last):
        scores = [_dot_nt(q_ref[r * tqs:(r + 1) * tqs, cols[hh]], k_ref[pl.ds(start, widths[r]), cols[hh]])
                  for hh, r in chains]
        probs = []
        for (hh, r), s in zip(chains, scores):
            if mask_last:
                plain_w = widths[r] - tkn
                qp = qpos0 + r * tqs + lax.broadcasted_iota(jnp.int32, (tqs, tkn), 0)
                kp = start + plain_w + lax.broadcasted_iota(jnp.int32, (tqs, tkn), 1)
                ok = ((kp >> CHUNK_SHIFT) <= (qp >> CHUNK_SHIFT)) & (kp < sk_real)
                last = jnp.where(ok, s[:, plain_w:], NEG)
                s = last if plain_w == 0 else jnp.concatenate([s[:, :plain_w], last], axis=1)
            alpha, p = _softmax_block(s, m_ref.at[hh, r])
            probs.append((alpha, p.astype(BF16)))
        for (hh, r), (alpha, p) in zip(chains, probs):
            acc_ref[hh, r] = acc_ref[hh, r] * alpha + _dot(p, v_ref[pl.ds(start, widths[r]), cols[hh]])

    def plain(start, width):
        step(start, [width] * n_sub, False)

    def tail(first_block, n_lead):
        widths = [(n_lead + n_diag - (n_sub - 1 - r)) * tkn for r in range(n_sub)]
        step(pl.multiple_of(first_block * tkn, tkn), widths, True)

    _sweep_blocks(qpos0 // tkn, 0, False, tkw, tkn, plain, tail)

    for hh in range(hs):
        for r in range(n_sub):
            acc = acc_ref[hh, r]
            lane = lax.broadcasted_iota(jnp.int32, acc.shape, 1)
            denom = jnp.sum(jnp.where(lane == MLA_V, acc, 0.0), axis=-1, keepdims=True)
            o_ref[r * tqs:(r + 1) * tqs, hh * LANES:(hh + 1) * LANES] = (
                jnp.where(lane < MLA_V, acc / denom, 0.0).astype(BF16))


def _mla_attn(qext, kext, vext, *, hs, kv_buffers, tqs, n_sub, n_diag, tkw, tkn, q_off, sk_real):
    b, sq, _ = qext.shape
    skp = kext.shape[1]
    tq = tqs * n_sub
    q_spec, k_spec, v_spec = _attn_specs(tq, skp, width=hs * LANES, kv_buffers=kv_buffers)
    kern = functools.partial(_mla_kernel, hs=hs, tqs=tqs, n_sub=n_sub, n_diag=n_diag, tkw=tkw, tkn=tkn,
                             q_off=q_off, sk_real=sk_real)
    return pl.pallas_call(
        kern,
        grid=(b, MLA_HEADS // hs, sq // tq),
        in_specs=[q_spec, k_spec, v_spec],
        out_specs=q_spec,
        out_shape=jax.ShapeDtypeStruct((b, sq, MLA_HEADS * LANES), BF16),
        scratch_shapes=[pltpu.VMEM((hs, n_sub, tqs, LANES), F32), pltpu.VMEM((hs, n_sub, tqs, LANES), F32)],
        compiler_params=_cparams(3),
        name="mla_attn",
    )(qext, kext, vext)


def _diff_kernel(t5_ref, q_ref, k_ref, v_ref, bkt_ref, lamv_ref, subln_ref, o_ref,
                 bias_ref, m_ref, acc_ref, *, hs, tqs, n_sub, n_diag, tkw, tkn, q_off, n_near, near_back, far_bucket,
                 lam_init):
    first = (pl.program_id(0) == 0) & (pl.program_id(1) == 0) & (pl.program_id(2) == 0)

    @pl.when(first)
    def _build_bias():
        for r in range(n_near):
            bkt = bkt_ref[r]
            vals = [jnp.full((tqs, tkn), NEG, F32) for _ in range(DIFF_HEADS)]
            for t in range(T5_BUCKETS):
                hit = bkt == t
                for hh in range(DIFF_HEADS):
                    vals[hh] = jnp.where(hit, (t5_ref[t, hh] - t5_ref[far_bucket, hh]) * LOG2E, vals[hh])
            for hh in range(DIFF_HEADS):
                bias_ref[r, hh] = vals[hh]

    group = pl.program_id(1)
    qi = pl.program_id(2)
    qpos0 = q_off + qi * (tqs * n_sub)
    lane = lax.broadcasted_iota(jnp.int32, (tqs * n_sub, LANES), 1)
    qm = []
    for hh in range(hs):
        q = q_ref[:, hh * LANES:(hh + 1) * LANES]
        qm.append([_keep_lanes(q, lane < DIFF_QK), _keep_lanes(q, lane >= DIFF_QK)])
    m_ref[...] = jnp.full(m_ref.shape, NEG, F32)
    acc_ref[...] = jnp.zeros(acc_ref.shape, F32)

    chains = [(hh, mi, r) for hh in range(hs) for r in range(n_sub) for mi in range(2)]
    n_ahead = near_back + (-near_back) % n_sub

    def step(start, widths, kinds):
        scores = [_dot_nt(qm[hh][mi][r * tqs:(r + 1) * tqs, :],
                          k_ref[pl.ds(start, widths[r]), hh * LANES:(hh + 1) * LANES]) for hh, mi, r in chains]
        probs = []
        for (hh, mi, r), s in zip(chains, scores):
            if kinds is not None:
                n_plain = sum(kind is None for kind in kinds[r])
                pieces = [s[:, :n_plain * tkn]] if n_plain else []
                for i, kind in enumerate(kinds[r]):
                    if kind is not None:
                        pieces.append(s[:, i * tkn:(i + 1) * tkn] + bias_ref[kind, group * hs + hh])
                s = pieces[0] if len(pieces) == 1 else jnp.concatenate(pieces, axis=1)
            alpha, p = _softmax_block(s, m_ref.at[hh, mi, r])
            probs.append((alpha, p.astype(BF16)))
        for (hh, mi, r), (alpha, p) in zip(chains, probs):
            v = v_ref[pl.ds(start, widths[r]), hh * 2 * DIFF_V:(hh + 1) * 2 * DIFF_V]
            acc_ref[hh, mi, r] = acc_ref[hh, mi, r] * jnp.tile(alpha, (1, 2)) + _dot(p, v)

    def plain(start, width):
        step(start, [width] * n_sub, None)

    def tail(first_block, n_lead):
        n_blocks = [n_lead + n_diag - (n_sub - 1 - r) for r in range(n_sub)]
        step(pl.multiple_of(first_block * tkn, tkn), [n * tkn for n in n_blocks],
             [_tail_kinds(n, near_back) for n in n_blocks])

    _sweep_blocks(qpos0 // tkn, n_ahead, q_off // tkn < n_ahead, tkw, tkn, plain, tail)

    lv = lamv_ref[...]
    lam = (jnp.exp(jnp.sum(lv[0:1] * lv[1:2], axis=-1, keepdims=True))
           - jnp.exp(jnp.sum(lv[2:3] * lv[3:4], axis=-1, keepdims=True)) + lam_init)
    for hh in range(hs):
        for r in range(n_sub):
            a0, a1 = acc_ref[hh, 0, r], acc_ref[hh, 1, r]
            o = a0[:, :DIFF_V] / a0[:, DIFF_V:] - lam * (a1[:, :DIFF_V] / a1[:, DIFF_V:])
            o_ref[r * tqs:(r + 1) * tqs, hh * LANES:(hh + 1) * LANES] = (
                _rms(o, subln_ref[...]) * (1.0 - lam_init)).astype(BF16)


def _diff_attn(t5, dq, dk, dv, bkt, lamv, subln, *, hs, kv_buffers, tqs, n_sub, n_diag, tkw, tkn, q_off, near_back,
               far_bucket,
               lam_init):
    b, sq, _ = dq.shape
    skp = dk.shape[1]
    n_near = bkt.shape[0]
    tq = tqs * n_sub
    q_spec, k_spec, v_spec = _attn_specs(tq, skp, width=hs * LANES, v_width=hs * 2 * DIFF_V, kv_buffers=kv_buffers)
    kern = functools.partial(_diff_kernel, hs=hs, tqs=tqs, n_sub=n_sub, n_diag=n_diag, tkw=tkw, tkn=tkn,
                             q_off=q_off, n_near=n_near, near_back=near_back, far_bucket=far_bucket,
                             lam_init=lam_init)
    return pl.pallas_call(
        kern,
        grid=(b, DIFF_HEADS // hs, sq // tq),
        in_specs=[pl.BlockSpec(memory_space=pltpu.SMEM), q_spec, k_spec, v_spec,
                  _full(bkt.shape), _full(lamv.shape), _full(subln.shape)],
        out_specs=q_spec,
        out_shape=jax.ShapeDtypeStruct((b, sq, DIFF_HEADS * DIFF_V), BF16),
        scratch_shapes=[pltpu.VMEM((n_near, DIFF_HEADS, tqs, tkn), F32), pltpu.VMEM((hs, 2, n_sub, tqs, LANES), F32),
                        pltpu.VMEM((hs, 2, n_sub, tqs, 2 * DIFF_V), F32)],
        compiler_params=_cparams(3),
        name="diff_attn",
    )(t5, dq, dk, dv, bkt, lamv, subln)


def _sb_kernel(q_ref, k_ref, v_ref, tri_ref, o_ref, run_ref, acc_ref, *, hp, tq, tk, q_off, n_masked):
    qi = pl.program_id(2)
    qpos0 = q_off + qi * tq
    top = (qpos0 + tq - 1) // tk
    lane = lax.broadcasted_iota(jnp.int32, (tq, LANES), 1)
    tri = tri_ref[...]
    qhs = []
    for pp in range(hp):
        qpair = q_ref[:, pp * LANES:(pp + 1) * LANES]
        qhs.append([_keep_lanes(qpair, lane < SB_DIM), _keep_lanes(qpair, lane >= SB_DIM)])

    def block(kj, masked, valid=None):
        if valid is not None:
            kj = jnp.maximum(kj, 0)
        ks = pl.multiple_of(kj * tk, tk)
        if masked:
            qp = qpos0 + lax.broadcasted_iota(jnp.int32, (tq, tk), 0)
            kp = kj * tk + lax.broadcasted_iota(jnp.int32, (tq, tk), 1)
            ok = kp < qp
        chains = [(pp, hh) for pp in range(hp) for hh in range(2)]
        ks_ = [k_ref[pl.ds(ks, tk), pp * LANES:(pp + 1) * LANES] for pp in range(hp)]
        vs_ = [v_ref[pl.ds(ks, tk), pp * LANES:(pp + 1) * LANES] for pp in range(hp)]
        zs = [_dot_nt(qhs[pp][hh], ks_[pp]) for pp, hh in chains]
        sps, log_betas = [], []
        for z in zs:
            sp = jnp.maximum(z, 0.0) + jnp.log(1.0 + jnp.exp2(jnp.abs(z) * -LOG2E))
            log_betas.append(z - sp)
            if masked:
                sp = jnp.where(ok, sp, 0.0)
            if valid is not None:
                sp = jnp.where(valid, sp, 0.0)
            sps.append(sp)
        laters = [_dot(sp.astype(BF16), tri) for sp in sps]
        probs = []
        for (pp, hh), log_beta, later in zip(chains, log_betas, laters):
            a = jnp.exp2(((log_beta - run_ref[pp, hh]) - later) * LOG2E)
            if masked:
                a = jnp.where(ok, a, 0.0)
            if valid is not None:
                a = jnp.where(valid, a, 0.0)
            probs.append(a.astype(BF16))
        for (pp, hh), a, sp in zip(chains, probs, sps):
            acc_ref[pp, hh] += _dot(a, vs_[pp])
            run_ref[pp, hh] += jnp.sum(sp, axis=-1, keepdims=True)

    run_ref[...] = jnp.zeros(run_ref.shape, F32)
    acc_ref[...] = jnp.zeros(acc_ref.shape, F32)
    for r in range(n_masked):
        block(top - r, True)
    nxt = top - n_masked
    block(nxt, False, valid=nxt >= 0)

    def cond(kj):
        return (kj >= 0) & (jnp.min(run_ref[...]) < -SB_EXIT)

    def body(kj):
        block(kj, False)
        return kj - 1

    lax.while_loop(cond, body, nxt - 1)
    for pp in range(hp):
        o_ref[:, pp * LANES:(pp + 1) * LANES] = jnp.where(lane < SB_DIM, acc_ref[pp, 0], acc_ref[pp, 1]).astype(BF16)


def _sb_attn(sq_, sk_, sv_, tri, *, hp, kv_buffers, tq, tk, q_off, n_masked):
    b, sq, _ = sq_.shape
    skp = sk_.shape[1]
    q_spec, kv_spec, _ = _attn_specs(tq, skp, width=hp * LANES, kv_buffers=kv_buffers)
    kern = functools.partial(_sb_kernel, hp=hp, tq=tq, tk=tk, q_off=q_off, n_masked=n_masked)
    return pl.pallas_call(
        kern,
        grid=(b, SB_HEADS // (2 * hp), sq // tq),
        in_specs=[q_spec, kv_spec, kv_spec, _full(tri.shape)],
        out_specs=q_spec,
        out_shape=jax.ShapeDtypeStruct((b, sq, SB_HEADS * SB_DIM), BF16),
        scratch_shapes=[pltpu.VMEM((hp, 2, tq, 1), F32), pltpu.VMEM((hp, 2, tq, LANES), F32)],
        compiler_params=_cparams(3),
        name="sb_attn",
    )(sq_, sk_, sv_, tri)


CA_ROW_GROUP = 16


def _ca_kernel(tab_ref, q_ref, k_ref, v_ref, o_ref, bias_ref, *, hp, tq, win, win_real, q_off):
    first = (pl.program_id(0) == 0) & (pl.program_id(1) == 0) & (pl.program_id(2) == 0)
    n_shift = -(-(tq + win) // LANES) * LANES
    n_rel = n_shift + LANES

    @pl.when(first)
    def _build_bias():
        x = lax.broadcasted_iota(jnp.int32, (CA_HEADS, n_rel), 1)
        idx = jnp.clip(x - (tq - 1) - CA_BAND, -CA_MAX_REL, CA_MAX_REL) + CA_MAX_REL
        f = jnp.zeros((CA_HEADS, n_rel), F32)
        for t in range(2 * CA_MAX_REL + 1):
            f = jnp.where(idx == t, tab_ref[:, t:t + 1], f)
        g = CA_ROW_GROUP
        i_loc = lax.broadcasted_iota(jnp.int32, (g, win), 0)
        j_loc = lax.broadcasted_iota(jnp.int32, (g, win), 1)
        for hh in range(CA_HEADS):
            fh = f[hh:hh + 1, :]
            shifted = jnp.concatenate([fh[:, g - 1 - bb:g - 1 - bb + n_shift] for bb in range(g)], axis=0)
            for a in range(tq // g):
                start = tq - g * a - g
                tile = shifted[:, start:start + win]
                i = i_loc + g * a
                kc = (j_loc >> CHUNK_SHIFT) - CA_LEFT_CHUNKS
                qc = i >> CHUNK_SHIFT
                ok = (kc <= qc) & (kc >= qc - CA_LEFT_CHUNKS) & (j_loc < win_real)
                bias_ref[hh, g * a:g * a + g, :] = jnp.where(ok, tile, NEG)

    group = pl.program_id(1)
    qi = pl.program_id(2)
    qpos0 = q_off + qi * tq
    ws = pl.multiple_of(qi * tq, tq)
    lane = lax.broadcasted_iota(jnp.int32, (tq, LANES), 1)
    kpos = qpos0 - CA_BAND + lax.broadcasted_iota(jnp.int32, (tq, win), 1)
    chains = [(pp, hh) for pp in range(hp) for hh in range(2)]
    cols = [slice(pp * LANES, (pp + 1) * LANES) for pp in range(hp)]
    ks = [k_ref[pl.ds(ws, win), cols[pp]] for pp in range(hp)]
    vs = [v_ref[pl.ds(ws, win), cols[pp]] for pp in range(hp)]
    scores = []
    for pp, hh in chains:
        qh = _keep_lanes(q_ref[:, cols[pp]], (lane < CA_DIM) if hh == 0 else (lane >= CA_DIM))
        scores.append(_dot_nt(qh, ks[pp]))
    probs = []
    for (pp, hh), s in zip(chains, scores):
        s = jnp.where(kpos >= 0, s + bias_ref[2 * (group * hp + pp) + hh], NEG)
        p = jnp.exp(s - jnp.max(s, axis=-1, keepdims=True))
        probs.append((p.astype(BF16), jnp.sum(p, axis=-1, keepdims=True)))
    outs = [_dot(p, vs[pp]) / denom for (pp, hh), (p, denom) in zip(chains, probs)]
    for pp in range(hp):
        o_ref[:, cols[pp]] = jnp.where(lane < CA_DIM, outs[2 * pp], outs[2 * pp + 1]).astype(BF16)


def _ca_attn(tab_t, cq, ck, cv, *, hp, kv_buffers, tq, win, win_real, q_off):
    b, sq, _ = cq.shape
    skp = ck.shape[1]
    q_spec, kv_spec, _ = _attn_specs(tq, skp, width=hp * LANES, kv_buffers=kv_buffers)
    kern = functools.partial(_ca_kernel, hp=hp, tq=tq, win=win, win_real=win_real, q_off=q_off)
    return pl.pallas_call(
        kern,
        grid=(b, CA_HEADS // (2 * hp), sq // tq),
        in_specs=[_full(tab_t.shape), q_spec, kv_spec, kv_spec],
        out_specs=q_spec,
        out_shape=jax.ShapeDtypeStruct((b, sq, CA_HEADS * CA_DIM), BF16),
        scratch_shapes=[pltpu.VMEM((CA_HEADS, tq, win), F32)],
        compiler_params=_cparams(3),
        name="ca_attn",
    )(tab_t, cq, ck, cv)


def _post_kernel(x_ref, oa_ref, ob_ref, wa_ref, wb_ref, g_ref, w1_ref, w2_ref, gf_ref, y_ref,
                 x1_ref, hn_ref, acc_ref, *, final_norm):
    j = pl.program_id(1)

    @pl.when(j == 0)
    def _mix():
        x1 = x_ref[...] + _dot(oa_ref[...], wa_ref[...]) + _dot(ob_ref[...], wb_ref[...])
        x1_ref[...] = x1
        hn_ref[...] = _rms(x1, g_ref[...]).astype(BF16)
        acc_ref[...] = jnp.zeros(acc_ref.shape, F32)

    a = jnp.maximum(_dot(hn_ref[...], w1_ref[...]), 0.0)
    acc_ref[...] += _dot((a * a).astype(BF16), w2_ref[...])

    @pl.when(j == pl.num_programs(1) - 1)
    def _finish():
        y = x1_ref[...] + acc_ref[...]
        if final_norm:
            y = _rms(y, gf_ref[...])
        y_ref[...] = y


def _post(x, oa, ob, wa, wb, g, w1, w2, gf, *, tm, tf, final_norm):
    m = x.shape[0]
    d_ff = w1.shape[1]
    rows = lambda width: pl.BlockSpec((tm, width), lambda i, j: (i, 0))
    return pl.pallas_call(
        functools.partial(_post_kernel, final_norm=final_norm),
        grid=(m // tm, d_ff // tf),
        in_specs=[rows(D_MODEL), rows(oa.shape[1]), rows(ob.shape[1]), _full(wa.shape), _full(wb.shape),
                  _full(g.shape), pl.BlockSpec((D_MODEL, tf), lambda i, j: (0, j)),
                  pl.BlockSpec((tf, D_MODEL), lambda i, j: (j, 0)), _full(gf.shape)],
        out_specs=rows(D_MODEL),
        out_shape=jax.ShapeDtypeStruct((m, D_MODEL), F32),
        scratch_shapes=[pltpu.VMEM((tm, D_MODEL), F32), pltpu.VMEM((tm, D_MODEL), BF16),
                        pltpu.VMEM((tm, D_MODEL), F32)],
        compiler_params=_cparams(2),
        name="post_mlp",
    )(x, oa, ob, wa, wb, g, w1, w2, gf)


def _t5_bucket(rel):
    nb = T5_BUCKETS // 2
    max_exact = nb // 2
    ret = jnp.where(rel > 0, nb, 0)
    n = jnp.abs(rel)
    nf = jnp.maximum(n, 1).astype(F32)
    large = max_exact + (jnp.log(nf / max_exact) / math.log(T5_MAX_DIST / max_exact) * (nb - max_exact)).astype(jnp.int32)
    large = jnp.minimum(large, nb - 1)
    return ret + jnp.where(n < max_exact, n, large)


def _rope_tables(pos):
    half = MLA_ROPE // 2
    inv = ROPE_BASE ** (-jnp.arange(half, dtype=F32) / half)
    ang = pos.astype(F32)[:, None] * inv[None, :]
    cos, sin = jnp.cos(ang), jnp.sin(ang)
    cosk = jnp.concatenate([cos, cos], axis=1)
    sink = jnp.concatenate([-sin, sin], axis=1)
    n = pos.shape[0]
    pad = jnp.zeros((n, LANES - MLA_NOPE - MLA_ROPE), F32)
    qscale = (MLA_NOPE + MLA_ROPE) ** -0.5 * LOG2E
    cosq = jnp.concatenate([jnp.ones((n, MLA_NOPE), F32), cosk, pad], axis=1) * qscale
    sinq = jnp.concatenate([jnp.zeros((n, MLA_NOPE), F32), sink, pad], axis=1) * qscale
    return {"cosq": cosq, "sinq": sinq, "cosk": cosk, "sink": sink}


def _swap_halves(w):
    half = w.shape[-1] // 2
    return jnp.concatenate([w[..., half:], w[..., :half]], axis=-1)


def _even_weights(w_in, q_norm, kv_norm, w_uq, w_ukv, w_out):
    sizes = [MLA_Q_RANK, MLA_KV_RANK, MLA_ROPE, DIFF_HEADS * 2 * DIFF_QK, DIFF_HEADS * 2 * DIFF_QK,
             DIFF_HEADS * DIFF_V]
    offs = np.cumsum([0] + sizes)
    wcq, wckv, wkr, wdq, wdk, wdv = (w_in[:, offs[i]:offs[i + 1]].astype(BF16) for i in range(6))
    uq = w_uq.reshape(MLA_Q_RANK, MLA_HEADS, MLA_NOPE + MLA_ROPE)
    zq = jnp.zeros((MLA_Q_RANK, MLA_HEADS, LANES - MLA_NOPE - MLA_ROPE), F32)
    wq = jnp.concatenate([uq, zq], axis=-1)
    wqs = jnp.concatenate([jnp.zeros_like(uq[..., :MLA_NOPE]), _swap_halves(uq[..., MLA_NOPE:]), zq], axis=-1)
    ukv = w_ukv.reshape(MLA_KV_RANK, MLA_HEADS, MLA_NOPE + MLA_V)
    zk = jnp.zeros((MLA_KV_RANK, MLA_HEADS, LANES - MLA_NOPE), F32)
    wk = jnp.concatenate([ukv[..., :MLA_NOPE], zk], axis=-1)
    wv = jnp.concatenate([ukv[..., MLA_NOPE:], jnp.zeros((MLA_KV_RANK, MLA_HEADS, LANES - MLA_V), F32)], axis=-1)
    place = np.zeros((MLA_ROPE, MLA_HEADS, LANES), np.float32)
    ones = np.zeros((1, MLA_HEADS, LANES), np.float32)
    for hh in range(MLA_HEADS):
        place[np.arange(MLA_ROPE), hh, MLA_NOPE + np.arange(MLA_ROPE)] = 1.0
        ones[0, hh, MLA_V] = 1.0
    flat = lambda a: a.reshape(a.shape[0], MLA_HEADS * LANES)
    wo_mla = w_out[:MLA_HEADS * MLA_V].reshape(MLA_HEADS, MLA_V, D_MODEL)
    wo_mla = jnp.concatenate([wo_mla, jnp.zeros((MLA_HEADS, LANES - MLA_V, D_MODEL), F32)], axis=1)
    return {
        "wcq": wcq, "wckv": wckv, "wkr": wkr, "wkrs": _swap_halves(wkr), "wdq": wdq, "wdk": wdk, "wdv": wdv,
        "qn": q_norm.reshape(1, -1), "kvn": kv_norm.reshape(1, -1),
        "wq": flat(wq).astype(BF16), "wqs": flat(wqs).astype(BF16),
        "wk": flat(wk).astype(BF16), "wv": flat(wv).astype(BF16),
        "place": jnp.asarray(flat(place), BF16), "ones": jnp.asarray(flat(ones), F32),
        "wo_mla": wo_mla.reshape(MLA_HEADS * LANES, D_MODEL).astype(BF16),
        "wo_diff": w_out[MLA_HEADS * MLA_V:].astype(BF16),
    }


def _pad_rows(a, total, front=0):
    back = total - front - a.shape[1]
    return jnp.pad(a, ((0, 0), (front, back), (0, 0)))


def _round_up(n, mult):
    return -(-n // mult) * mult


def _diff_buckets(tq, tk, q_off, sk_real, skp):
    near_back = -((q_off - (T5_MAX_DIST - 1)) // tk - q_off // tk)
    last = (_round_up(q_off + tq, CHUNK) - 1) // tk
    last = min(last, skp // tk - 1)
    n_near = last - (q_off // tk - near_back) + 1
    i = np.arange(tq)[:, None]
    mats = []
    for r in range(n_near):
        kp = (q_off // tk - near_back + r) * tk + np.arange(tk)[None, :]
        qp = q_off + i
        ok = ((kp >> CHUNK_SHIFT) <= (qp >> CHUNK_SHIFT)) & (kp < sk_real)
        bkt = _t5_bucket(jnp.asarray(kp - qp, jnp.int32))
        mats.append(jnp.where(jnp.asarray(ok), bkt, -1))
    return jnp.stack(mats).astype(jnp.int32), near_back


def _trunk(x, q_off, caches, prm, cfg):
    b, sq, _ = x.shape
    m = b * sq
    tq, tk, tm, tf = cfg["tq"], cfg["tk"], cfg["tm"], cfg["tf"]
    sk_real = q_off + sq
    skp = _round_up(sk_real, tk)
    pos = q_off + jnp.arange(sq, dtype=jnp.int32)
    tabs = {k: jnp.tile(v, (b, 1)) for k, v in _rope_tables(pos).items()}
    x2 = x.reshape(m, D_MODEL)

    def with_past(past, new, dtype):
        new = new.reshape(b, sq, -1)
        if past is None:
            return new.astype(dtype)
        return jnp.concatenate([past.reshape(b, past.shape[1], -1).astype(dtype), new.astype(dtype)], axis=1)

    ew = prm["even"]
    qext, ckv, kr, dq, dk, dkb, dv, dvb = _even_proj(x2, prm["norm_mix"][0:1], ew, tabs, tm)
    past = (None,) * 4 if caches is None else tuple(c[0] for c in caches[:4])
    ckv_all = _pad_rows(with_past(past[0], ckv, F32), skp)
    kr_all = _pad_rows(with_past(past[1], kr, F32), skp)
    kext, vext = _kv_up(ckv_all.reshape(b * skp, -1), kr_all.reshape(b * skp, -1), ew, cfg["tm_kv"])
    kext = kext.reshape(b, skp, -1)
    vext = vext.reshape(b, skp, -1)
    tqs, n_sub = cfg["tqs"], cfg["n_sub"]
    assert n_sub == 1 or tqs == tk
    n_diag = (_round_up(q_off + tqs * n_sub, CHUNK) - 1) // tk - q_off // tk + 1
    tiles = dict(tqs=tqs, n_sub=n_sub, n_diag=n_diag, tkw=cfg["tkw"], tkn=tk, q_off=q_off)
    o_mla = _mla_attn(qext.reshape(b, sq, -1), kext, vext, hs=cfg["mla_heads"], kv_buffers=cfg["mla_kv_buffers"],
                      sk_real=sk_real, **tiles)
    def keys_values(past, new, total):
        return _pad_rows(with_past(past, new, BF16), total)

    dk_all = keys_values(past[2], dkb, skp)
    past_dv = past[3]
    if past_dv is not None:
        flat = past_dv.reshape(b, past_dv.shape[1], -1)
        ones = jnp.ones(flat.shape[:2] + (DIFF_V,), BF16)
        past_dv = jnp.concatenate(
            [piece for hh in range(DIFF_HEADS)
             for piece in (flat[..., hh * DIFF_V:(hh + 1) * DIFF_V].astype(BF16), ones)], axis=-1)
    dv_all = keys_values(past_dv, dvb, skp)
    bkt, near_back = _diff_buckets(tqs, tk, q_off, sk_real, skp)
    assert bkt.shape[0] == near_back + n_diag - (n_sub - 1)
    lam_init = 0.8 - 0.6 * math.exp(-0.3 * 0)
    o_diff = _diff_attn(prm["t5"], dq.reshape(b, sq, -1), dk_all, dv_all, bkt, prm["lam_vecs"], prm["subln"],
                        hs=cfg["diff_heads"], kv_buffers=cfg["kv_buffers"], near_back=near_back,
                        far_bucket=T5_BUCKETS // 2 - 1,
                        lam_init=lam_init, **tiles)
    x2 = _post(x2, o_mla.reshape(m, -1), o_diff.reshape(m, -1), ew["wo_mla"], ew["wo_diff"],
               prm["norm_ff"][0:1], prm["w_ff1"][0], prm["w_ff2"][0], prm["final_norm"],
               tm=cfg["tm_post"], tf=tf, final_norm=False)
    new_even = (ckv.reshape(1, b, sq, MLA_KV_RANK), kr.reshape(1, b, sq, MLA_ROPE),
                dk.reshape(1, b, sq, DIFF_HEADS, 2 * DIFF_QK), dv.reshape(1, b, sq, DIFF_HEADS, DIFF_V))

    sq_, sk_, skb, sv_, svb, cq, ck, ckb, cv, cvb = _odd_proj(x2, prm["norm_mix"][1:2], prm["w_in_odd"], tm)
    past = (None,) * 4 if caches is None else tuple(c[0] for c in caches[4:])
    sk_all = keys_values(past[0], skb, skp)
    sv_all = keys_values(past[1], svb, skp)
    n_masked = (q_off + tq - 1) // tk - q_off // tk + 1
    pairs = dict(hp=cfg["head_pairs"], kv_buffers=cfg["kv_buffers"])
    o_sb = _sb_attn(sq_.reshape(b, sq, -1), sk_all, sv_all, prm["tri"][tk], tq=tq, tk=tk, q_off=q_off,
                    n_masked=n_masked, **pairs)
    win_real = tq + CA_BAND
    win = _round_up(win_real, LANES)
    if caches is None:
        ck_all = _pad_rows(ckb.reshape(b, sq, -1), sq + CA_BAND + win - win_real, front=CA_BAND)
        cv_all = _pad_rows(cvb.reshape(b, sq, -1), sq + CA_BAND + win - win_real, front=CA_BAND)
    else:
        ck_all = keys_values(past[2], ckb, win)
        cv_all = keys_values(past[3], cvb, win)
    o_ca = _ca_attn(prm["ca_tab_t"], cq.reshape(b, sq, -1), ck_all, cv_all, tq=tq, win=win, win_real=win_real,
                    q_off=q_off, **pairs)
    x2 = _post(x2, o_sb.reshape(m, -1), o_ca.reshape(m, -1), prm["wo_sb"], prm["wo_ca"],
               prm["norm_ff"][1:2], prm["w_ff1"][1], prm["w_ff2"][1], prm["final_norm"],
               tm=cfg["tm_post"], tf=tf, final_norm=True)

    heads = lambda a: a.reshape(b, sq, SB_HEADS, SB_DIM)
    if caches is None:
        nb = min(CA_BAND, sq)
        cak, cav = heads(ck)[:, sq - nb:], heads(cv)[:, sq - nb:]
    else:
        nb = past[2].shape[1]
        cak = jnp.concatenate([past[2], heads(ck)], axis=1)[:, sq:]
        cav = jnp.concatenate([past[3], heads(cv)], axis=1)[:, sq:]
        assert cak.shape[1] == nb
    new_odd = (heads(sk_)[None], heads(sv_)[None], cak[None], cav[None])
    return x2.reshape(b, sq, D_MODEL), new_even + new_odd


def _tri(tk):
    j = np.arange(tk)[:, None]
    s = np.arange(tk)[None, :]
    return jnp.asarray((j > s).astype(np.float32), BF16)


def kernel(x_prompt, x_sample, cache_mla_ckv, cache_mla_krope, cache_diff_k, cache_diff_v, cache_sb_k, cache_sb_v, cache_ca_k, cache_ca_v, norm_mix, norm_ff, w_in_even, mla_q_norm, mla_kv_norm, mla_w_uq, mla_w_ukv, diff_lambda_vecs, diff_subln, t5_bias, w_out_even, w_in_odd, ca_rel_bias, w_out_odd, w_ff1, w_ff2, final_norm):
    seq = x_prompt.shape[1]
    dec_seq = x_sample.shape[1]
    past_len = cache_mla_ckv.shape[2]
    assert cache_ca_k.shape[2] == CA_BAND and past_len % CHUNK == 0

    cfg_p = {"tq": 256, "tqs": 256, "n_sub": 4, "tk": 256, "tkw": (2048, 1024), "mla_heads": 1, "diff_heads": 1,
             "head_pairs": 2, "kv_buffers": 1, "mla_kv_buffers": None,
             "tm": min(512, seq), "tm_kv": min(512, seq), "tm_post": min(1024, seq), "tf": 1024}
    rows_s = x_sample.shape[0] * dec_seq
    cfg_s = {"tq": dec_seq, "tqs": dec_seq, "n_sub": 1, "tk": 128, "tkw": (512,), "mla_heads": MLA_HEADS,
             "diff_heads": DIFF_HEADS, "head_pairs": SB_HEADS // 2, "kv_buffers": None, "mla_kv_buffers": None,
             "tm": rows_s, "tm_kv": x_sample.shape[0] * 128 // 2, "tm_post": rows_s,
             "tf": 512}
    n_sb = SB_HEADS * SB_DIM
    prm = {
        "norm_mix": norm_mix, "norm_ff": norm_ff, "final_norm": final_norm.reshape(1, -1),
        "even": _even_weights(w_in_even[0], mla_q_norm[0], mla_kv_norm[0], mla_w_uq[0], mla_w_ukv[0],
                              w_out_even[0]),
        "t5": t5_bias, "lam_vecs": diff_lambda_vecs[0], "subln": diff_subln[0].reshape(1, -1),
        "w_in_odd": w_in_odd[0].astype(BF16), "ca_tab_t": ca_rel_bias[0].T,
        "wo_sb": w_out_odd[0][:n_sb].astype(BF16), "wo_ca": w_out_odd[0][n_sb:].astype(BF16),
        "w_ff1": w_ff1.astype(BF16), "w_ff2": w_ff2.astype(BF16),
        "tri": {tk: _tri(tk) for tk in {cfg_p["tk"], cfg_s["tk"]}},
    }
    y_prompt, new_p = _trunk(x_prompt, 0, None, prm, cfg_p)
    caches = (cache_mla_ckv, cache_mla_krope, cache_diff_k, cache_diff_v,
              cache_sb_k, cache_sb_v, cache_ca_k, cache_ca_v)
    y_sample, new_s = _trunk(x_sample, past_len, caches, prm, cfg_s)
    return (y_prompt, y_sample) + tuple(new_p) + tuple(new_s)
```
